```python
import functools
import jax
import jax.numpy as jnp
from jax import lax
import numpy as np


D_MODEL = 2048
BATCH = 4
SEQ = 4096
DEPTH = 2

N_MEM = 256
NORM_EPS = 1e-6

DN_HEADS = 8
DN_DK = 128
DN_DV = 128
DN_CONV = 4
DN_CHUNK = 64
GLA_HEADS = 4
GLA_DK = 128
GLA_DV = 256
GLA_RANK = 16
GLA_TAU = 16.0
GLA_CHUNK = 64
S5_GROUP = 16
S5_GROUPS = 64
S5_WIDTH = S5_GROUP * S5_GROUPS
S5_STATE = 64
XA_HEADS = 4
XA_DH = 128
FFN_HIDDEN = 5632
FFN_CONV = 3

IN_SIZES = (
    DN_HEADS * DN_DK,
    DN_HEADS * DN_DK,
    DN_HEADS * DN_DV,
    DN_HEADS * DN_DV,
    DN_HEADS,
    DN_HEADS,
    GLA_HEADS * GLA_DK,
    GLA_HEADS * GLA_DK,
    GLA_HEADS * GLA_DV,
    GLA_RANK,
    GLA_HEADS * GLA_DV,
    S5_WIDTH,
    3 * D_MODEL,
)
IN_WIDTH = sum(IN_SIZES)

kernel_name = 'hybrid_deltanet_gla_s5_xattn_convffn'


def rms_norm(x, g):
    xf = x.astype(jnp.float32)
    y = xf * lax.rsqrt(jnp.mean(xf * xf, axis=-1, keepdims=True) + NORM_EPS)
    return (y * g.astype(jnp.float32)).astype(x.dtype)


def l2_normalize(x):
    xf = x.astype(jnp.float32)
    return xf * lax.rsqrt(jnp.sum(xf * xf, axis=-1, keepdims=True) + NORM_EPS)


def causal_depthwise_conv(x, w):
    k_w = w.shape[-1]
    s = x.shape[1]
    xp = jnp.pad(x, ((0, 0), (k_w - 1, 0), (0, 0)))
    y = xp[:, 0:s] * w[:, 0]
    for j in range(1, k_w):
        y = y + xp[:, j:j + s] * w[:, j]
    return y


def split_columns(z, sizes):
    idx = []
    acc = 0
    for n in sizes[:-1]:
        acc += n
        idx.append(acc)
    return jnp.split(z, idx, axis=-1)


def to_chunks(t, c):
    b, s, h = t.shape[:3]
    t = t.reshape((b, s // c, c, h) + t.shape[3:])
    t = jnp.moveaxis(t, 3, 2)
    return jnp.moveaxis(t, 1, 0)


def from_chunks(o):
    n, b, h, c, d = o.shape
    return o.transpose(1, 0, 3, 2, 4).reshape(b, n * c, h, d)


def gated_delta_rule(q, k, v, g, beta):
    f32 = jnp.float32
    c = DN_CHUNK
    dk = q.shape[-1]
    q = to_chunks(l2_normalize(q) * (dk ** -0.5), c)
    k = to_chunks(l2_normalize(k), c)
    v = to_chunks(v.astype(f32), c)
    beta = to_chunks(beta.astype(f32), c)
    gc = jnp.cumsum(to_chunks(g.astype(f32), c), axis=-1)
    causal = jnp.tril(jnp.ones((c, c), dtype=bool))
    strict = jnp.tril(jnp.ones((c, c), dtype=bool), k=-1)
    decay = jnp.exp(jnp.where(causal, gc[..., :, None] - gc[..., None, :], -jnp.inf))
    kb = k * beta[..., None]
    lower = jnp.where(strict, jnp.einsum('nbhid,nbhjd->nbhij', kb, k) * decay, 0.0)
    t_mat = lower + jnp.eye(c, dtype=f32)
    solve = functools.partial(lax.linalg.triangular_solve, left_side=True, lower=True, unit_diagonal=True)
    u = solve(t_mat, v * beta[..., None])
    w = solve(t_mat, kb * jnp.exp(gc)[..., None])
    a_qk = jnp.where(causal, jnp.einsum('nbhid,nbhjd->nbhij', q, k) * decay, 0.0)
    q_dec = q * jnp.exp(gc)[..., None]
    k_dec = k * jnp.exp(gc[..., -1:] - gc)[..., None]
    g_last = jnp.exp(gc[..., -1])

    def step(state, inp):
        a_c, qd, kd, wc, uc, gl = inp
        v_new = uc - jnp.einsum('bhcd,bhde->bhce', wc, state)
        o = jnp.einsum('bhcd,bhde->bhce', qd, state) + jnp.einsum('bhij,bhje->bhie', a_c, v_new)
        state = state * gl[..., None, None] + jnp.einsum('bhcd,bhce->bhde', kd, v_new)
        return state, o

    s0 = jnp.zeros(q.shape[1:3] + (dk, v.shape[-1]), f32)
    _, o = lax.scan(step, s0, (a_qk, q_dec, k_dec, w, u, g_last))
    return from_chunks(o)


def gla_chunked(q, k, v, log_a):
    f32 = jnp.float32
    c = GLA_CHUNK
    dk = q.shape[-1]
    q = to_chunks(q.astype(f32) * (dk ** -0.5), c)
    k = to_chunks(k.astype(f32), c)
    v = to_chunks(v.astype(f32), c)
    b = jnp.cumsum(to_chunks(log_a.astype(f32), c), axis=-2)
    q_dec = q * jnp.exp(b)
    k_dec = k * jnp.exp(b[..., -1:, :] - b)
    a_last = jnp.exp(b[..., -1, :])
    causal = jnp.tril(jnp.ones((c, c), dtype=bool))[:, :, None]

    def step(state, inp):
        qc, kc, vc, bc, qd, kd, al = inp
        dec = jnp.exp(jnp.where(causal, bc[:, :, :, None, :] - bc[:, :, None, :, :], -jnp.inf))
        a_qk = jnp.einsum('bhijd,bhjd->bhij', qc[:, :, :, None, :] * dec, kc)
        o = jnp.einsum('bhid,bhde->bhie', qd, state) + jnp.einsum('bhij,bhje->bhie', a_qk, vc)
        state = state * al[..., :, None] + jnp.einsum('bhcd,bhce->bhde', kd, vc)
        return state, o

    s0 = jnp.zeros(q.shape[1:3] + (dk, v.shape[-1]), f32)
    _, o = lax.scan(step, s0, (q, k, v, b, q_dec, k_dec, a_last))
    return from_chunks(o)


def s5_ssm(u, lam_re, lam_im, log_step, b_re, b_im, c_re, c_im, d_skip):
    f32 = jnp.float32
    bsz, s, _ = u.shape
    uf = u.astype(f32).reshape(bsz, s, S5_GROUPS, S5_GROUP)
    lr = jnp.minimum(lam_re.astype(f32), -1e-4)
    li = lam_im.astype(f32)
    dt = jnp.exp(log_step.astype(f32))[:, None]
    mag = jnp.exp(lr * dt)
    ar = mag * jnp.cos(li * dt)
    ai = mag * jnp.sin(li * dt)
    nr = ar - 1.0
    den = lr * lr + li * li
    zr = (nr * lr + ai * li) / den
    zi = (ai * lr - nr * li) / den
    br = b_re.astype(f32)
    bi = b_im.astype(f32)
    bbr = zr[..., None] * br - zi[..., None] * bi
    bbi = zr[..., None] * bi + zi[..., None] * br
    bu_r = jnp.einsum('bsgh,gph->sbgp', uf, bbr)
    bu_i = jnp.einsum('bsgh,gph->sbgp', uf, bbi)
    a_r = jnp.broadcast_to(ar, (s, 1) + ar.shape)
    a_i = jnp.broadcast_to(ai, (s, 1) + ai.shape)

    def combine(e1, e2):
        a1r, a1i, b1r, b1i = e1
        a2r, a2i, b2r, b2i = e2
        return (a2r * a1r - a2i * a1i,
                a2r * a1i + a2i * a1r,
                a2r * b1r - a2i * b1i + b2r,
                a2r * b1i + a2i * b1r + b2i)

    _, _, xr, xi = lax.associative_scan(combine, (a_r, a_i, bu_r, bu_i), axis=0)
    y = (jnp.einsum('sbgp,ghp->bsgh', xr, c_re.astype(f32))
         - jnp.einsum('sbgp,ghp->bsgh', xi, c_im.astype(f32)))
    y = y + d_skip.astype(f32).reshape(S5_GROUPS, S5_GROUP) * uf
    return y.reshape(bsz, s, S5_WIDTH).astype(u.dtype)


def hybrid_mixer(h, w_in, dn_conv_w, dn_a_log, dn_dt_bias, dn_norm_w, w_dn_out,
                 gla_w_up, gla_b_up, gla_norm_w, w_gla_out,
                 s5_lam_re, s5_lam_im, s5_log_step, s5_b_re, s5_b_im, s5_c_re, s5_c_im, s5_d,
                 w_s5_glu, w_mix_out):
    bsz, s, _ = h.shape
    z = h @ w_in
    (dq, dkk, dv, dz, db, da, gq, gk, gv, glow, gr, su, gates) = split_columns(z, IN_SIZES)

    qkv = jax.nn.silu(causal_depthwise_conv(jnp.concatenate([dq, dkk, dv], axis=-1), dn_conv_w))
    dq, dkk, dv = jnp.split(qkv, [DN_HEADS * DN_DK, 2 * DN_HEADS * DN_DK], axis=-1)
    beta = jax.nn.sigmoid(db.astype(jnp.float32))
    g = -jnp.exp(dn_a_log.astype(jnp.float32)) * jax.nn.softplus(da.astype(jnp.float32) + dn_dt_bias.astype(jnp.float32))
    o_a = gated_delta_rule(dq.reshape(bsz, s, DN_HEADS, DN_DK), dkk.reshape(bsz, s, DN_HEADS, DN_DK),
                           dv.reshape(bsz, s, DN_HEADS, DN_DV), g, beta)
    o_a = rms_norm(o_a, dn_norm_w) * jax.nn.silu(dz.reshape(bsz, s, DN_HEADS, DN_DV).astype(jnp.float32))
    y_a = o_a.reshape(bsz, s, DN_HEADS * DN_DV).astype(h.dtype) @ w_dn_out

    gate_logit = (glow @ gla_w_up + gla_b_up).astype(jnp.float32)
    log_a = jax.nn.log_sigmoid(gate_logit) / GLA_TAU
    o_b = gla_chunked(gq.reshape(bsz, s, GLA_HEADS, GLA_DK), gk.reshape(bsz, s, GLA_HEADS, GLA_DK),
                      gv.reshape(bsz, s, GLA_HEADS, GLA_DV), log_a.reshape(bsz, s, GLA_HEADS, GLA_DK))
    o_b = rms_norm(o_b, gla_norm_w) * jax.nn.silu(gr.reshape(bsz, s, GLA_HEADS, GLA_DV).astype(jnp.float32))
    y_b = o_b.reshape(bsz, s, GLA_HEADS * GLA_DV).astype(h.dtype) @ w_gla_out

    y_s = jax.nn.gelu(s5_ssm(su, s5_lam_re, s5_lam_im, s5_log_step, s5_b_re, s5_b_im, s5_c_re, s5_c_im, s5_d))
    glu_a, glu_b = jnp.split(y_s @ w_s5_glu, 2, axis=-1)
    y_c = glu_a * jax.nn.sigmoid(glu_b)

    g_a, g_b, g_c = jnp.split(jax.nn.sigmoid(gates), 3, axis=-1)
    merged = g_a * y_a + g_b * y_b + g_c * y_c
    return merged @ w_mix_out


def cross_attention(h, mem_n, w_xa_q, w_xa_kv, w_xa_out):
    bsz, s, _ = h.shape
    m = mem_n.shape[1]
    q = (h @ w_xa_q).reshape(bsz, s, XA_HEADS, XA_DH)
    kv = (mem_n @ w_xa_kv).reshape(bsz, m, 2, XA_HEADS, XA_DH)
    k = kv[:, :, 0]
    v = kv[:, :, 1]
    scores = jnp.einsum('bshd,bmhd->bhsm', q, k).astype(jnp.float32) * (XA_DH ** -0.5)
    p = jax.nn.softmax(scores, axis=-1).astype(v.dtype)
    o = jnp.einsum('bhsm,bmhd->bshd', p, v).reshape(bsz, s, XA_HEADS * XA_DH)
    return o @ w_xa_out


def conv_ffn(h, w_ffn_up, ffn_conv_w, w_ffn_down):
    gate, up = jnp.split(h @ w_ffn_up, 2, axis=-1)
    gate = causal_depthwise_conv(gate, ffn_conv_w)
    return (jax.nn.gelu(gate) * up) @ w_ffn_down


def setup_inputs(seed: int = 0) -> dict:
    key = jax.random.key(seed)
    ks = iter(jax.random.split(key, 48))
    f32 = jnp.float32
    L = DEPTH

    def dense(shape, fan_in):
        return jax.random.normal(next(ks), shape, f32) * (fan_in ** -0.5)

    def gain(shape):
        return 1.0 + 0.02 * jax.random.normal(next(ks), shape, f32)

    def log_uniform(shape, lo, hi):
        return jax.random.uniform(next(ks), shape, f32, minval=float(np.log(lo)), maxval=float(np.log(hi)))

    x = jax.random.normal(next(ks), (BATCH, SEQ, D_MODEL), f32)
    mem = jax.random.normal(next(ks), (BATCH, N_MEM, D_MODEL), f32)
    dt_init = jnp.exp(log_uniform((L, DN_HEADS), 1e-3, 1e-1))
    lam_im = jnp.broadcast_to(jnp.pi * jnp.arange(S5_STATE, dtype=f32), (L, S5_GROUPS, S5_STATE))
    return {
        'x': x,
        'mem': mem,
        'norm_mix_pre': gain((L, D_MODEL)),
        'norm_mix_post': gain((L, D_MODEL)),
        'w_in': dense((L, D_MODEL, IN_WIDTH), D_MODEL),
        'dn_conv_w': dense((L, 3 * DN_HEADS * DN_DK, DN_CONV), DN_CONV),
        'dn_a_log': jnp.log(jax.random.uniform(next(ks), (L, DN_HEADS), f32, minval=1.0, maxval=16.0)),
        'dn_dt_bias': dt_init + jnp.log(-jnp.expm1(-dt_init)),
        'dn_norm_w': gain((L, DN_DV)),
        'w_dn_out': dense((L, DN_HEADS * DN_DV, D_MODEL), DN_HEADS * DN_DV),
        'gla_w_up': dense((L, GLA_RANK, GLA_HEADS * GLA_DK), GLA_RANK),
        'gla_b_up': 0.1 * jax.random.normal(next(ks), (L, GLA_HEADS * GLA_DK), f32),
        'gla_norm_w': gain((L, GLA_DV)),
        'w_gla_out': dense((L, GLA_HEADS * GLA_DV, D_MODEL), GLA_HEADS * GLA_DV),
        's5_lam_re': -0.5 + 0.01 * jax.random.normal(next(ks), (L, S5_GROUPS, S5_STATE), f32),
        's5_lam_im': lam_im,
        's5_log_step': log_uniform((L, S5_GROUPS), 1e-3, 1e-1),
        's5_b_re': dense((L, S5_GROUPS, S5_STATE, S5_GROUP), 2 * S5_GROUP),
        's5_b_im': dense((L, S5_GROUPS, S5_STATE, S5_GROUP), 2 * S5_GROUP),
        's5_c_re': dense((L, S5_GROUPS, S5_GROUP, S5_STATE), S5_STATE),
        's5_c_im': dense((L, S5_GROUPS, S5_GROUP, S5_STATE), S5_STATE),
        's5_d': jax.random.normal(next(ks), (L, S5_WIDTH), f32),
        'w_s5_glu': dense((L, S5_WIDTH, 2 * D_MODEL), S5_WIDTH),
        'w_mix_out': dense((L, D_MODEL, D_MODEL), D_MODEL),
        'norm_xa_pre': gain((L, D_MODEL)),
        'norm_xa_post': gain((L, D_MODEL)),
        'norm_mem': gain((L, D_MODEL)),
        'w_xa_q': dense((L, D_MODEL, XA_HEADS * XA_DH), D_MODEL),
        'w_xa_kv': dense((L, D_MODEL, 2 * XA_HEADS * XA_DH), D_MODEL),
        'w_xa_out': dense((L, XA_HEADS * XA_DH, D_MODEL), XA_HEADS * XA_DH),
        'norm_ffn_pre': gain((L, D_MODEL)),
        'norm_ffn_post': gain((L, D_MODEL)),
        'w_ffn_up': dense((L, D_MODEL, 2 * FFN_HIDDEN), D_MODEL),
        'ffn_conv_w': dense((L, FFN_HIDDEN, FFN_CONV), FFN_CONV),
        'w_ffn_down': dense((L, FFN_HIDDEN, D_MODEL), FFN_HIDDEN),
    }


def reference(x, mem, norm_mix_pre, norm_mix_post, w_in, dn_conv_w, dn_a_log, dn_dt_bias, dn_norm_w,
              w_dn_out, gla_w_up, gla_b_up, gla_norm_w, w_gla_out, s5_lam_re, s5_lam_im, s5_log_step,
              s5_b_re, s5_b_im, s5_c_re, s5_c_im, s5_d, w_s5_glu, w_mix_out, norm_xa_pre, norm_xa_post,
              norm_mem, w_xa_q, w_xa_kv, w_xa_out, norm_ffn_pre, norm_ffn_post, w_ffn_up, ffn_conv_w,
              w_ffn_down):
    for l in range(DEPTH):
        h = rms_norm(x, norm_mix_pre[l])
        y = hybrid_mixer(h, w_in[l], dn_conv_w[l], dn_a_log[l], dn_dt_bias[l], dn_norm_w[l], w_dn_out[l],
                         gla_w_up[l], gla_b_up[l], gla_norm_w[l], w_gla_out[l],
                         s5_lam_re[l], s5_lam_im[l], s5_log_step[l], s5_b_re[l], s5_b_im[l],
                         s5_c_re[l], s5_c_im[l], s5_d[l], w_s5_glu[l], w_mix_out[l])
        x = x + rms_norm(y, norm_mix_post[l])
        h = rms_norm(x, norm_xa_pre[l])
        y = cross_attention(h, rms_norm(mem, norm_mem[l]), w_xa_q[l], w_xa_kv[l], w_xa_out[l])
        x = x + rms_norm(y, norm_xa_post[l])
        h = rms_norm(x, norm_ffn_pre[l])
        y = conv_ffn(h, w_ffn_up[l], ffn_conv_w[l], w_ffn_down[l])
        x = x + rms_norm(y, norm_ffn_post[l])
    return x
```

```python
import functools

import jax
import jax.numpy as jnp
from jax import lax
from jax.experimental import pallas as pl
from jax.experimental.pallas import tpu as pltpu

F32 = jnp.float32
BF16 = jnp.bfloat16

NORM_EPS = 1e-6

D_MODEL = 2048
DN_HEADS, DN_DK, DN_DV, DN_CONV = 8, 128, 128, 4
GLA_HEADS, GLA_DK, GLA_DV, GLA_RANK, GLA_TAU = 4, 128, 256, 16, 16.0
S5_GROUP, S5_GROUPS, S5_STATE = 16, 64, 64
S5_WIDTH = S5_GROUP * S5_GROUPS
XA_HEADS, XA_DH = 4, 128
FFN_HIDDEN, FFN_CONV = 5632, 3

CHUNK = 64
GLA_SUB = 16
S5_L = 16
S5_PAIR = 2
LANE = 128
SUBLANE = 8
VMEM_LIMIT = 56 * 1024 * 1024

_OFF_DQ, _OFF_DK, _OFF_DV, _OFF_DZ = 0, 1024, 2048, 3072
_OFF_GQ, _OFF_GK, _OFF_GV, _OFF_GR = 4096, 4608, 5120, 6144
_OFF_SU, _OFF_GATES = 7168, 8192
_BIG_WIDTH = 14336
_SM_DB, _SM_DA, _SM_GLOW = 0, 8, 16


def _cparams(sem):
    return pltpu.CompilerParams(dimension_semantics=sem, vmem_limit_bytes=VMEM_LIMIT)


def _rms(xf, g):
    return xf * lax.rsqrt(jnp.mean(xf * xf, axis=-1, keepdims=True) + NORM_EPS) * g


def _dot(a, b):
    return jnp.dot(a.astype(BF16), b.astype(BF16), preferred_element_type=F32)


def _dot_nt(a, b):
    return lax.dot_general(a.astype(BF16), b.astype(BF16), (((1,), (1,)), ((), ())),
                           preferred_element_type=F32)


def _dot_tn(a, b):
    return lax.dot_general(a.astype(BF16), b.astype(BF16), (((0,), (0,)), ((), ())),
                           preferred_element_type=F32)


def _sigmoid(x):
    return 1.0 / (1.0 + jnp.exp(-x))


def _silu(x):
    return x * _sigmoid(x)


def _softplus(x):
    return jnp.maximum(x, 0.0) + jnp.log(1.0 + jnp.exp(-jnp.abs(x)))


def _gelu_tanh(x):
    c = 0.7978845608028654
    return 0.5 * x * (1.0 + jnp.tanh(c * (x + 0.044715 * (x * x * x))))


def _in_proj_kernel(x_ref, g_ref, w_ref, ws_ref, o_ref, os_ref, hn_ref):
    @pl.when(pl.program_id(1) == 0)
    def _():
        hn = _rms(x_ref[...], g_ref[...]).astype(BF16)
        hn_ref[...] = hn
        os_ref[...] = jnp.dot(hn, ws_ref[...], preferred_element_type=F32)

    o_ref[...] = jnp.dot(hn_ref[...], w_ref[...], preferred_element_type=F32).astype(o_ref.dtype)


def _in_proj(x2, g, w_big, w_small, tm=1024, tn=512):
    t, d = x2.shape
    n = w_big.shape[1]
    ns = w_small.shape[1]
    return pl.pallas_call(
        _in_proj_kernel,
        out_shape=(jax.ShapeDtypeStruct((t, n), BF16), jax.ShapeDtypeStruct((t, ns), F32)),
        grid=(t // tm, n // tn),
        in_specs=[
            pl.BlockSpec((tm, d), lambda i, j: (i, 0)),
            pl.BlockSpec((1, d), lambda i, j: (0, 0)),
            pl.BlockSpec((d, tn), lambda i, j: (0, j)),
            pl.BlockSpec((d, ns), lambda i, j: (0, 0)),
        ],
        out_specs=(
            pl.BlockSpec((tm, tn), lambda i, j: (i, j)),
            pl.BlockSpec((tm, ns), lambda i, j: (i, 0)),
        ),
        scratch_shapes=[pltpu.VMEM((tm, d), BF16)],
        compiler_params=_cparams(("parallel", "arbitrary")),
        name="in_proj",
    )(x2, g, w_big, w_small)


def _norm_proj_kernel(x_ref, g_ref, w_ref, o_ref):
    hn = _rms(x_ref[...], g_ref[...]).astype(BF16)
    o_ref[...] = jnp.dot(hn, w_ref[...], preferred_element_type=F32).astype(o_ref.dtype)


def _norm_proj(x2, g, w, tm=512):
    t, d = x2.shape
    n = w.shape[1]
    return pl.pallas_call(
        _norm_proj_kernel,
        out_shape=jax.ShapeDtypeStruct((t, n), BF16),
        grid=(t // tm,),
        in_specs=[
            pl.BlockSpec((tm, d), lambda i: (i, 0)),
            pl.BlockSpec((1, d), lambda i: (0, 0)),
            pl.BlockSpec((d, n), lambda i: (0, 0)),
        ],
        out_specs=pl.BlockSpec((tm, n), lambda i: (i, 0)),
        compiler_params=_cparams(("parallel",)),
        name="mem_kv_proj",
    )(x2, g, w)


def _row_to_col(row, eye):
    return jnp.sum(jnp.where(eye, row, 0.0), axis=1, keepdims=True)


def _deltanet_kernel(alog_ref, dtb_ref, q_ref, k_ref, v_ref, zg_ref, gate_ref,
                     cwq_ref, cwk_ref, cwv_ref, nw_ref, o_ref, state_ref, tail_ref, ext_ref):
    h = pl.program_id(1)
    tt = q_ref.shape[0]
    c = CHUNK
    halo = SUBLANE

    @pl.when(pl.program_id(2) == 0)
    def _():
        state_ref[...] = jnp.zeros_like(state_ref)
        tail_ref[...] = jnp.zeros_like(tail_ref)

    def conv_silu(idx, raw_ref, cw_ref):
        raw = raw_ref[...].astype(F32)
        ext_ref[idx, 0:halo, :] = tail_ref[idx]
        ext_ref[idx, halo:halo + tt, :] = raw
        tail_ref[idx] = raw[tt - halo:tt, :]
        cw = cw_ref[...]
        y = ext_ref[idx, halo - (DN_CONV - 1):halo - (DN_CONV - 1) + tt, :] * cw[0:1, :]
        for j in range(1, DN_CONV):
            off = halo - (DN_CONV - 1) + j
            y = y + ext_ref[idx, off:off + tt, :] * cw[j:j + 1, :]
        return _silu(y)

    q = conv_silu(0, q_ref, cwq_ref)
    k = conv_silu(1, k_ref, cwk_ref)
    v = conv_silu(2, v_ref, cwv_ref)
    q = q * (lax.rsqrt(jnp.sum(q * q, axis=-1, keepdims=True) + NORM_EPS) * (DN_DK ** -0.5))
    k = k * lax.rsqrt(jnp.sum(k * k, axis=-1, keepdims=True) + NORM_EPS)

    neg_a = -jnp.exp(jnp.full((1, c), alog_ref[h], F32))
    dtb = jnp.full((1, c), dtb_ref[h], F32)

    ri = lax.broadcasted_iota(jnp.int32, (c, c), 0)
    ci = lax.broadcasted_iota(jnp.int32, (c, c), 1)
    eye = ri == ci
    causal = ri >= ci
    strict = ri > ci
    ident = jnp.where(eye, 1.0, 0.0).astype(F32)

    nw = nw_ref[...]
    gates = gate_ref[...]
    for n in range(tt // c):
        sl = slice(n * c, (n + 1) * c)
        qc, kc, vc = q[sl], k[sl], v[sl]
        beta_row = _sigmoid(gates[0:1, sl])
        g_row = neg_a * _softplus(gates[1:2, sl] + dtb)
        beta_col = _row_to_col(beta_row, eye)
        g_col = _row_to_col(g_row, eye)
        gc_col = jnp.sum(jnp.where(causal, g_row, 0.0), axis=1, keepdims=True)
        gc_row = jnp.sum(jnp.where(ri <= ci, g_col, 0.0), axis=0, keepdims=True)
        gc_last = jnp.sum(g_row, axis=1, keepdims=True)
        decay = jnp.where(causal, jnp.exp(jnp.where(causal, gc_col - gc_row, 0.0)), 0.0)

        kb = kc * beta_col
        lower = jnp.where(strict, _dot_nt(kb, kc) * decay, 0.0)
        inv = ident - lower
        pw = _dot(lower, lower)
        span = 2
        while span < c:
            inv = inv + _dot(inv, pw)
            span *= 2
            if span < c:
                pw = _dot(pw, pw)
        e_col = jnp.exp(gc_col)
        u = _dot(inv, vc * beta_col)
        w = _dot(inv, kb * e_col)
        a_qk = jnp.where(causal, _dot_nt(qc, kc) * decay, 0.0)
        q_dec = qc * e_col
        k_dec = kc * jnp.exp(gc_last - gc_col)
        g_last = jnp.exp(gc_last)

        state = state_ref[...]
        v_new = u - _dot(w, state)
        o = _dot(q_dec, state) + _dot(a_qk, v_new)
        state_ref[...] = state * g_last + _dot_tn(k_dec, v_new)

        o = _rms(o, nw) * _silu(zg_ref[sl, :].astype(F32))
        o_ref[sl, :] = o.astype(o_ref.dtype)


def _deltanet(z3, gates_t, dn_a_log, dn_dt_bias, conv_w_t, norm_w, tt=256):
    b, s, _ = z3.shape
    hq, hk, hv, hz = (_OFF_DQ // DN_DK, _OFF_DK // DN_DK, _OFF_DV // DN_DV, _OFF_DZ // DN_DV)
    nh = DN_HEADS
    smem = pl.BlockSpec(memory_space=pltpu.SMEM)
    return pl.pallas_call(
        _deltanet_kernel,
        out_shape=jax.ShapeDtypeStruct((b, s, nh * DN_DV), BF16),
        grid=(b, nh, s // tt),
        in_specs=[
            smem, smem,
            pl.BlockSpec((None, tt, DN_DK), lambda bi, h, i: (bi, i, hq + h)),
            pl.BlockSpec((None, tt, DN_DK), lambda bi, h, i: (bi, i, hk + h)),
            pl.BlockSpec((None, tt, DN_DV), lambda bi, h, i: (bi, i, hv + h)),
            pl.BlockSpec((None, tt, DN_DV), lambda bi, h, i: (bi, i, hz + h)),
            pl.BlockSpec((None, None, 2, tt), lambda bi, h, i: (bi, h, 0, i)),
            pl.BlockSpec((DN_CONV, DN_DK), lambda bi, h, i: (0, h)),
            pl.BlockSpec((DN_CONV, DN_DK), lambda bi, h, i: (0, nh + h)),
            pl.BlockSpec((DN_CONV, DN_DV), lambda bi, h, i: (0, 2 * nh + h)),
            pl.BlockSpec((1, DN_DV), lambda bi, h, i: (0, 0)),
        ],
        out_specs=pl.BlockSpec((None, tt, DN_DV), lambda bi, h, i: (bi, i, h)),
        scratch_shapes=[
            pltpu.VMEM((DN_DK, DN_DV), F32),
            pltpu.VMEM((3, SUBLANE, DN_DK), F32),
            pltpu.VMEM((3, tt + SUBLANE, DN_DK), F32),
        ],
        compiler_params=_cparams(("parallel", "parallel", "arbitrary")),
        name="deltanet",
    )(dn_a_log, dn_dt_bias, z3, z3, z3, z3, gates_t, conv_w_t, conv_w_t, conv_w_t, norm_w)


def _gla_kernel(q_ref, k_ref, v_ref, r_ref, zs_ref, wup_ref, bup_ref, nw_ref, o_ref, state_ref):
    tt = q_ref.shape[0]
    c = CHUNK
    sub = GLA_SUB

    @pl.when(pl.program_id(2) == 0)
    def _():
        state_ref[...] = jnp.zeros_like(state_ref)

    logit = _dot(zs_ref[...], wup_ref[...]) + bup_ref[...]
    log_a = -_softplus(-logit) * (1.0 / GLA_TAU)

    ri = lax.broadcasted_iota(jnp.int32, (c, c), 0)
    ci = lax.broadcasted_iota(jnp.int32, (c, c), 1)
    tril = jnp.where(ri >= ci, 1.0, 0.0).astype(BF16)
    sub_rows = lax.broadcasted_iota(jnp.int32, (sub, 1), 0)
    key_rows = lax.broadcasted_iota(jnp.int32, (c, 1), 0)
    blk_cols = lax.broadcasted_iota(jnp.int32, (sub, c), 1)

    nw = nw_ref[...]
    scale = GLA_DK ** -0.5
    for n in range(tt // c):
        sl = slice(n * c, (n + 1) * c)
        la = log_a[sl]
        la_hi = la.astype(BF16)
        la_lo = (la - la_hi.astype(F32)).astype(BF16)
        bcum = (jnp.dot(tril, la_hi, preferred_element_type=F32)
                + jnp.dot(tril, la_lo, preferred_element_type=F32))
        qc = q_ref[sl, :].astype(F32) * scale
        kc = k_ref[sl, :].astype(F32)
        vc = v_ref[sl, :]
        b_last = bcum[c - 1:c, :]
        q_dec = qc * jnp.exp(bcum)
        k_dec = kc * jnp.exp(b_last - bcum)
        a_last = jnp.exp(b_last)

        row_blocks = []
        for bi in range(c // sub):
            r0 = bi * sub
            q_i = qc[r0:r0 + sub]
            b_i = bcum[r0:r0 + sub]
            if bi > 0:
                b_ref = bcum[r0 - 1:r0, :]
                early = key_rows < r0
                q_f = q_i * jnp.exp(b_i - b_ref)
                k_f = jnp.where(early, kc * jnp.exp(jnp.where(early, b_ref - bcum, 0.0)), 0.0)
                blk = _dot_nt(q_f, k_f)
            else:
                blk = jnp.zeros((sub, c), F32)
            for j in range(sub):
                kj = kc[r0 + j:r0 + j + 1, :]
                bj = bcum[r0 + j:r0 + j + 1, :]
                live = sub_rows >= j
                dec = jnp.exp(jnp.where(live, b_i - bj, 0.0))
                col = jnp.sum(jnp.where(live, q_i * kj * dec, 0.0), axis=1, keepdims=True)
                blk = jnp.where(blk_cols == r0 + j, col, blk)
            row_blocks.append(blk)
        a_qk = jnp.concatenate(row_blocks, axis=0)

        state_t = state_ref[...]
        o = _dot_nt(q_dec, state_t) + _dot(a_qk, vc)
        state_ref[...] = state_t * a_last + _dot_tn(vc, k_dec)

        o = _rms(o, nw) * _silu(r_ref[sl, :].astype(F32))
        o_ref[sl, :] = o.astype(o_ref.dtype)


def _gla(z3, zs3, w_up_pad, b_up, norm_w, tt=256):
    b, s, _ = z3.shape
    hq, hk = _OFF_GQ // GLA_DK, _OFF_GK // GLA_DK
    hv, hr = _OFF_GV // GLA_DV, _OFF_GR // GLA_DV
    ns = zs3.shape[-1]
    return pl.pallas_call(
        _gla_kernel,
        out_shape=jax.ShapeDtypeStruct((b, s, GLA_HEADS * GLA_DV), BF16),
        grid=(b, GLA_HEADS, s // tt),
        in_specs=[
            pl.BlockSpec((None, tt, GLA_DK), lambda bi, h, i: (bi, i, hq + h)),
            pl.BlockSpec((None, tt, GLA_DK), lambda bi, h, i: (bi, i, hk + h)),
            pl.BlockSpec((None, tt, GLA_DV), lambda bi, h, i: (bi, i, hv + h)),
            pl.BlockSpec((None, tt, GLA_DV), lambda bi, h, i: (bi, i, hr + h)),
            pl.BlockSpec((None, tt, ns), lambda bi, h, i: (bi, i, 0)),
            pl.BlockSpec((ns, GLA_DK), lambda bi, h, i: (0, h)),
            pl.BlockSpec((1, GLA_DK), lambda bi, h, i: (0, h)),
            pl.BlockSpec((1, GLA_DV), lambda bi, h, i: (0, 0)),
        ],
        out_specs=pl.BlockSpec((None, tt, GLA_DV), lambda bi, h, i: (bi, i, h)),
        scratch_shapes=[pltpu.VMEM((GLA_DV, GLA_DK), F32)],
        compiler_params=_cparams(("parallel", "parallel", "arbitrary")),
        name="gla",
    )(z3, z3, z3, z3, zs3, w_up_pad, b_up, norm_w)


def _s5_scan_kernel(nbatch, u_ref, m_ref, v_ref, w_ref, al_ref, d_ref, o_ref, e_ref, xp_ref):
    rows = u_ref.shape[0]
    half = al_ref.shape[1] // 2
    u = u_ref[...]
    e_ref[...] = jnp.dot(u, v_ref[...], preferred_element_type=F32)
    ar = al_ref[:, 0:half]
    ai = al_ref[:, half:2 * half]
    xr = jnp.zeros((nbatch, half), F32)
    xi = jnp.zeros((nbatch, half), F32)
    for n in range(rows // nbatch):
        r0 = n * nbatch
        xp_ref[r0:r0 + nbatch, 0:half] = xr
        xp_ref[r0:r0 + nbatch, half:2 * half] = xi
        er = e_ref[r0:r0 + nbatch, 0:half]
        ei = e_ref[r0:r0 + nbatch, half:2 * half]
        xr, xi = ar * xr - ai * xi + er, ar * xi + ai * xr + ei
    y = (jnp.dot(u, m_ref[...], preferred_element_type=F32)
         + jnp.dot(xp_ref[...].astype(BF16), w_ref[...], preferred_element_type=F32)
         + d_ref[...] * u.astype(F32))
    o_ref[...] = _gelu_tanh(y).astype(o_ref.dtype)


def _s5(u_pairs, m_pair, v_pair, w_pair, al_pair, d_pair, nbatch):
    npair, rows, width = u_pairs.shape
    nstate = v_pair.shape[-1]
    return pl.pallas_call(
        functools.partial(_s5_scan_kernel, nbatch),
        out_shape=jax.ShapeDtypeStruct((npair, rows, width), BF16),
        grid=(npair,),
        in_specs=[
            pl.BlockSpec((None, rows, width), lambda g: (g, 0, 0)),
            pl.BlockSpec((None, width, width), lambda g: (g, 0, 0)),
            pl.BlockSpec((None, width, nstate), lambda g: (g, 0, 0)),
            pl.BlockSpec((None, nstate, width), lambda g: (g, 0, 0)),
            pl.BlockSpec((None, 1, nstate), lambda g: (g, 0, 0)),
            pl.BlockSpec((None, 1, width), lambda g: (g, 0, 0)),
        ],
        out_specs=pl.BlockSpec((None, rows, width), lambda g: (g, 0, 0)),
        scratch_shapes=[pltpu.VMEM((rows, nstate), F32), pltpu.VMEM((rows, nstate), F32)],
        compiler_params=_cparams(("parallel",)),
        name="s5",
    )(u_pairs, m_pair, v_pair, w_pair, al_pair, d_pair)


def _s5_matrices(lam_re, lam_im, log_step, b_re, b_im, c_re, c_im, d_skip):
    hi = lax.Precision.HIGHEST
    g, p, hh, ll = S5_GROUPS, S5_STATE, S5_GROUP, S5_L
    lr = jnp.minimum(lam_re.astype(F32), -1e-4)
    li = lam_im.astype(F32)
    dt = jnp.exp(log_step.astype(F32))[:, None]
    mag = jnp.exp(lr * dt)
    ar = mag * jnp.cos(li * dt)
    ai = mag * jnp.sin(li * dt)
    nr = ar - 1.0
    den = lr * lr + li * li
    zr = (nr * lr + ai * li) / den
    zi = (ai * lr - nr * li) / den
    br = b_re.astype(F32)
    bi = b_im.astype(F32)
    bbr = zr[..., None] * br - zi[..., None] * bi
    bbi = zr[..., None] * bi + zi[..., None] * br
    pr = [jnp.ones_like(ar)]
    pi = [jnp.zeros_like(ai)]
    for _ in range(ll):
        pr.append(pr[-1] * ar - pi[-1] * ai)
        pi.append(pr[-2] * ai + pi[-1] * ar)
    pr = jnp.stack(pr, axis=0)
    pi = jnp.stack(pi, axis=0)
    cr = c_re.astype(F32)
    cim = c_im.astype(F32)
    qr = cr[None] * pr[:, :, None, :] - cim[None] * pi[:, :, None, :]
    qi = cr[None] * pi[:, :, None, :] + cim[None] * pr[:, :, None, :]
    kd = (jnp.einsum('dghp,gpk->dghk', qr, bbr, precision=hi)
          - jnp.einsum('dghp,gpk->dghk', qi, bbi, precision=hi))
    si = jnp.arange(ll)[:, None]
    ii = jnp.arange(ll)[None, :]
    dd = jnp.clip(ii - si, 0, ll)
    toep = jnp.where((ii >= si)[:, :, None, None, None], kd[dd], 0.0)
    m = toep.transpose(2, 0, 4, 1, 3).reshape(g, ll * hh, ll * hh)
    vr = pr[ll - 1::-1][:ll, :, :, None] * bbr[None] - pi[ll - 1::-1][:ll, :, :, None] * bbi[None]
    vi = pr[ll - 1::-1][:ll, :, :, None] * bbi[None] + pi[ll - 1::-1][:ll, :, :, None] * bbr[None]
    vr = vr.transpose(1, 0, 3, 2).reshape(g, ll * hh, p)
    vi = vi.transpose(1, 0, 3, 2).reshape(g, ll * hh, p)
    wr = qr[1:ll + 1].transpose(1, 3, 0, 2).reshape(g, p, ll * hh)
    wi = -qi[1:ll + 1].transpose(1, 3, 0, 2).reshape(g, p, ll * hh)

    np_ = g // S5_PAIR
    wd = ll * hh

    def pair_diag(a):
        a = a.reshape(np_, S5_PAIR, a.shape[1], a.shape[2])
        z = jnp.zeros_like(a[:, 0])
        top = jnp.concatenate([a[:, 0], z], axis=2)
        bot = jnp.concatenate([z, a[:, 1]], axis=2)
        return jnp.concatenate([top, bot], axis=1)

    m_pair = pair_diag(m)
    v_pair = jnp.concatenate([pair_diag(vr), pair_diag(vi)], axis=2)
    w_pair = jnp.concatenate([pair_diag(wr), pair_diag(wi)], axis=1)
    al_pair = jnp.concatenate([pr[ll].reshape(np_, 1, S5_PAIR * p),
                               pi[ll].reshape(np_, 1, S5_PAIR * p)], axis=2)
    d_pair = jnp.broadcast_to(d_skip.astype(F32).reshape(np_, S5_PAIR, 1, hh),
                              (np_, S5_PAIR, ll, hh)).reshape(np_, 1, S5_PAIR * wd)
    return m_pair.astype(BF16), v_pair.astype(BF16), w_pair.astype(BF16), al_pair, d_pair


def _s5_branch(su, mats):
    b, s, _ = su.shape
    nc = s // S5_L
    np_ = S5_GROUPS // S5_PAIR
    u = su.reshape(b, nc, S5_L, np_, S5_PAIR, S5_GROUP)
    u = u.transpose(3, 1, 0, 4, 2, 5).reshape(np_, nc * b, S5_PAIR * S5_L * S5_GROUP)
    y = _s5(u, *mats, nbatch=b)
    y = y.reshape(np_, nc, b, S5_PAIR, S5_L, S5_GROUP).transpose(2, 1, 4, 0, 3, 5)
    return y.reshape(b, s, S5_WIDTH)


def _mix_out_kernel(oa_ref, ob_ref, ys_ref, ga_ref, gb_ref, gc_ref, x_ref,
                    wa_ref, wb_ref, wca_ref, wcb_ref, wm_ref, gpost_ref, o_ref, acc_ref):
    n = pl.program_id(1)
    y_a = jnp.dot(oa_ref[...], wa_ref[...], preferred_element_type=F32)
    y_b = jnp.dot(ob_ref[...], wb_ref[...], preferred_element_type=F32)
    ys = ys_ref[...]
    glu_a = jnp.dot(ys, wca_ref[...], preferred_element_type=F32)
    glu_b = jnp.dot(ys, wcb_ref[...], preferred_element_type=F32)
    y_c = glu_a * _sigmoid(glu_b)
    merged = (_sigmoid(ga_ref[...].astype(F32)) * y_a + _sigmoid(gb_ref[...].astype(F32)) * y_b
              + _sigmoid(gc_ref[...].astype(F32)) * y_c)
    part = jnp.dot(merged.astype(BF16), wm_ref[...], preferred_element_type=F32)

    @pl.when(n == 0)
    def _():
        acc_ref[...] = part

    @pl.when(n > 0)
    def _():
        acc_ref[...] += part

    @pl.when(n == pl.num_programs(1) - 1)
    def _():
        o_ref[...] = x_ref[...] + _rms(acc_ref[...], gpost_ref[...])


def _mix_out(oa, ob, ys, z2, x2, w_dn_out, w_gla_out, w_s5_glu, w_mix_out, g_post, tm=512, tn=512):
    t, d = x2.shape
    nsteps = d // tn
    gblk = _OFF_GATES // tn
    ka, kb, kc = oa.shape[1], ob.shape[1], ys.shape[1]
    return pl.pallas_call(
        _mix_out_kernel,
        out_shape=jax.ShapeDtypeStruct((t, d), F32),
        grid=(t // tm, nsteps),
        in_specs=[
            pl.BlockSpec((tm, ka), lambda i, n: (i, 0)),
            pl.BlockSpec((tm, kb), lambda i, n: (i, 0)),
            pl.BlockSpec((tm, kc), lambda i, n: (i, 0)),
            pl.BlockSpec((tm, tn), lambda i, n: (i, gblk + n)),
            pl.BlockSpec((tm, tn), lambda i, n: (i, gblk + nsteps + n)),
            pl.BlockSpec((tm, tn), lambda i, n: (i, gblk + 2 * nsteps + n)),
            pl.BlockSpec((tm, d), lambda i, n: (i, 0)),
            pl.BlockSpec((ka, tn), lambda i, n: (0, n)),
            pl.BlockSpec((kb, tn), lambda i, n: (0, n)),
            pl.BlockSpec((kc, tn), lambda i, n: (0, n)),
            pl.BlockSpec((kc, tn), lambda i, n: (0, nsteps + n)),
            pl.BlockSpec((tn, d), lambda i, n: (n, 0)),
            pl.BlockSpec((1, d), lambda i, n: (0, 0)),
        ],
        out_specs=pl.BlockSpec((tm, d), lambda i, n: (i, 0)),
        scratch_shapes=[pltpu.VMEM((tm, d), F32)],
        compiler_params=_cparams(("parallel", "arbitrary")),
        name="mix_out",
    )(oa, ob, ys, z2, z2, z2, x2, w_dn_out, w_gla_out, w_s5_glu, w_s5_glu, w_mix_out, g_post)


def _xattn_kernel(x_ref, gpre_ref, wq_ref, kv_ref, wo_ref, gpost_ref, o_ref):
    x = x_ref[...]
    hn = _rms(x, gpre_ref[...]).astype(BF16)
    q = jnp.dot(hn, wq_ref[...], preferred_element_type=F32)
    kv = kv_ref[...]
    outs = []
    for h in range(XA_HEADS):
        qh = q[:, h * XA_DH:(h + 1) * XA_DH]
        kh = kv[:, h * XA_DH:(h + 1) * XA_DH]
        vh = kv[:, (XA_HEADS + h) * XA_DH:(XA_HEADS + h + 1) * XA_DH]
        s = _dot_nt(qh, kh) * (XA_DH ** -0.5)
        s = s - jnp.max(s, axis=-1, keepdims=True)
        p = jnp.exp(s)
        p = p / jnp.sum(p, axis=-1, keepdims=True)
        outs.append(_dot(p, vh))
    o = jnp.concatenate(outs, axis=1)
    y = jnp.dot(o.astype(BF16), wo_ref[...], preferred_element_type=F32)
    o_ref[...] = x + _rms(y, gpost_ref[...])


def _xattn(x3, g_pre, w_q, kv3, w_out, g_post, tm=512):
    b, s, d = x3.shape
    m, nkv = kv3.shape[1], kv3.shape[2]
    nq = w_q.shape[1]
    return pl.pallas_call(
        _xattn_kernel,
        out_shape=jax.ShapeDtypeStruct((b, s, d), F32),
        grid=(b, s // tm),
        in_specs=[
            pl.BlockSpec((None, tm, d), lambda bi, i: (bi, i, 0)),
            pl.BlockSpec((1, d), lambda bi, i: (0, 0)),
            pl.BlockSpec((d, nq), lambda bi, i: (0, 0)),
            pl.BlockSpec((None, m, nkv), lambda bi, i: (bi, 0, 0)),
            pl.BlockSpec((nq, d), lambda bi, i: (0, 0)),
            pl.BlockSpec((1, d), lambda bi, i: (0, 0)),
        ],
        out_specs=pl.BlockSpec((None, tm, d), lambda bi, i: (bi, i, 0)),
        compiler_params=_cparams(("parallel", "parallel")),
        name="xattn",
    )(x3, g_pre, w_q, kv3, w_out, g_post)


def _convffn_kernel(x_ref, xh_ref, gpre_ref, wg_ref, wu_ref, cw_ref, wd_ref, gpost_ref,
                    o_ref, hn_ref, hh_ref, gate_ref, acc_ref):
    i = pl.program_id(1)
    f = pl.program_id(2)
    tm = x_ref.shape[0]
    halo = SUBLANE

    @pl.when(f == 0)
    def _():
        hn_ref[...] = _rms(x_ref[...], gpre_ref[...]).astype(BF16)
        keep = jnp.where(i > 0, 1.0, 0.0)
        hh_ref[...] = (_rms(xh_ref[...], gpre_ref[...]) * keep).astype(BF16)

    hn = hn_ref[...]
    wg = wg_ref[...]
    gate_ref[0:halo, :] = jnp.dot(hh_ref[...], wg, preferred_element_type=F32)
    gate_ref[halo:halo + tm, :] = jnp.dot(hn, wg, preferred_element_type=F32)
    up = jnp.dot(hn, wu_ref[...], preferred_element_type=F32)
    cw = cw_ref[...]
    base = halo - (FFN_CONV - 1)
    conv = gate_ref[base:base + tm, :] * cw[0:1, :]
    for j in range(1, FFN_CONV):
        conv = conv + gate_ref[base + j:base + j + tm, :] * cw[j:j + 1, :]
    act = (_gelu_tanh(conv) * up).astype(BF16)
    part = jnp.dot(act, wd_ref[...], preferred_element_type=F32)

    @pl.when(f == 0)
    def _():
        acc_ref[...] = part

    @pl.when(f > 0)
    def _():
        acc_ref[...] += part

    @pl.when(f == pl.num_programs(2) - 1)
    def _():
        o_ref[...] = x_ref[...] + _rms(acc_ref[...], gpost_ref[...])


def _convffn(x3, g_pre, w_up, conv_w_t, w_down, g_post, tm=512, tf=512):
    b, s, d = x3.shape
    fh = w_down.shape[0]
    nf = fh // tf
    hb = tm // SUBLANE
    return pl.pallas_call(
        _convffn_kernel,
        out_shape=jax.ShapeDtypeStruct((b, s, d), F32),
        grid=(b, s // tm, nf),
        in_specs=[
            pl.BlockSpec((None, tm, d), lambda bi, i, f: (bi, i, 0)),
            pl.BlockSpec((None, SUBLANE, d), lambda bi, i, f: (bi, jnp.maximum(i * hb - 1, 0), 0)),
            pl.BlockSpec((1, d), lambda bi, i, f: (0, 0)),
            pl.BlockSpec((d, tf), lambda bi, i, f: (0, f)),
            pl.BlockSpec((d, tf), lambda bi, i, f: (0, nf + f)),
            pl.BlockSpec((FFN_CONV, tf), lambda bi, i, f: (0, f)),
            pl.BlockSpec((tf, d), lambda bi, i, f: (f, 0)),
            pl.BlockSpec((1, d), lambda bi, i, f: (0, 0)),
        ],
        out_specs=pl.BlockSpec((None, tm, d), lambda bi, i, f: (bi, i, 0)),
        scratch_shapes=[
            pltpu.VMEM((tm, d), BF16),
            pltpu.VMEM((SUBLANE, d), BF16),
            pltpu.VMEM((tm + SUBLANE, tf), F32),
            pltpu.VMEM((tm, d), F32),
        ],
        compiler_params=_cparams(("parallel", "parallel", "arbitrary")),
        name="convffn",
    )(x3, x3, g_pre, w_up, w_up, conv_w_t, w_down, g_post)


def _split_w_in(w_in_l):
    sizes = (DN_HEADS * DN_DK, DN_HEADS * DN_DK, DN_HEADS * DN_DV, DN_HEADS * DN_DV, DN_HEADS,
             DN_HEADS, GLA_HEADS * GLA_DK, GLA_HEADS * GLA_DK, GLA_HEADS * GLA_DV, GLA_RANK,
             GLA_HEADS * GLA_DV, S5_WIDTH, 3 * D_MODEL)
    parts = []
    acc = 0
    for n in sizes:
        parts.append(w_in_l[:, acc:acc + n])
        acc += n
    (dq, dk, dv, dz, db, da, gq, gk, gv, glow, gr, su, gates) = parts
    big = jnp.concatenate([dq, dk, dv, dz, gq, gk, gv, gr, su, gates], axis=1).astype(BF16)
    pad = jnp.zeros((w_in_l.shape[0], LANE - 2 * DN_HEADS - GLA_RANK), w_in_l.dtype)
    small = jnp.concatenate([db, da, glow, pad], axis=1).astype(BF16)
    return big, small


def kernel(x, mem, norm_mix_pre, norm_mix_post, w_in, dn_conv_w, dn_a_log, dn_dt_bias, dn_norm_w, w_dn_out, gla_w_up, gla_b_up, gla_norm_w, w_gla_out, s5_lam_re, s5_lam_im, s5_log_step, s5_b_re, s5_b_im, s5_c_re, s5_c_im, s5_d, w_s5_glu, w_mix_out, norm_xa_pre, norm_xa_post, norm_mem, w_xa_q, w_xa_kv, w_xa_out, norm_ffn_pre, norm_ffn_post, w_ffn_up, ffn_conv_w, w_ffn_down):
    b, s, d = x.shape
    t = b * s
    depth = w_in.shape[0]
    mem2 = mem.reshape(-1, d)
    row = lambda a: a.reshape(1, -1).astype(F32)

    for l in range(depth):
        w_big, w_small = _split_w_in(w_in[l])
        z2, zs2 = _in_proj(x.reshape(t, d), row(norm_mix_pre[l]), w_big, w_small)
        z3 = z2.reshape(b, s, -1)
        zs3 = zs2.reshape(b, s, -1)

        gates_t = zs3[..., :2 * DN_HEADS].reshape(b, s, 2, DN_HEADS).transpose(0, 3, 2, 1)
        oa = _deltanet(z3, gates_t, dn_a_log[l].astype(F32), dn_dt_bias[l].astype(F32),
                       dn_conv_w[l].T.astype(F32), row(dn_norm_w[l]))

        w_up_pad = jnp.zeros((zs2.shape[1], GLA_HEADS * GLA_DK), F32)
        w_up_pad = w_up_pad.at[_SM_GLOW:_SM_GLOW + GLA_RANK].set(gla_w_up[l]).astype(BF16)
        ob = _gla(z3, zs3, w_up_pad, row(gla_b_up[l]), row(gla_norm_w[l]))

        mats = _s5_matrices(s5_lam_re[l], s5_lam_im[l], s5_log_step[l], s5_b_re[l], s5_b_im[l],
                            s5_c_re[l], s5_c_im[l], s5_d[l])
        ys = _s5_branch(z3[..., _OFF_SU:_OFF_SU + S5_WIDTH], mats)

        x = _mix_out(oa.reshape(t, -1), ob.reshape(t, -1), ys.reshape(t, -1), z2, x.reshape(t, d),
                     w_dn_out[l].astype(BF16), w_gla_out[l].astype(BF16), w_s5_glu[l].astype(BF16),
                     w_mix_out[l].astype(BF16), row(norm_mix_post[l])).reshape(b, s, d)

        kv = _norm_proj(mem2, row(norm_mem[l]), w_xa_kv[l].astype(BF16))
        x = _xattn(x, row(norm_xa_pre[l]), w_xa_q[l].astype(BF16), kv.reshape(b, -1, kv.shape[-1]),
                   w_xa_out[l].astype(BF16), row(norm_xa_post[l]))

        x = _convffn(x, row(norm_ffn_pre[l]), w_ffn_up[l].astype(BF16), ffn_conv_w[l].T.astype(F32),
                     w_ffn_down[l].astype(BF16), row(norm_ffn_post[l]))
    return x
```

```python
import jax
import jax.numpy as jnp
from jax import lax
from jax.experimental import pallas as pl
from jax.experimental.pallas import tpu as pltpu

F32 = jnp.float32
BF16 = jnp.bfloat16

NORM_EPS = 1e-6

D_MODEL = 2048
DN_HEADS, DN_DK, DN_DV, DN_CONV = 8, 128, 128, 4
GLA_HEADS, GLA_DK, GLA_DV, GLA_RANK, GLA_TAU = 4, 128, 256, 16, 16.0
S5_GROUP, S5_GROUPS, S5_STATE = 16, 64, 64
S5_WIDTH = S5_GROUP * S5_GROUPS
XA_HEADS, XA_DH = 4, 128
FFN_HIDDEN, FFN_CONV = 5632, 3

CHUNK = 64
GLA_SUB = 16
S5_L = 16
DN_HG = 4
DN_BASE = 8
LANE = 128
S5_LB = LANE // S5_GROUP
SUBLANE = 8
VMEM_LIMIT = 56 * 1024 * 1024

_OFF_DQ, _OFF_DK, _OFF_DV, _OFF_DZ = 0, 1024, 2048, 3072
_OFF_GQ, _OFF_GK, _OFF_GV, _OFF_GR = 4096, 4608, 5120, 6144
_OFF_SU, _OFF_GATES = 7168, 8192
_BIG_WIDTH = 14336
_SM_DB, _SM_DA, _SM_GLOW = 0, 8, 16


def _cparams(sem):
    return pltpu.CompilerParams(dimension_semantics=sem, vmem_limit_bytes=VMEM_LIMIT)


def _rms(xf, g):
    return xf * lax.rsqrt(jnp.mean(xf * xf, axis=-1, keepdims=True) + NORM_EPS) * g


def _dot(a, b):
    return jnp.dot(a.astype(BF16), b.astype(BF16), preferred_element_type=F32)


def _dot_nt(a, b):
    return lax.dot_general(a.astype(BF16), b.astype(BF16), (((1,), (1,)), ((), ())),
                           preferred_element_type=F32)


def _dot_tn(a, b):
    return lax.dot_general(a.astype(BF16), b.astype(BF16), (((0,), (0,)), ((), ())),
                           preferred_element_type=F32)


def _sigmoid(x):
    return 1.0 / (1.0 + jnp.exp(-x))


def _silu(x):
    return x * _sigmoid(x)


def _softplus(x):
    return jnp.maximum(x, 0.0) + jnp.log(1.0 + jnp.exp(-jnp.abs(x)))


def _gelu_tanh(x):
    c = 0.7978845608028654
    return 0.5 * x * (1.0 + jnp.tanh(c * (x + 0.044715 * (x * x * x))))


def _in_proj_kernel(x_ref, g_ref, w_ref, ws_ref, o_ref, os_ref, hn_ref):
    @pl.when(pl.program_id(1) == 0)
    def _():
        hn = _rms(x_ref[...], g_ref[...]).astype(BF16)
        hn_ref[...] = hn
        os_ref[...] = jnp.dot(hn, ws_ref[...], preferred_element_type=F32)

    o_ref[...] = jnp.dot(hn_ref[...], w_ref[...], preferred_element_type=F32).astype(o_ref.dtype)


def _in_proj(x2, g, w_big, w_small, tm=1024, tn=512):
    t, d = x2.shape
    n = w_big.shape[1]
    ns = w_small.shape[1]
    return pl.pallas_call(
        _in_proj_kernel,
        out_shape=(jax.ShapeDtypeStruct((t, n), BF16), jax.ShapeDtypeStruct((t, ns), F32)),
        grid=(t // tm, n // tn),
        in_specs=[
            pl.BlockSpec((tm, d), lambda i, j: (i, 0)),
            pl.BlockSpec((1, d), lambda i, j: (0, 0)),
            pl.BlockSpec((d, tn), lambda i, j: (0, j)),
            pl.BlockSpec((d, ns), lambda i, j: (0, 0)),
        ],
        out_specs=(
            pl.BlockSpec((tm, tn), lambda i, j: (i, j)),
            pl.BlockSpec((tm, ns), lambda i, j: (i, 0)),
        ),
        scratch_shapes=[pltpu.VMEM((tm, d), BF16)],
        compiler_params=_cparams(("parallel", "arbitrary")),
        name="in_proj",
    )(x2, g, w_big, w_small)


def _norm_proj_kernel(x_ref, g_ref, w_ref, o_ref):
    hn = _rms(x_ref[...], g_ref[...]).astype(BF16)
    o_ref[...] = jnp.dot(hn, w_ref[...], preferred_element_type=F32).astype(o_ref.dtype)


def _norm_proj(x2, g, w, tm=512):
    t, d = x2.shape
    n = w.shape[1]
    return pl.pallas_call(
        _norm_proj_kernel,
        out_shape=jax.ShapeDtypeStruct((t, n), BF16),
        grid=(t // tm,),
        in_specs=[
            pl.BlockSpec((tm, d), lambda i: (i, 0)),
            pl.BlockSpec((1, d), lambda i: (0, 0)),
            pl.BlockSpec((d, n), lambda i: (0, 0)),
        ],
        out_specs=pl.BlockSpec((tm, n), lambda i: (i, 0)),
        compiler_params=_cparams(("parallel",)),
        name="mem_kv_proj",
    )(x2, g, w)


def _dot_hilo_rhs(a01, b):
    hi = b.astype(BF16)
    lo = (b - hi.astype(F32)).astype(BF16)
    return (jnp.dot(a01, hi, preferred_element_type=F32) + jnp.dot(a01, lo, preferred_element_type=F32))


def _dot3(a, b):
    ah = a.astype(BF16)
    al = (a - ah.astype(F32)).astype(BF16)
    bh = b.astype(BF16)
    bl = (b - bh.astype(F32)).astype(BF16)
    return (jnp.dot(ah, bh, preferred_element_type=F32)
            + (jnp.dot(ah, bl, preferred_element_type=F32) + jnp.dot(al, bh, preferred_element_type=F32)))


def _dot_hilo_lhs(a, b01):
    hi = a.astype(BF16)
    lo = (a - hi.astype(F32)).astype(BF16)
    return (jnp.dot(hi, b01, preferred_element_type=F32) + jnp.dot(lo, b01, preferred_element_type=F32))


def _deltanet_kernel(q_ref, k_ref, v_ref, zg_ref, grow_ref, gcol_ref, alr_ref, dtr_ref, alc_ref,
                     dtc_ref, cwq_ref, cwk_ref, cwv_ref, nw_ref, o_ref, state_ref, tail_ref, ext_ref):
    tt = q_ref.shape[0]
    c = CHUNK
    hg = DN_HG
    dk, dv = DN_DK, DN_DV
    halo = SUBLANE
    nchunk = tt // c

    @pl.when(pl.program_id(2) == 0)
    def _():
        state_ref[...] = jnp.zeros_like(state_ref)
        tail_ref[...] = jnp.zeros_like(tail_ref)

    def conv_silu(idx, raw_ref, cw_ref):
        raw = raw_ref[...].astype(F32)
        ext_ref[idx, 0:halo, :] = tail_ref[idx]
        ext_ref[idx, halo:halo + tt, :] = raw
        tail_ref[idx] = raw[tt - halo:tt, :]
        cw = cw_ref[...]
        base = halo - (DN_CONV - 1)
        y = ext_ref[idx, base:base + tt, :] * cw[0:1, :]
        for j in range(1, DN_CONV):
            y = y + ext_ref[idx, base + j:base + j + tt, :] * cw[j:j + 1, :]
        return _silu(y)

    q_all = conv_silu(0, q_ref, cwq_ref)
    k_all = conv_silu(1, k_ref, cwk_ref)
    v_all = conv_silu(2, v_ref, cwv_ref)

    g_rows = -jnp.exp(alr_ref[...]) * _softplus(grow_ref[...] + dtr_ref[...])
    gcol = gcol_ref[...]
    beta_cols = _sigmoid(gcol)
    g_cols = -jnp.exp(alc_ref[...]) * _softplus(gcol + dtc_ref[...])

    ri = lax.broadcasted_iota(jnp.int32, (c, c), 0)
    ci = lax.broadcasted_iota(jnp.int32, (c, c), 1)
    causal = ri >= ci
    strict = ri > ci
    ident = jnp.where(ri == ci, 1.0, 0.0).astype(F32)
    tril01 = jnp.where(causal, 1.0, 0.0).astype(BF16)
    triu01 = jnp.where(ri <= ci, 1.0, 0.0).astype(BF16)

    items = [(hh, n) for n in range(nchunk) for hh in range(hg)]
    qs, lower, a_qk, e_col, k_dec, g_last, vb = {}, {}, {}, {}, {}, {}, {}
    for n in range(nchunk):
        sl = slice(n * c, (n + 1) * c)
        gcs_c = _dot_hilo_rhs(tril01, g_cols[sl])
        gcs_r = _dot_hilo_lhs(g_rows[:, sl], triu01)
        for hh in range(hg):
            it = (hh, n)
            hs = slice(hh * dk, (hh + 1) * dk)
            qc, kc = q_all[sl, hs], k_all[sl, hs]
            qc = qc * (lax.rsqrt(jnp.sum(qc * qc, axis=-1, keepdims=True) + NORM_EPS) * (dk ** -0.5))
            kc = kc * lax.rsqrt(jnp.sum(kc * kc, axis=-1, keepdims=True) + NORM_EPS)
            beta_col = beta_cols[sl, hh:hh + 1]
            gc_col = gcs_c[:, hg + hh:hg + hh + 1]
            gc_row = gcs_r[hh:hh + 1, :]
            gc_end = gcs_r[hh:hh + 1, c - 1:c]
            decay = jnp.where(causal, jnp.exp(jnp.where(causal, gc_col - gc_row, 0.0)), 0.0)
            kb = kc * beta_col
            res = _dot_nt(jnp.concatenate([kb, qc], axis=0), kc)
            lower[it] = jnp.where(strict, res[0:c] * decay, 0.0)
            a_qk[it] = jnp.where(causal, res[c:2 * c] * decay, 0.0)
            e_col[it] = jnp.exp(gc_col)
            k_dec[it] = kc * jnp.exp(gc_end - gc_col)
            g_last[it] = jnp.exp(gc_end)
            vb[it] = jnp.concatenate([v_all[sl, hh * dv:(hh + 1) * dv] * beta_col, kb * e_col[it]], axis=1)
            qs[it] = qc

    base_shift = DN_BASE.bit_length() - 1
    same_base = (ri >> base_shift) == (ci >> base_shift)
    inv, pw = {}, {}
    for it in items:
        lb = jnp.where(same_base, lower[it], 0.0)
        inv[it] = ident - lb
        pw[it] = _dot3(lb, lb)
    span = 2
    while span < DN_BASE:
        for it in items:
            if span * 2 < DN_BASE:
                res = _dot3(jnp.concatenate([inv[it], pw[it]], axis=0), pw[it])
                inv[it] = inv[it] + res[0:c]
                pw[it] = res[c:2 * c]
            else:
                inv[it] = inv[it] + _dot3(inv[it], pw[it])
        span *= 2
    width = 2 * DN_BASE
    while width <= c:
        sh = width.bit_length() - 1
        same_w = (ri >> sh) == (ci >> sh)
        same_half = (ri >> (sh - 1)) == (ci >> (sh - 1))
        off = {it: jnp.where(same_w, jnp.where(same_half, 0.0, lower[it]), 0.0) for it in items}
        mid = {it: _dot3(off[it], inv[it]) for it in items}
        for it in items:
            inv[it] = inv[it] - _dot3(inv[it], mid[it])
        width *= 2

    uw = {it: _dot3(inv[it], vb[it]) for it in items}
    a_uw = {it: _dot(a_qk[it], uw[it]) for it in items}
    k_uw = {it: _dot_tn(k_dec[it], uw[it]) for it in items}

    nw = nw_ref[...]
    state = [state_ref[hh] for hh in range(hg)]
    for n in range(nchunk):
        sl = slice(n * c, (n + 1) * c)
        for hh in range(hg):
            it = (hh, n)
            q_eff = qs[it] * e_col[it] - a_uw[it][:, dv:dv + dk]
            res = _dot(jnp.concatenate([q_eff, k_uw[it][:, dv:dv + dk]], axis=0), state[hh])
            o = res[0:c] + a_uw[it][:, 0:dv]
            state[hh] = state[hh] * g_last[it] - res[c:c + dk] + k_uw[it][:, 0:dv]
            hs = slice(hh * dv, (hh + 1) * dv)
            o = _rms(o, nw) * _silu(zg_ref[sl, hs].astype(F32))
            o_ref[sl, hs] = o.astype(o_ref.dtype)
    for hh in range(hg):
        state_ref[hh] = state[hh]


def _deltanet(z3, zs3, dn_a_log, dn_dt_bias, conv_w_t, norm_w, tt=256):
    b, s, _ = z3.shape
    nh, hg = DN_HEADS, DN_HG
    ngrp = nh // hg
    wq = hg * DN_DK
    bq, bk, bv, bz = _OFF_DQ // wq, _OFF_DK // wq, _OFF_DV // wq, _OFF_DZ // wq
    db = zs3[..., _SM_DB:_SM_DB + nh].reshape(b, s, ngrp, hg)
    da = zs3[..., _SM_DA:_SM_DA + nh].reshape(b, s, ngrp, hg)
    grow = jnp.pad(da.transpose(0, 2, 3, 1), ((0, 0), (0, 0), (0, SUBLANE - hg), (0, 0)))
    gcol = jnp.pad(jnp.concatenate([db, da], axis=-1).transpose(0, 2, 1, 3),
                   ((0, 0), (0, 0), (0, 0), (0, LANE - 2 * hg)))
    al = dn_a_log.astype(F32).reshape(ngrp, hg)
    dtb = dn_dt_bias.astype(F32).reshape(ngrp, hg)
    alr = jnp.pad(al, ((0, 0), (0, SUBLANE - hg)))[:, :, None]
    dtr = jnp.pad(dtb, ((0, 0), (0, SUBLANE - hg)))[:, :, None]
    alc = jnp.pad(al, ((0, 0), (hg, LANE - 2 * hg)))[:, None, :]
    dtc = jnp.pad(dtb, ((0, 0), (hg, LANE - 2 * hg)))[:, None, :]
    return pl.pallas_call(
        _deltanet_kernel,
        out_shape=jax.ShapeDtypeStruct((b, s, nh * DN_DV), BF16),
        grid=(b, ngrp, s // tt),
        in_specs=[
            pl.BlockSpec((None, tt, wq), lambda bi, g, i: (bi, i, bq + g)),
            pl.BlockSpec((None, tt, wq), lambda bi, g, i: (bi, i, bk + g)),
            pl.BlockSpec((None, tt, wq), lambda bi, g, i: (bi, i, bv + g)),
            pl.BlockSpec((None, tt, wq), lambda bi, g, i: (bi, i, bz + g)),
            pl.BlockSpec((None, None, SUBLANE, tt), lambda bi, g, i: (bi, g, 0, i)),
            pl.BlockSpec((None, None, tt, LANE), lambda bi, g, i: (bi, g, i, 0)),
            pl.BlockSpec((None, SUBLANE, 1), lambda bi, g, i: (g, 0, 0)),
            pl.BlockSpec((None, SUBLANE, 1), lambda bi, g, i: (g, 0, 0)),
            pl.BlockSpec((None, 1, LANE), lambda bi, g, i: (g, 0, 0)),
            pl.BlockSpec((None, 1, LANE), lambda bi, g, i: (g, 0, 0)),
            pl.BlockSpec((DN_CONV, wq), lambda bi, g, i: (0, g)),
            pl.BlockSpec((DN_CONV, wq), lambda bi, g, i: (0, ngrp + g)),
            pl.BlockSpec((DN_CONV, wq), lambda bi, g, i: (0, 2 * ngrp + g)),
            pl.BlockSpec((1, DN_DV), lambda bi, g, i: (0, 0)),
        ],
        out_specs=pl.BlockSpec((None, tt, wq), lambda bi, g, i: (bi, i, g)),
        scratch_shapes=[
            pltpu.VMEM((hg, DN_DK, DN_DV), F32),
            pltpu.VMEM((3, SUBLANE, wq), F32),
            pltpu.VMEM((3, tt + SUBLANE, wq), F32),
        ],
        compiler_params=_cparams(("parallel", "parallel", "arbitrary")),
        name="deltanet",
    )(z3, z3, z3, z3, grow, gcol, alr, dtr, alc, dtc, conv_w_t, conv_w_t, conv_w_t, norm_w)


def _gla_kernel(q_ref, k_ref, v_ref, r_ref, zs_ref, wup_ref, bup_ref, nw_ref, o_ref, state_ref):
    tt = q_ref.shape[0]
    c = CHUNK
    sub = GLA_SUB

    @pl.when(pl.program_id(2) == 0)
    def _():
        state_ref[...] = jnp.zeros_like(state_ref)

    logit = _dot(zs_ref[...], wup_ref[...]) + bup_ref[...]
    log_a = -_softplus(-logit) * (1.0 / GLA_TAU)

    ri = lax.broadcasted_iota(jnp.int32, (c, c), 0)
    ci = lax.broadcasted_iota(jnp.int32, (c, c), 1)
    tril = jnp.where(ri >= ci, 1.0, 0.0).astype(BF16)
    sub_rows = lax.broadcasted_iota(jnp.int32, (sub, 1), 0)
    key_rows = lax.broadcasted_iota(jnp.int32, (c, 1), 0)
    blk_cols = lax.broadcasted_iota(jnp.int32, (sub, c), 1)

    nw = nw_ref[...]
    scale = GLA_DK ** -0.5
    for n in range(tt // c):
        sl = slice(n * c, (n + 1) * c)
        bcum = _dot_hilo_rhs(tril, log_a[sl])
        qc = q_ref[sl, :].astype(F32) * scale
        kc = k_ref[sl, :].astype(F32)
        vc = v_ref[sl, :]
        b_last = bcum[c - 1:c, :]
        q_dec = qc * jnp.exp(bcum)
        k_dec = kc * jnp.exp(b_last - bcum)
        a_last = jnp.exp(b_last)

        row_blocks = []
        for bi in range(c // sub):
            r0 = bi * sub
            q_i = qc[r0:r0 + sub]
            b_i = bcum[r0:r0 + sub]
            if bi > 0:
                b_ref = bcum[r0 - 1:r0, :]
                early = key_rows < r0
                q_f = q_i * jnp.exp(b_i - b_ref)
                k_f = jnp.where(early, kc * jnp.exp(jnp.where(early, b_ref - bcum, 0.0)), 0.0)
                blk = _dot_nt(q_f, k_f)
            else:
                blk = jnp.zeros((sub, c), F32)
            for j in range(sub):
                kj = kc[r0 + j:r0 + j + 1, :]
                bj = bcum[r0 + j:r0 + j + 1, :]
                live = sub_rows >= j
                dec = jnp.exp(jnp.where(live, b_i - bj, 0.0))
                col = jnp.sum(jnp.where(live, q_i * kj * dec, 0.0), axis=1, keepdims=True)
                blk = jnp.where(blk_cols == r0 + j, col, blk)
            row_blocks.append(blk)
        a_qk = jnp.concatenate(row_blocks, axis=0)

        state_t = state_ref[...]
        o = _dot_nt(q_dec, state_t) + _dot(a_qk, vc)
        state_ref[...] = state_t * a_last + _dot_tn(vc, k_dec)

        o = _rms(o, nw) * _silu(r_ref[sl, :].astype(F32))
        o_ref[sl, :] = o.astype(o_ref.dtype)


def _gla(z3, zs3, w_up_pad, b_up, norm_w, tt=256):
    b, s, _ = z3.shape
    hq, hk = _OFF_GQ // GLA_DK, _OFF_GK // GLA_DK
    hv, hr = _OFF_GV // GLA_DV, _OFF_GR // GLA_DV
    ns = zs3.shape[-1]
    return pl.pallas_call(
        _gla_kernel,
        out_shape=jax.ShapeDtypeStruct((b, s, GLA_HEADS * GLA_DV), BF16),
        grid=(b, GLA_HEADS, s // tt),
        in_specs=[
            pl.BlockSpec((None, tt, GLA_DK), lambda bi, h, i: (bi, i, hq + h)),
            pl.BlockSpec((None, tt, GLA_DK), lambda bi, h, i: (bi, i, hk + h)),
            pl.BlockSpec((None, tt, GLA_DV), lambda bi, h, i: (bi, i, hv + h)),
            pl.BlockSpec((None, tt, GLA_DV), lambda bi, h, i: (bi, i, hr + h)),
            pl.BlockSpec((None, tt, ns), lambda bi, h, i: (bi, i, 0)),
            pl.BlockSpec((ns, GLA_DK), lambda bi, h, i: (0, h)),
            pl.BlockSpec((1, GLA_DK), lambda bi, h, i: (0, h)),
            pl.BlockSpec((1, GLA_DV), lambda bi, h, i: (0, 0)),
        ],
        out_specs=pl.BlockSpec((None, tt, GLA_DV), lambda bi, h, i: (bi, i, h)),
        scratch_shapes=[pltpu.VMEM((GLA_DV, GLA_DK), F32)],
        compiler_params=_cparams(("parallel", "parallel", "arbitrary")),
        name="gla",
    )(z3, z3, z3, z3, zs3, w_up_pad, b_up, norm_w)


def _s5_kernel(*refs):
    ll = S5_L
    u_refs = refs[:ll]
    (kblk_ref, vre_ref, vim_ref, wre_ref, wim_ref, al_ref, d_ref,
     o_ref, m_ref, v_ref, w_ref, e_ref, xp_ref) = refs[ll:]
    rows = e_ref.shape[0]
    nst = S5_LB * S5_STATE

    @pl.when(pl.program_id(1) == 0)
    def _():
        zero_blk = jnp.zeros((LANE, LANE), BF16)
        for s in range(ll):
            for i in range(ll):
                m_ref[s * LANE:(s + 1) * LANE, i * LANE:(i + 1) * LANE] = (
                    kblk_ref[i - s] if i >= s else zero_blk)
        rblk = 2 * LANE
        vr = lax.broadcasted_iota(jnp.int32, (rblk, nst), 0)
        vc = lax.broadcasted_iota(jnp.int32, (rblk, nst), 1)
        keep_v = ((vr & (LANE - 1)) >> 4) == (vc >> 6)
        for r in range(ll * LANE // rblk):
            rs = slice(r * rblk, (r + 1) * rblk)
            v_ref[rs, 0:nst] = jnp.where(keep_v, jnp.tile(vre_ref[rs, :], (1, nst // LANE)), 0.0).astype(BF16)
            v_ref[rs, nst:2 * nst] = jnp.where(keep_v, jnp.tile(vim_ref[rs, :], (1, nst // LANE)), 0.0).astype(BF16)
        wr = lax.broadcasted_iota(jnp.int32, (nst, rblk), 0)
        wc = lax.broadcasted_iota(jnp.int32, (nst, rblk), 1)
        keep_w = (wr >> 6) == ((wc & (LANE - 1)) >> 4)
        for r in range(ll * LANE // rblk):
            cs = slice(r * rblk, (r + 1) * rblk)
            w_ref[0:nst, cs] = jnp.where(keep_w, jnp.tile(wre_ref[:, cs], (S5_LB, 1)), 0.0).astype(BF16)
            w_ref[nst:2 * nst, cs] = jnp.where(keep_w, jnp.tile(wim_ref[:, cs], (S5_LB, 1)), 0.0).astype(BF16)

    u_list = [r[...] for r in u_refs]
    u_cat = jnp.concatenate(u_list, axis=1)
    e_ref[...] = jnp.dot(u_cat, v_ref[...], preferred_element_type=F32)
    ar = al_ref[:, 0:nst]
    ai = al_ref[:, nst:2 * nst]

    def step(cidx, carry):
        xr, xi = carry
        row = pl.ds(cidx, 1)
        xp_ref[row, 0:nst] = xr
        xp_ref[row, nst:2 * nst] = xi
        er = e_ref[row, 0:nst]
        ei = e_ref[row, nst:2 * nst]
        return ar * xr - ai * xi + er, ar * xi + ai * xr + ei

    zero = jnp.zeros((1, nst), F32)
    lax.fori_loop(0, rows, step, (zero, zero), unroll=8)

    y = (jnp.dot(u_cat, m_ref[...], preferred_element_type=F32)
         + jnp.dot(xp_ref[...].astype(BF16), w_ref[...], preferred_element_type=F32))
    d = d_ref[...]
    for i in range(ll):
        yi = y[:, i * LANE:(i + 1) * LANE] + d * u_list[i].astype(F32)
        o_ref[i] = _gelu_tanh(yi).astype(o_ref.dtype)


def _s5(z2, mats, nbatch):
    kblk, vre, vim, wre, wim, al, dskip = mats
    t, ncol = z2.shape
    ll = S5_L
    nrow = t // ll
    rows = nrow // nbatch
    zr = z2.reshape(nrow, ll * ncol)
    nblk = S5_WIDTH // LANE
    cb, sb = ncol // LANE, _OFF_SU // LANE
    nst2 = 2 * S5_LB * S5_STATE

    def u_spec(s):
        return pl.BlockSpec((rows, LANE), lambda j, b: (b, s * cb + sb + j))

    out = pl.pallas_call(
        _s5_kernel,
        out_shape=jax.ShapeDtypeStruct((ll, nrow, S5_WIDTH), BF16),
        grid=(nblk, nbatch),
        in_specs=[u_spec(s) for s in range(ll)] + [
            pl.BlockSpec((None, ll, LANE, LANE), lambda j, b: (j, 0, 0, 0)),
            pl.BlockSpec((None, ll * LANE, LANE), lambda j, b: (j, 0, 0)),
            pl.BlockSpec((None, ll * LANE, LANE), lambda j, b: (j, 0, 0)),
            pl.BlockSpec((None, S5_STATE, ll * LANE), lambda j, b: (j, 0, 0)),
            pl.BlockSpec((None, S5_STATE, ll * LANE), lambda j, b: (j, 0, 0)),
            pl.BlockSpec((None, 1, nst2), lambda j, b: (j, 0, 0)),
            pl.BlockSpec((None, 1, LANE), lambda j, b: (j, 0, 0)),
        ],
        out_specs=pl.BlockSpec((ll, rows, LANE), lambda j, b: (0, b, j)),
        scratch_shapes=[
            pltpu.VMEM((ll * LANE, ll * LANE), BF16),
            pltpu.VMEM((ll * LANE, nst2), BF16),
            pltpu.VMEM((nst2, ll * LANE), BF16),
            pltpu.VMEM((rows, nst2), F32),
            pltpu.VMEM((rows, nst2), F32),
        ],
        compiler_params=_cparams(("parallel", "arbitrary")),
        name="s5",
    )(*([zr] * ll), kblk, vre, vim, wre, wim, al, dskip)
    return out.transpose(1, 0, 2).reshape(t, S5_WIDTH)


def _s5_matrices(lam_re, lam_im, log_step, b_re, b_im, c_re, c_im, d_skip):
    g, p, hh, ll, lb = S5_GROUPS, S5_STATE, S5_GROUP, S5_L, S5_LB
    nb = g // lb
    lr = jnp.minimum(lam_re.astype(F32), -1e-4)
    li = lam_im.astype(F32)
    dt = jnp.exp(log_step.astype(F32))[:, None]
    mag = jnp.exp(lr * dt)
    ar = mag * jnp.cos(li * dt)
    ai = mag * jnp.sin(li * dt)
    nr = ar - 1.0
    den = lr * lr + li * li
    zr = (nr * lr + ai * li) / den
    zi = (ai * lr - nr * li) / den
    br = b_re.astype(F32)
    bi = b_im.astype(F32)
    bbr = zr[..., None] * br - zi[..., None] * bi
    bbi = zr[..., None] * bi + zi[..., None] * br
    pr = [jnp.ones_like(ar)]
    pi = [jnp.zeros_like(ai)]
    for _ in range(ll):
        pr.append(pr[-1] * ar - pi[-1] * ai)
        pi.append(pr[-2] * ai + pi[-1] * ar)
    pr_rev = jnp.stack(pr[ll - 1::-1], axis=0)
    pi_rev = jnp.stack(pi[ll - 1::-1], axis=0)
    pr = jnp.stack(pr, axis=0)
    pi = jnp.stack(pi, axis=0)
    cr = c_re.astype(F32)
    cim = c_im.astype(F32)

    def blk(a, *shape):
        return a.reshape((nb, lb) + shape)

    pr_d = jnp.moveaxis(pr.reshape(ll + 1, nb, lb, p), 0, 1)[:, :, :, None, None, :]
    pi_d = jnp.moveaxis(pi.reshape(ll + 1, nb, lb, p), 0, 1)[:, :, :, None, None, :]
    cr_b = blk(cr, hh, p)[:, None, :, None, :, :]
    ci_b = blk(cim, hh, p)[:, None, :, None, :, :]
    qr = cr_b * pr_d - ci_b * pi_d
    qi = cr_b * pi_d + ci_b * pr_d
    bbr_b = blk(jnp.swapaxes(bbr, 1, 2), hh, p)[:, None, :, :, None, :]
    bbi_b = blk(jnp.swapaxes(bbi, 1, 2), hh, p)[:, None, :, :, None, :]
    kd = jnp.sum(qr[:, :ll] * bbr_b - qi[:, :ll] * bbi_b, axis=-1)
    eye = jnp.eye(lb, dtype=F32)[None, None, :, None, :, None]
    kblk = (kd[:, :, :, :, None, :] * eye).reshape(nb, ll, lb * hh, lb * hh)
    prs = jnp.moveaxis(pr_rev.reshape(ll, nb, lb, p), 0, 1)[:, :, :, None, :]
    pis = jnp.moveaxis(pi_rev.reshape(ll, nb, lb, p), 0, 1)[:, :, :, None, :]
    bbr_v, bbi_v = bbr_b[:, :, :, :, 0, :], bbi_b[:, :, :, :, 0, :]
    vre = jnp.tile((prs * bbr_v - pis * bbi_v).reshape(nb, ll * lb * hh, p), (1, 1, 2))
    vim = jnp.tile((prs * bbi_v + pis * bbr_v).reshape(nb, ll * lb * hh, p), (1, 1, 2))
    wre = jnp.moveaxis(qr[:, 1:ll + 1, :, 0], -1, 1).reshape(nb, p, ll * lb * hh)
    wim = -jnp.moveaxis(qi[:, 1:ll + 1, :, 0], -1, 1).reshape(nb, p, ll * lb * hh)

    al = jnp.concatenate([pr[ll].reshape(nb, 1, lb * p), pi[ll].reshape(nb, 1, lb * p)], axis=2)
    dskip = d_skip.astype(F32).reshape(nb, 1, lb * hh)
    return kblk.astype(BF16), vre, vim, wre, wim, al, dskip


def _mix_out_kernel(oa_ref, ob_ref, ys_ref, ga_ref, gb_ref, gc_ref, x_ref,
                    wa_ref, wb_ref, wca_ref, wcb_ref, wm_ref, gpost_ref, o_ref, acc_ref):
    n = pl.program_id(1)
    y_a = jnp.dot(oa_ref[...], wa_ref[...], preferred_element_type=F32)
    y_b = jnp.dot(ob_ref[...], wb_ref[...], preferred_element_type=F32)
    ys = ys_ref[...]
    glu_a = jnp.dot(ys, wca_ref[...], preferred_element_type=F32)
    glu_b = jnp.dot(ys, wcb_ref[...], preferred_element_type=F32)
    y_c = glu_a * _sigmoid(glu_b)
    merged = (_sigmoid(ga_ref[...].astype(F32)) * y_a + _sigmoid(gb_ref[...].astype(F32)) * y_b
              + _sigmoid(gc_ref[...].astype(F32)) * y_c)
    part = jnp.dot(merged.astype(BF16), wm_ref[...], preferred_element_type=F32)

    @pl.when(n == 0)
    def _():
        acc_ref[...] = part

    @pl.when(n > 0)
    def _():
        acc_ref[...] += part

    @pl.when(n == pl.num_programs(1) - 1)
    def _():
        o_ref[...] = x_ref[...] + _rms(acc_ref[...], gpost_ref[...])


def _mix_out(oa, ob, ys, z2, x2, w_dn_out, w_gla_out, w_s5_glu, w_mix_out, g_post, tm=512, tn=512):
    t, d = x2.shape
    nsteps = d // tn
    gblk = _OFF_GATES // tn
    ka, kb, kc = oa.shape[1], ob.shape[1], ys.shape[1]
    return pl.pallas_call(
        _mix_out_kernel,
        out_shape=jax.ShapeDtypeStruct((t, d), F32),
        grid=(t // tm, nsteps),
        in_specs=[
            pl.BlockSpec((tm, ka), lambda i, n: (i, 0)),
            pl.BlockSpec((tm, kb), lambda i, n: (i, 0)),
            pl.BlockSpec((tm, kc), lambda i, n: (i, 0)),
            pl.BlockSpec((tm, tn), lambda i, n: (i, gblk + n)),
            pl.BlockSpec((tm, tn), lambda i, n: (i, gblk + nsteps + n)),
            pl.BlockSpec((tm, tn), lambda i, n: (i, gblk + 2 * nsteps + n)),
            pl.BlockSpec((tm, d), lambda i, n: (i, 0)),
            pl.BlockSpec((ka, tn), lambda i, n: (0, n)),
            pl.BlockSpec((kb, tn), lambda i, n: (0, n)),
            pl.BlockSpec((kc, tn), lambda i, n: (0, n)),
            pl.BlockSpec((kc, tn), lambda i, n: (0, nsteps + n)),
            pl.BlockSpec((tn, d), lambda i, n: (n, 0)),
            pl.BlockSpec((1, d), lambda i, n: (0, 0)),
        ],
        out_specs=pl.BlockSpec((tm, d), lambda i, n: (i, 0)),
        scratch_shapes=[pltpu.VMEM((tm, d), F32)],
        compiler_params=_cparams(("parallel", "arbitrary")),
        name="mix_out",
    )(oa, ob, ys, z2, z2, z2, x2, w_dn_out, w_gla_out, w_s5_glu, w_s5_glu, w_mix_out, g_post)


def _xattn_kernel(x_ref, gpre_ref, wq_ref, kv_ref, wo_ref, gpost_ref, o_ref):
    x = x_ref[...]
    hn = _rms(x, gpre_ref[...]).astype(BF16)
    q = jnp.dot(hn, wq_ref[...], preferred_element_type=F32)
    kv = kv_ref[...]
    outs = []
    for h in range(XA_HEADS):
        qh = q[:, h * XA_DH:(h + 1) * XA_DH]
        kh = kv[:, h * XA_DH:(h + 1) * XA_DH]
        vh = kv[:, (XA_HEADS + h) * XA_DH:(XA_HEADS + h + 1) * XA_DH]
        s = _dot_nt(qh, kh) * (XA_DH ** -0.5)
        s = s - jnp.max(s, axis=-1, keepdims=True)
        p = jnp.exp(s)
        p = p / jnp.sum(p, axis=-1, keepdims=True)
        outs.append(_dot(p, vh))
    o = jnp.concatenate(outs, axis=1)
    y = jnp.dot(o.astype(BF16), wo_ref[...], preferred_element_type=F32)
    o_ref[...] = x + _rms(y, gpost_ref[...])


def _xattn(x3, g_pre, w_q, kv3, w_out, g_post, tm=512):
    b, s, d = x3.shape
    m, nkv = kv3.shape[1], kv3.shape[2]
    nq = w_q.shape[1]
    return pl.pallas_call(
        _xattn_kernel,
        out_shape=jax.ShapeDtypeStruct((b, s, d), F32),
        grid=(b, s // tm),
        in_specs=[
            pl.BlockSpec((None, tm, d), lambda bi, i: (bi, i, 0)),
            pl.BlockSpec((1, d), lambda bi, i: (0, 0)),
            pl.BlockSpec((d, nq), lambda bi, i: (0, 0)),
            pl.BlockSpec((None, m, nkv), lambda bi, i: (bi, 0, 0)),
            pl.BlockSpec((nq, d), lambda bi, i: (0, 0)),
            pl.BlockSpec((1, d), lambda bi, i: (0, 0)),
        ],
        out_specs=pl.BlockSpec((None, tm, d), lambda bi, i: (bi, i, 0)),
        compiler_params=_cparams(("parallel", "parallel")),
        name="xattn",
    )(x3, g_pre, w_q, kv3, w_out, g_post)


def _convffn_kernel(x_ref, xh_ref, gpre_ref, wg_ref, wu_ref, cw_ref, wd_ref, gpost_ref,
                    o_ref, hn_ref, hh_ref, gate_ref, acc_ref):
    i = pl.program_id(1)
    f = pl.program_id(2)
    tm = x_ref.shape[0]
    halo = SUBLANE

    @pl.when(f == 0)
    def _():
        hn_ref[...] = _rms(x_ref[...], gpre_ref[...]).astype(BF16)
        keep = jnp.where(i > 0, 1.0, 0.0)
        hh_ref[...] = (_rms(xh_ref[...], gpre_ref[...]) * keep).astype(BF16)

    hn = hn_ref[...]
    wg = wg_ref[...]
    gate_ref[0:halo, :] = jnp.dot(hh_ref[...], wg, preferred_element_type=F32)
    gate_ref[halo:halo + tm, :] = jnp.dot(hn, wg, preferred_element_type=F32)
    up = jnp.dot(hn, wu_ref[...], preferred_element_type=F32)
    cw = cw_ref[...]
    base = halo - (FFN_CONV - 1)
    conv = gate_ref[base:base + tm, :] * cw[0:1, :]
    for j in range(1, FFN_CONV):
        conv = conv + gate_ref[base + j:base + j + tm, :] * cw[j:j + 1, :]
    act = (_gelu_tanh(conv) * up).astype(BF16)
    part = jnp.dot(act, wd_ref[...], preferred_element_type=F32)

    @pl.when(f == 0)
    def _():
        acc_ref[...] = part

    @pl.when(f > 0)
    def _():
        acc_ref[...] += part

    @pl.when(f == pl.num_programs(2) - 1)
    def _():
        o_ref[...] = x_ref[...] + _rms(acc_ref[...], gpost_ref[...])


def _convffn(x3, g_pre, w_up, conv_w_t, w_down, g_post, tm=512, tf=512):
    b, s, d = x3.shape
    fh = w_down.shape[0]
    nf = fh // tf
    hb = tm // SUBLANE
    return pl.pallas_call(
        _convffn_kernel,
        out_shape=jax.ShapeDtypeStruct((b, s, d), F32),
        grid=(b, s // tm, nf),
        in_specs=[
            pl.BlockSpec((None, tm, d), lambda bi, i, f: (bi, i, 0)),
            pl.BlockSpec((None, SUBLANE, d), lambda bi, i, f: (bi, jnp.maximum(i * hb - 1, 0), 0)),
            pl.BlockSpec((1, d), lambda bi, i, f: (0, 0)),
            pl.BlockSpec((d, tf), lambda bi, i, f: (0, f)),
            pl.BlockSpec((d, tf), lambda bi, i, f: (0, nf + f)),
            pl.BlockSpec((FFN_CONV, tf), lambda bi, i, f: (0, f)),
            pl.BlockSpec((tf, d), lambda bi, i, f: (f, 0)),
            pl.BlockSpec((1, d), lambda bi, i, f: (0, 0)),
        ],
        out_specs=pl.BlockSpec((None, tm, d), lambda bi, i, f: (bi, i, 0)),
        scratch_shapes=[
            pltpu.VMEM((tm, d), BF16),
            pltpu.VMEM((SUBLANE, d), BF16),
            pltpu.VMEM((tm + SUBLANE, tf), F32),
            pltpu.VMEM((tm, d), F32),
        ],
        compiler_params=_cparams(("parallel", "parallel", "arbitrary")),
        name="convffn",
    )(x3, x3, g_pre, w_up, w_up, conv_w_t, w_down, g_post)


def _split_w_in(w_in_l):
    sizes = (DN_HEADS * DN_DK, DN_HEADS * DN_DK, DN_HEADS * DN_DV, DN_HEADS * DN_DV, DN_HEADS,
             DN_HEADS, GLA_HEADS * GLA_DK, GLA_HEADS * GLA_DK, GLA_HEADS * GLA_DV, GLA_RANK,
             GLA_HEADS * GLA_DV, S5_WIDTH, 3 * D_MODEL)
    parts = []
    acc = 0
    for n in sizes:
        parts.append(w_in_l[:, acc:acc + n])
        acc += n
    (dq, dk, dv, dz, db, da, gq, gk, gv, glow, gr, su, gates) = parts
    big = jnp.concatenate([dq, dk, dv, dz, gq, gk, gv, gr, su, gates], axis=1).astype(BF16)
    pad = jnp.zeros((w_in_l.shape[0], LANE - 2 * DN_HEADS - GLA_RANK), w_in_l.dtype)
    small = jnp.concatenate([db, da, glow, pad], axis=1).astype(BF16)
    return big, small


def kernel(x, mem, norm_mix_pre, norm_mix_post, w_in, dn_conv_w, dn_a_log, dn_dt_bias, dn_norm_w, w_dn_out, gla_w_up, gla_b_up, gla_norm_w, w_gla_out, s5_lam_re, s5_lam_im, s5_log_step, s5_b_re, s5_b_im, s5_c_re, s5_c_im, s5_d, w_s5_glu, w_mix_out, norm_xa_pre, norm_xa_post, norm_mem, w_xa_q, w_xa_kv, w_xa_out, norm_ffn_pre, norm_ffn_post, w_ffn_up, ffn_conv_w, w_ffn_down):
    b, s, d = x.shape
    t = b * s
    depth = w_in.shape[0]
    mem2 = mem.reshape(-1, d)
    row = lambda a: a.reshape(1, -1).astype(F32)

    for l in range(depth):
        w_big, w_small = _split_w_in(w_in[l])
        z2, zs2 = _in_proj(x.reshape(t, d), row(norm_mix_pre[l]), w_big, w_small)
        z3 = z2.reshape(b, s, -1)
        zs3 = zs2.reshape(b, s, -1)

        oa = _deltanet(z3, zs3, dn_a_log[l], dn_dt_bias[l], dn_conv_w[l].T.astype(F32),
                       row(dn_norm_w[l]))

        w_up_pad = jnp.zeros((zs2.shape[1], GLA_HEADS * GLA_DK), F32)
        w_up_pad = w_up_pad.at[_SM_GLOW:_SM_GLOW + GLA_RANK].set(gla_w_up[l]).astype(BF16)
        ob = _gla(z3, zs3, w_up_pad, row(gla_b_up[l]), row(gla_norm_w[l]))

        mats = _s5_matrices(s5_lam_re[l], s5_lam_im[l], s5_log_step[l], s5_b_re[l], s5_b_im[l],
                            s5_c_re[l], s5_c_im[l], s5_d[l])
        ys = _s5(z2, mats, nbatch=b)

        x = _mix_out(oa.reshape(t, -1), ob.reshape(t, -1), ys, z2, x.reshape(t, d),
                     w_dn_out[l].astype(BF16), w_gla_out[l].astype(BF16), w_s5_glu[l].astype(BF16),
                     w_mix_out[l].astype(BF16), row(norm_mix_post[l])).reshape(b, s, d)

        kv = _norm_proj(mem2, row(norm_mem[l]), w_xa_kv[l].astype(BF16))
        x = _xattn(x, row(norm_xa_pre[l]), w_xa_q[l].astype(BF16), kv.reshape(b, -1, kv.shape[-1]),
                   w_xa_out[l].astype(BF16), row(norm_xa_post[l]))

        x = _convffn(x, row(norm_ffn_pre[l]), w_ffn_up[l].astype(BF16), ffn_conv_w[l].T.astype(F32),
                     w_ffn_down[l].astype(BF16), row(norm_ffn_post[l]))
    return x
```

```python
import jax
import jax.numpy as jnp
from jax import lax
from jax.experimental import pallas as pl
from jax.experimental.pallas import tpu as pltpu

F32 = jnp.float32
BF16 = jnp.bfloat16

NORM_EPS = 1e-6

D_MODEL = 2048
DN_HEADS, DN_DK, DN_DV, DN_CONV = 8, 128, 128, 4
GLA_HEADS, GLA_DK, GLA_DV, GLA_RANK, GLA_TAU = 4, 128, 256, 16, 16.0
S5_GROUP, S5_GROUPS, S5_STATE = 16, 64, 64
S5_WIDTH = S5_GROUP * S5_GROUPS
XA_HEADS, XA_DH = 4, 128
FFN_HIDDEN, FFN_CONV = 5632, 3

CHUNK = 64
GLA_SUB = 16
S5_L = 16
DN_HG = 4
DN_BASE = 8
LANE = 128
S5_LB = LANE // S5_GROUP
SUBLANE = 8
VMEM_LIMIT = 56 * 1024 * 1024

_OFF_DQ, _OFF_DK, _OFF_DV, _OFF_DZ = 0, 1024, 2048, 3072
_OFF_GQ, _OFF_GK, _OFF_GV, _OFF_GR = 4096, 4608, 5120, 6144
_OFF_SU, _OFF_GATES = 7168, 8192
_BIG_WIDTH = 14336
_SM_DB, _SM_DA, _SM_GLOW = 0, 8, 16


def _cparams(sem):
    return pltpu.CompilerParams(dimension_semantics=sem, vmem_limit_bytes=VMEM_LIMIT)


def _rms(xf, g):
    return xf * lax.rsqrt(jnp.mean(xf * xf, axis=-1, keepdims=True) + NORM_EPS) * g


def _dot(a, b):
    return jnp.dot(a.astype(BF16), b.astype(BF16), preferred_element_type=F32)


def _dot_nt(a, b):
    return lax.dot_general(a.astype(BF16), b.astype(BF16), (((1,), (1,)), ((), ())),
                           preferred_element_type=F32)


def _dot_tn(a, b):
    return lax.dot_general(a.astype(BF16), b.astype(BF16), (((0,), (0,)), ((), ())),
                           preferred_element_type=F32)


def _sigmoid(x):
    return 0.5 * jnp.tanh(0.5 * x) + 0.5


def _silu(x):
    return x * _sigmoid(x)


def _softplus(x):
    return jnp.maximum(x, 0.0) + jnp.log(1.0 + jnp.exp(-jnp.abs(x)))


def _gelu_tanh(x):
    c = 0.7978845608028654
    return 0.5 * x * (1.0 + jnp.tanh(c * (x + 0.044715 * (x * x * x))))


def _in_proj_kernel(x_ref, g_ref, w_ref, ws_ref, o_ref, os_ref, su_ref, hn_ref):
    j = pl.program_id(1)
    tn = o_ref.shape[1]

    @pl.when(j == 0)
    def _():
        hn = _rms(x_ref[...], g_ref[...]).astype(BF16)
        hn_ref[...] = hn
        os_ref[...] = jnp.dot(hn, ws_ref[...], preferred_element_type=F32)

    acc = jnp.dot(hn_ref[...], w_ref[...], preferred_element_type=F32)
    o_ref[...] = acc.astype(o_ref.dtype)

    @pl.when(jnp.logical_and(j >= _OFF_SU // tn, j < (_OFF_SU + S5_WIDTH) // tn))
    def _():
        su_ref[...] = acc


def _in_proj(x2, g, w_big, w_small, tm=1024, tn=512):
    t, d = x2.shape
    n = w_big.shape[1]
    ns = w_small.shape[1]
    j0, nsu = _OFF_SU // tn, S5_WIDTH // tn
    return pl.pallas_call(
        _in_proj_kernel,
        out_shape=(jax.ShapeDtypeStruct((t, n), BF16), jax.ShapeDtypeStruct((t, ns), F32),
                   jax.ShapeDtypeStruct((t, S5_WIDTH), F32)),
        grid=(t // tm, n // tn),
        in_specs=[
            pl.BlockSpec((tm, d), lambda i, j: (i, 0)),
            pl.BlockSpec((1, d), lambda i, j: (0, 0)),
            pl.BlockSpec((d, tn), lambda i, j: (0, j)),
            pl.BlockSpec((d, ns), lambda i, j: (0, 0)),
        ],
        out_specs=(
            pl.BlockSpec((tm, tn), lambda i, j: (i, j)),
            pl.BlockSpec((tm, ns), lambda i, j: (i, 0)),
            pl.BlockSpec((tm, tn), lambda i, j: (i, jnp.clip(j - j0, 0, nsu - 1))),
        ),
        scratch_shapes=[pltpu.VMEM((tm, d), BF16)],
        compiler_params=_cparams(("parallel", "arbitrary")),
        name="in_proj",
    )(x2, g, w_big, w_small)


def _norm_proj_kernel(x_ref, g_ref, w_ref, o_ref):
    hn = _rms(x_ref[...], g_ref[...]).astype(BF16)
    o_ref[...] = jnp.dot(hn, w_ref[...], preferred_element_type=F32).astype(o_ref.dtype)


def _norm_proj(x2, g, w, tm=512):
    t, d = x2.shape
    n = w.shape[1]
    return pl.pallas_call(
        _norm_proj_kernel,
        out_shape=jax.ShapeDtypeStruct((t, n), BF16),
        grid=(t // tm,),
        in_specs=[
            pl.BlockSpec((tm, d), lambda i: (i, 0)),
            pl.BlockSpec((1, d), lambda i: (0, 0)),
            pl.BlockSpec((d, n), lambda i: (0, 0)),
        ],
        out_specs=pl.BlockSpec((tm, n), lambda i: (i, 0)),
        compiler_params=_cparams(("parallel",)),
        name="mem_kv_proj",
    )(x2, g, w)


def _dot_hilo_rhs(a01, b):
    hi = b.astype(BF16)
    lo = (b - hi.astype(F32)).astype(BF16)
    return (jnp.dot(a01, hi, preferred_element_type=F32) + jnp.dot(a01, lo, preferred_element_type=F32))


def _dot3(a, b):
    ah = a.astype(BF16)
    al = (a - ah.astype(F32)).astype(BF16)
    bh = b.astype(BF16)
    bl = (b - bh.astype(F32)).astype(BF16)
    return (jnp.dot(ah, bh, preferred_element_type=F32)
            + (jnp.dot(ah, bl, preferred_element_type=F32) + jnp.dot(al, bh, preferred_element_type=F32)))


def _dot_hilo_lhs(a, b01):
    hi = a.astype(BF16)
    lo = (a - hi.astype(F32)).astype(BF16)
    return (jnp.dot(hi, b01, preferred_element_type=F32) + jnp.dot(lo, b01, preferred_element_type=F32))


def _deltanet_kernel(q_ref, k_ref, v_ref, zg_ref, grow_ref, gcol_ref, alr_ref, dtr_ref, alc_ref,
                     dtc_ref, cwq_ref, cwk_ref, cwv_ref, nw_ref, o_ref, state_ref, tail_ref, ext_ref):
    tt = q_ref.shape[0]
    c = CHUNK
    hg = DN_HG
    dk, dv = DN_DK, DN_DV
    halo = SUBLANE
    nchunk = tt // c

    @pl.when(pl.program_id(2) == 0)
    def _():
        state_ref[...] = jnp.zeros_like(state_ref)
        tail_ref[...] = jnp.zeros_like(tail_ref)

    def conv_silu(idx, raw_ref, cw_ref):
        raw = raw_ref[...].astype(F32)
        ext_ref[idx, 0:halo, :] = tail_ref[idx]
        ext_ref[idx, halo:halo + tt, :] = raw
        tail_ref[idx] = raw[tt - halo:tt, :]
        cw = cw_ref[...]
        base = halo - (DN_CONV - 1)
        y = ext_ref[idx, base:base + tt, :] * cw[0:1, :]
        for j in range(1, DN_CONV):
            y = y + ext_ref[idx, base + j:base + j + tt, :] * cw[j:j + 1, :]
        return _silu(y)

    q_all = conv_silu(0, q_ref, cwq_ref)
    k_all = conv_silu(1, k_ref, cwk_ref)
    v_all = conv_silu(2, v_ref, cwv_ref)

    g_rows = -jnp.exp(alr_ref[...]) * _softplus(grow_ref[...] + dtr_ref[...])
    gcol = gcol_ref[...]
    beta_cols = _sigmoid(gcol)
    g_cols = -jnp.exp(alc_ref[...]) * _softplus(gcol + dtc_ref[...])

    ri = lax.broadcasted_iota(jnp.int32, (c, c), 0)
    ci = lax.broadcasted_iota(jnp.int32, (c, c), 1)
    causal = ri >= ci
    strict = ri > ci
    ident = jnp.where(ri == ci, 1.0, 0.0).astype(F32)
    tril01 = jnp.where(causal, 1.0, 0.0).astype(BF16)
    triu01 = jnp.where(ri <= ci, 1.0, 0.0).astype(BF16)

    items = [(hh, n) for n in range(nchunk) for hh in range(hg)]
    qs, lower, a_qk, e_col, k_dec, g_last, vb = {}, {}, {}, {}, {}, {}, {}
    for n in range(nchunk):
        sl = slice(n * c, (n + 1) * c)
        gcs_c = _dot_hilo_rhs(tril01, g_cols[sl])
        gcs_r = _dot_hilo_lhs(g_rows[:, sl], triu01)
        for hh in range(hg):
            it = (hh, n)
            hs = slice(hh * dk, (hh + 1) * dk)
            qc, kc = q_all[sl, hs], k_all[sl, hs]
            qc = qc * (lax.rsqrt(jnp.sum(qc * qc, axis=-1, keepdims=True) + NORM_EPS) * (dk ** -0.5))
            kc = kc * lax.rsqrt(jnp.sum(kc * kc, axis=-1, keepdims=True) + NORM_EPS)
            beta_col = beta_cols[sl, hh:hh + 1]
            gc_col = gcs_c[:, hg + hh:hg + hh + 1]
            gc_row = gcs_r[hh:hh + 1, :]
            gc_end = gcs_r[hh:hh + 1, c - 1:c]
            decay = jnp.where(causal, jnp.exp(jnp.where(causal, gc_col - gc_row, 0.0)), 0.0)
            kb = kc * beta_col
            res = _dot_nt(jnp.concatenate([kb, qc], axis=0), kc)
            lower[it] = jnp.where(strict, res[0:c] * decay, 0.0)
            a_qk[it] = jnp.where(causal, res[c:2 * c] * decay, 0.0)
            e_col[it] = jnp.exp(gc_col)
            k_dec[it] = kc * jnp.exp(gc_end - gc_col)
            g_last[it] = jnp.exp(gc_end)
            vb[it] = jnp.concatenate([v_all[sl, hh * dv:(hh + 1) * dv] * beta_col, kb * e_col[it]], axis=1)
            qs[it] = qc

    base_shift = DN_BASE.bit_length() - 1
    same_base = (ri >> base_shift) == (ci >> base_shift)
    inv, pw = {}, {}
    for it in items:
        lb = jnp.where(same_base, lower[it], 0.0)
        inv[it] = ident - lb
        pw[it] = _dot3(lb, lb)
    span = 2
    while span < DN_BASE:
        for it in items:
            if span * 2 < DN_BASE:
                res = _dot3(jnp.concatenate([inv[it], pw[it]], axis=0), pw[it])
                inv[it] = inv[it] + res[0:c]
                pw[it] = res[c:2 * c]
            else:
                inv[it] = inv[it] + _dot3(inv[it], pw[it])
        span *= 2
    width = 2 * DN_BASE
    while width <= c:
        sh = width.bit_length() - 1
        same_w = (ri >> sh) == (ci >> sh)
        same_half = (ri >> (sh - 1)) == (ci >> (sh - 1))
        off = {it: jnp.where(same_w, jnp.where(same_half, 0.0, lower[it]), 0.0) for it in items}
        mid = {it: _dot3(off[it], inv[it]) for it in items}
        for it in items:
            inv[it] = inv[it] - _dot3(inv[it], mid[it])
        width *= 2

    uw = {it: _dot3(inv[it], vb[it]) for it in items}
    a_uw = {it: _dot(a_qk[it], uw[it]) for it in items}
    k_uw = {it: _dot_tn(k_dec[it], uw[it]) for it in items}

    nw = nw_ref[...]
    state = [state_ref[hh] for hh in range(hg)]
    for n in range(nchunk):
        sl = slice(n * c, (n + 1) * c)
        for hh in range(hg):
            it = (hh, n)
            q_eff = qs[it] * e_col[it] - a_uw[it][:, dv:dv + dk]
            res = _dot(jnp.concatenate([q_eff, k_uw[it][:, dv:dv + dk]], axis=0), state[hh])
            o = res[0:c] + a_uw[it][:, 0:dv]
            state[hh] = state[hh] * g_last[it] - res[c:c + dk] + k_uw[it][:, 0:dv]
            hs = slice(hh * dv, (hh + 1) * dv)
            o = _rms(o, nw) * _silu(zg_ref[sl, hs].astype(F32))
            o_ref[sl, hs] = o.astype(o_ref.dtype)
    for hh in range(hg):
        state_ref[hh] = state[hh]


def _deltanet(z3, zs3, dn_a_log, dn_dt_bias, conv_w_t, norm_w, tt=256):
    b, s, _ = z3.shape
    nh, hg = DN_HEADS, DN_HG
    ngrp = nh // hg
    wq = hg * DN_DK
    bq, bk, bv, bz = _OFF_DQ // wq, _OFF_DK // wq, _OFF_DV // wq, _OFF_DZ // wq
    db = zs3[..., _SM_DB:_SM_DB + nh].reshape(b, s, ngrp, hg)
    da = zs3[..., _SM_DA:_SM_DA + nh].reshape(b, s, ngrp, hg)
    grow = jnp.pad(da.transpose(0, 2, 3, 1), ((0, 0), (0, 0), (0, SUBLANE - hg), (0, 0)))
    gcol = jnp.pad(jnp.concatenate([db, da], axis=-1).transpose(0, 2, 1, 3),
                   ((0, 0), (0, 0), (0, 0), (0, LANE - 2 * hg)))
    al = dn_a_log.astype(F32).reshape(ngrp, hg)
    dtb = dn_dt_bias.astype(F32).reshape(ngrp, hg)
    alr = jnp.pad(al, ((0, 0), (0, SUBLANE - hg)))[:, :, None]
    dtr = jnp.pad(dtb, ((0, 0), (0, SUBLANE - hg)))[:, :, None]
    alc = jnp.pad(al, ((0, 0), (hg, LANE - 2 * hg)))[:, None, :]
    dtc = jnp.pad(dtb, ((0, 0), (hg, LANE - 2 * hg)))[:, None, :]
    return pl.pallas_call(
        _deltanet_kernel,
        out_shape=jax.ShapeDtypeStruct((b, s, nh * DN_DV), BF16),
        grid=(b, ngrp, s // tt),
        in_specs=[
            pl.BlockSpec((None, tt, wq), lambda bi, g, i: (bi, i, bq + g)),
            pl.BlockSpec((None, tt, wq), lambda bi, g, i: (bi, i, bk + g)),
            pl.BlockSpec((None, tt, wq), lambda bi, g, i: (bi, i, bv + g)),
            pl.BlockSpec((None, tt, wq), lambda bi, g, i: (bi, i, bz + g)),
            pl.BlockSpec((None, None, SUBLANE, tt), lambda bi, g, i: (bi, g, 0, i)),
            pl.BlockSpec((None, None, tt, LANE), lambda bi, g, i: (bi, g, i, 0)),
            pl.BlockSpec((None, SUBLANE, 1), lambda bi, g, i: (g, 0, 0)),
            pl.BlockSpec((None, SUBLANE, 1), lambda bi, g, i: (g, 0, 0)),
            pl.BlockSpec((None, 1, LANE), lambda bi, g, i: (g, 0, 0)),
            pl.BlockSpec((None, 1, LANE), lambda bi, g, i: (g, 0, 0)),
            pl.BlockSpec((DN_CONV, wq), lambda bi, g, i: (0, g)),
            pl.BlockSpec((DN_CONV, wq), lambda bi, g, i: (0, ngrp + g)),
            pl.BlockSpec((DN_CONV, wq), lambda bi, g, i: (0, 2 * ngrp + g)),
            pl.BlockSpec((1, DN_DV), lambda bi, g, i: (0, 0)),
        ],
        out_specs=pl.BlockSpec((None, tt, wq), lambda bi, g, i: (bi, i, g)),
        scratch_shapes=[
            pltpu.VMEM((hg, DN_DK, DN_DV), F32),
            pltpu.VMEM((3, SUBLANE, wq), F32),
            pltpu.VMEM((3, tt + SUBLANE, wq), F32),
        ],
        compiler_params=_cparams(("parallel", "parallel", "arbitrary")),
        name="deltanet",
    )(z3, z3, z3, z3, grow, gcol, alr, dtr, alc, dtc, conv_w_t, conv_w_t, conv_w_t, norm_w)


def _gla_kernel(q_ref, k_ref, v_ref, r_ref, zs_ref, wup_ref, bup_ref, nw_ref, o_ref, state_ref):
    tt = q_ref.shape[0]
    c = CHUNK
    sub = GLA_SUB

    @pl.when(pl.program_id(2) == 0)
    def _():
        state_ref[...] = jnp.zeros_like(state_ref)

    logit = _dot(zs_ref[...], wup_ref[...]) + bup_ref[...]
    log_a = -_softplus(-logit) * (1.0 / GLA_TAU)

    ri = lax.broadcasted_iota(jnp.int32, (c, c), 0)
    ci = lax.broadcasted_iota(jnp.int32, (c, c), 1)
    tril = jnp.where(ri >= ci, 1.0, 0.0).astype(BF16)
    sub_rows = lax.broadcasted_iota(jnp.int32, (sub, 1), 0)
    key_rows = lax.broadcasted_iota(jnp.int32, (c, 1), 0)
    blk_cols = lax.broadcasted_iota(jnp.int32, (sub, c), 1)

    nw = nw_ref[...]
    scale = GLA_DK ** -0.5
    for n in range(tt // c):
        sl = slice(n * c, (n + 1) * c)
        bcum = _dot_hilo_rhs(tril, log_a[sl])
        qc = q_ref[sl, :].astype(F32) * scale
        kc = k_ref[sl, :].astype(F32)
        vc = v_ref[sl, :]
        b_last = bcum[c - 1:c, :]
        q_dec = qc * jnp.exp(bcum)
        k_dec = kc * jnp.exp(b_last - bcum)
        a_last = jnp.exp(b_last)

        row_blocks = []
        for bi in range(c // sub):
            r0 = bi * sub
            q_i = qc[r0:r0 + sub]
            b_i = bcum[r0:r0 + sub]
            if bi > 0:
                b_ref = bcum[r0 - 1:r0, :]
                early = key_rows < r0
                q_f = q_i * jnp.exp(b_i - b_ref)
                k_f = jnp.where(early, kc * jnp.exp(jnp.where(early, b_ref - bcum, 0.0)), 0.0)
                blk = _dot_nt(q_f, k_f)
            else:
                blk = jnp.zeros((sub, c), F32)
            for j in range(sub):
                kj = kc[r0 + j:r0 + j + 1, :]
                bj = bcum[r0 + j:r0 + j + 1, :]
                live = sub_rows >= j
                dec = jnp.exp(jnp.where(live, b_i - bj, 0.0))
                col = jnp.sum(jnp.where(live, q_i * kj * dec, 0.0), axis=1, keepdims=True)
                blk = jnp.where(blk_cols == r0 + j, col, blk)
            row_blocks.append(blk)
        a_qk = jnp.concatenate(row_blocks, axis=0)

        state_t = state_ref[...]
        o = _dot_nt(q_dec, state_t) + _dot(a_qk, vc)
        state_ref[...] = state_t * a_last + _dot_tn(vc, k_dec)

        o = _rms(o, nw) * _silu(r_ref[sl, :].astype(F32))
        o_ref[sl, :] = o.astype(o_ref.dtype)


def _gla(z3, zs3, w_up_pad, b_up, norm_w, tt=256):
    b, s, _ = z3.shape
    hq, hk = _OFF_GQ // GLA_DK, _OFF_GK // GLA_DK
    hv, hr = _OFF_GV // GLA_DV, _OFF_GR // GLA_DV
    ns = zs3.shape[-1]
    return pl.pallas_call(
        _gla_kernel,
        out_shape=jax.ShapeDtypeStruct((b, s, GLA_HEADS * GLA_DV), BF16),
        grid=(b, GLA_HEADS, s // tt),
        in_specs=[
            pl.BlockSpec((None, tt, GLA_DK), lambda bi, h, i: (bi, i, hq + h)),
            pl.BlockSpec((None, tt, GLA_DK), lambda bi, h, i: (bi, i, hk + h)),
            pl.BlockSpec((None, tt, GLA_DV), lambda bi, h, i: (bi, i, hv + h)),
            pl.BlockSpec((None, tt, GLA_DV), lambda bi, h, i: (bi, i, hr + h)),
            pl.BlockSpec((None, tt, ns), lambda bi, h, i: (bi, i, 0)),
            pl.BlockSpec((ns, GLA_DK), lambda bi, h, i: (0, h)),
            pl.BlockSpec((1, GLA_DK), lambda bi, h, i: (0, h)),
            pl.BlockSpec((1, GLA_DV), lambda bi, h, i: (0, 0)),
        ],
        out_specs=pl.BlockSpec((None, tt, GLA_DV), lambda bi, h, i: (bi, i, h)),
        scratch_shapes=[pltpu.VMEM((GLA_DV, GLA_DK), F32)],
        compiler_params=_cparams(("parallel", "parallel", "arbitrary")),
        name="gla",
    )(z3, z3, z3, z3, zs3, w_up_pad, b_up, norm_w)


def _s5_kernel(su_ref, kblk_ref, vre_ref, vim_ref, wre_ref, wim_ref, al_ref, d_ref,
               o_ref, m_ref, v_ref, w_ref, e_ref, xp_ref):
    ll = S5_L
    rows = e_ref.shape[0]
    nst = S5_LB * S5_STATE

    @pl.when(pl.program_id(1) == 0)
    def _():
        zero_blk = jnp.zeros((LANE, LANE), BF16)
        for s in range(ll):
            for i in range(ll):
                m_ref[s * LANE:(s + 1) * LANE, i * LANE:(i + 1) * LANE] = (
                    kblk_ref[i - s] if i >= s else zero_blk)
        rblk = 2 * LANE
        vr = lax.broadcasted_iota(jnp.int32, (rblk, nst), 0)
        vc = lax.broadcasted_iota(jnp.int32, (rblk, nst), 1)
        keep_v = ((vr & (LANE - 1)) >> 4) == (vc >> 6)
        for r in range(ll * LANE // rblk):
            rs = slice(r * rblk, (r + 1) * rblk)
            v_ref[rs, 0:nst] = jnp.where(keep_v, jnp.tile(vre_ref[rs, :], (1, nst // LANE)), 0.0).astype(BF16)
            v_ref[rs, nst:2 * nst] = jnp.where(keep_v, jnp.tile(vim_ref[rs, :], (1, nst // LANE)), 0.0).astype(BF16)
        wr = lax.broadcasted_iota(jnp.int32, (nst, rblk), 0)
        wc = lax.broadcasted_iota(jnp.int32, (nst, rblk), 1)
        keep_w = (wr >> 6) == ((wc & (LANE - 1)) >> 4)
        for r in range(ll * LANE // rblk):
            cs = slice(r * rblk, (r + 1) * rblk)
            w_ref[0:nst, cs] = jnp.where(keep_w, jnp.tile(wre_ref[:, cs], (S5_LB, 1)), 0.0).astype(BF16)
            w_ref[nst:2 * nst, cs] = jnp.where(keep_w, jnp.tile(wim_ref[:, cs], (S5_LB, 1)), 0.0).astype(BF16)

    u_list = [su_ref[pl.ds(s, rows, stride=ll), :] for s in range(ll)]
    u_cat = jnp.concatenate([u.astype(BF16) for u in u_list], axis=1)
    e_ref[...] = jnp.dot(u_cat, v_ref[...], preferred_element_type=F32)
    ar = al_ref[:, 0:nst]
    ai = al_ref[:, nst:2 * nst]

    def step(cidx, carry):
        xr, xi = carry
        row = pl.ds(cidx, 1)
        xp_ref[row, 0:nst] = xr
        xp_ref[row, nst:2 * nst] = xi
        er = e_ref[row, 0:nst]
        ei = e_ref[row, nst:2 * nst]
        return ar * xr - ai * xi + er, ar * xi + ai * xr + ei

    zero = jnp.zeros((1, nst), F32)
    lax.fori_loop(0, rows, step, (zero, zero), unroll=8)

    y = (jnp.dot(u_cat, m_ref[...], preferred_element_type=F32)
         + jnp.dot(xp_ref[...].astype(BF16), w_ref[...], preferred_element_type=F32))
    d = d_ref[...]
    for i in range(ll):
        yi = y[:, i * LANE:(i + 1) * LANE] + d * u_list[i]
        o_ref[pl.ds(i, rows, stride=ll), :] = _gelu_tanh(yi)


def _s5(su3, mats):
    kblk, vre, vim, wre, wim, al, dskip = mats
    nbatch, s, width = su3.shape
    ll = S5_L
    rows = s // ll
    nblk = width // LANE
    nst2 = 2 * S5_LB * S5_STATE

    return pl.pallas_call(
        _s5_kernel,
        out_shape=jax.ShapeDtypeStruct((nbatch, s, width), F32),
        grid=(nblk, nbatch),
        in_specs=[
            pl.BlockSpec((None, s, LANE), lambda j, b: (b, 0, j)),
            pl.BlockSpec((None, ll, LANE, LANE), lambda j, b: (j, 0, 0, 0)),
            pl.BlockSpec((None, ll * LANE, LANE), lambda j, b: (j, 0, 0)),
            pl.BlockSpec((None, ll * LANE, LANE), lambda j, b: (j, 0, 0)),
            pl.BlockSpec((None, S5_STATE, ll * LANE), lambda j, b: (j, 0, 0)),
            pl.BlockSpec((None, S5_STATE, ll * LANE), lambda j, b: (j, 0, 0)),
            pl.BlockSpec((None, 1, nst2), lambda j, b: (j, 0, 0)),
            pl.BlockSpec((None, 1, LANE), lambda j, b: (j, 0, 0)),
        ],
        out_specs=pl.BlockSpec((None, s, LANE), lambda j, b: (b, 0, j)),
        scratch_shapes=[
            pltpu.VMEM((ll * LANE, ll * LANE), BF16),
            pltpu.VMEM((ll * LANE, nst2), BF16),
            pltpu.VMEM((nst2, ll * LANE), BF16),
            pltpu.VMEM((rows, nst2), F32),
            pltpu.VMEM((rows, nst2), F32),
        ],
        compiler_params=_cparams(("parallel", "arbitrary")),
        name="s5",
    )(su3, kblk, vre, vim, wre, wim, al, dskip)


def _s5_matrices(lam_re, lam_im, log_step, b_re, b_im, c_re, c_im, d_skip):
    g, p, hh, ll, lb = S5_GROUPS, S5_STATE, S5_GROUP, S5_L, S5_LB
    nb = g // lb
    lr = jnp.minimum(lam_re.astype(F32), -1e-4)
    li = lam_im.astype(F32)
    dt = jnp.exp(log_step.astype(F32))[:, None]
    mag = jnp.exp(lr * dt)
    ar = mag * jnp.cos(li * dt)
    ai = mag * jnp.sin(li * dt)
    nr = ar - 1.0
    den = lr * lr + li * li
    zr = (nr * lr + ai * li) / den
    zi = (ai * lr - nr * li) / den
    br = b_re.astype(F32)
    bi = b_im.astype(F32)
    bbr = zr[..., None] * br - zi[..., None] * bi
    bbi = zr[..., None] * bi + zi[..., None] * br
    pr = [jnp.ones_like(ar)]
    pi = [jnp.zeros_like(ai)]
    for _ in range(ll):
        pr.append(pr[-1] * ar - pi[-1] * ai)
        pi.append(pr[-2] * ai + pi[-1] * ar)
    pr_rev = jnp.stack(pr[ll - 1::-1], axis=0)
    pi_rev = jnp.stack(pi[ll - 1::-1], axis=0)
    pr = jnp.stack(pr, axis=0)
    pi = jnp.stack(pi, axis=0)
    cr = c_re.astype(F32)
    cim = c_im.astype(F32)

    def blk(a, *shape):
        return a.reshape((nb, lb) + shape)

    pr_d = jnp.moveaxis(pr.reshape(ll + 1, nb, lb, p), 0, 1)[:, :, :, None, None, :]
    pi_d = jnp.moveaxis(pi.reshape(ll + 1, nb, lb, p), 0, 1)[:, :, :, None, None, :]
    cr_b = blk(cr, hh, p)[:, None, :, None, :, :]
    ci_b = blk(cim, hh, p)[:, None, :, None, :, :]
    qr = cr_b * pr_d - ci_b * pi_d
    qi = cr_b * pi_d + ci_b * pr_d
    bbr_b = blk(jnp.swapaxes(bbr, 1, 2), hh, p)[:, None, :, :, None, :]
    bbi_b = blk(jnp.swapaxes(bbi, 1, 2), hh, p)[:, None, :, :, None, :]
    kd = jnp.sum(qr[:, :ll] * bbr_b - qi[:, :ll] * bbi_b, axis=-1)
    eye = jnp.eye(lb, dtype=F32)[None, None, :, None, :, None]
    kblk = (kd[:, :, :, :, None, :] * eye).reshape(nb, ll, lb * hh, lb * hh)
    prs = jnp.moveaxis(pr_rev.reshape(ll, nb, lb, p), 0, 1)[:, :, :, None, :]
    pis = jnp.moveaxis(pi_rev.reshape(ll, nb, lb, p), 0, 1)[:, :, :, None, :]
    bbr_v, bbi_v = bbr_b[:, :, :, :, 0, :], bbi_b[:, :, :, :, 0, :]
    vre = jnp.tile((prs * bbr_v - pis * bbi_v).reshape(nb, ll * lb * hh, p), (1, 1, 2))
    vim = jnp.tile((prs * bbi_v + pis * bbr_v).reshape(nb, ll * lb * hh, p), (1, 1, 2))
    wre = jnp.moveaxis(qr[:, 1:ll + 1, :, 0], -1, 1).reshape(nb, p, ll * lb * hh)
    wim = -jnp.moveaxis(qi[:, 1:ll + 1, :, 0], -1, 1).reshape(nb, p, ll * lb * hh)

    al = jnp.concatenate([pr[ll].reshape(nb, 1, lb * p), pi[ll].reshape(nb, 1, lb * p)], axis=2)
    dskip = d_skip.astype(F32).reshape(nb, 1, lb * hh)
    return kblk.astype(BF16), vre, vim, wre, wim, al, dskip


def _mix_out_kernel(oa_ref, ob_ref, ys_ref, ga_ref, gb_ref, gc_ref, x_ref,
                    wa_ref, wb_ref, wca_ref, wcb_ref, wm_ref, gpost_ref, o_ref, acc_ref):
    n = pl.program_id(1)

    @pl.when(n == 0)
    def _():
        acc_ref[...] = jnp.zeros_like(acc_ref)

    y_a = jnp.dot(oa_ref[...], wa_ref[...], preferred_element_type=F32)
    y_b = jnp.dot(ob_ref[...], wb_ref[...], preferred_element_type=F32)
    ys = ys_ref[...].astype(BF16)
    glu_a =jnp.dot(ys, wca_ref[...], preferred_element_type=F32)
    glu_b = jnp.dot(ys, wcb_ref[...], preferred_element_type=F32)
    y_c = glu_a * _sigmoid(glu_b)
    merged = (_sigmoid(ga_ref[...].astype(F32)) * y_a + _sigmoid(gb_ref[...].astype(F32)) * y_b
              + _sigmoid(gc_ref[...].astype(F32)) * y_c)
    acc_ref[...] += jnp.dot(merged.astype(BF16), wm_ref[...], preferred_element_type=F32)

    @pl.when(n == pl.num_programs(1) - 1)
    def _():
        o_ref[...] = x_ref[...] + _rms(acc_ref[...], gpost_ref[...])


def _mix_out(oa, ob, ys, z2, x2, w_dn_out, w_gla_out, w_s5_glu, w_mix_out, g_post, tm=512, tn=512):
    t, d = x2.shape
    nsteps = d // tn
    gblk = _OFF_GATES // tn
    ka, kb, kc = oa.shape[1], ob.shape[1], ys.shape[1]
    return pl.pallas_call(
        _mix_out_kernel,
        out_shape=jax.ShapeDtypeStruct((t, d), F32),
        grid=(t // tm, nsteps),
        in_specs=[
            pl.BlockSpec((tm, ka), lambda i, n: (i, 0)),
            pl.BlockSpec((tm, kb), lambda i, n: (i, 0)),
            pl.BlockSpec((tm, kc), lambda i, n: (i, 0)),
            pl.BlockSpec((tm, tn), lambda i, n: (i, gblk + n)),
            pl.BlockSpec((tm, tn), lambda i, n: (i, gblk + nsteps + n)),
            pl.BlockSpec((tm, tn), lambda i, n: (i, gblk + 2 * nsteps + n)),
            pl.BlockSpec((tm, d), lambda i, n: (i, 0)),
            pl.BlockSpec((ka, tn), lambda i, n: (0, n)),
            pl.BlockSpec((kb, tn), lambda i, n: (0, n)),
            pl.BlockSpec((kc, tn), lambda i, n: (0, n)),
            pl.BlockSpec((kc, tn), lambda i, n: (0, nsteps + n)),
            pl.BlockSpec((tn, d), lambda i, n: (n, 0)),
            pl.BlockSpec((1, d), lambda i, n: (0, 0)),
        ],
        out_specs=pl.BlockSpec((tm, d), lambda i, n: (i, 0)),
        scratch_shapes=[pltpu.VMEM((tm, d), F32)],
        compiler_params=_cparams(("parallel", "arbitrary")),
        name="mix_out",
    )(oa, ob, ys, z2, z2, z2, x2, w_dn_out, w_gla_out, w_s5_glu, w_s5_glu, w_mix_out, g_post)


def _xattn_kernel(x_ref, gpre_ref, wq_ref, kv_ref, wo_ref, gpost_ref, o_ref):
    x = x_ref[...]
    hn = _rms(x, gpre_ref[...]).astype(BF16)
    q = jnp.dot(hn, wq_ref[...], preferred_element_type=F32)
    kv = kv_ref[...]
    outs = []
    for h in range(XA_HEADS):
        qh = q[:, h * XA_DH:(h + 1) * XA_DH]
        kh = kv[:, h * XA_DH:(h + 1) * XA_DH]
        vh = kv[:, (XA_HEADS + h) * XA_DH:(XA_HEADS + h + 1) * XA_DH]
        s = _dot_nt(qh, kh) * (XA_DH ** -0.5)
        s = s - jnp.max(s, axis=-1, keepdims=True)
        p = jnp.exp(s)
        p = p / jnp.sum(p, axis=-1, keepdims=True)
        outs.append(_dot(p, vh))
    o = jnp.concatenate(outs, axis=1)
    y = jnp.dot(o.astype(BF16), wo_ref[...], preferred_element_type=F32)
    o_ref[...] = x + _rms(y, gpost_ref[...])


def _xattn(x3, g_pre, w_q, kv3, w_out, g_post, tm=512):
    b, s, d = x3.shape
    m, nkv = kv3.shape[1], kv3.shape[2]
    nq = w_q.shape[1]
    return pl.pallas_call(
        _xattn_kernel,
        out_shape=jax.ShapeDtypeStruct((b, s, d), F32),
        grid=(b, s // tm),
        in_specs=[
            pl.BlockSpec((None, tm, d), lambda bi, i: (bi, i, 0)),
            pl.BlockSpec((1, d), lambda bi, i: (0, 0)),
            pl.BlockSpec((d, nq), lambda bi, i: (0, 0)),
            pl.BlockSpec((None, m, nkv), lambda bi, i: (bi, 0, 0)),
            pl.BlockSpec((nq, d), lambda bi, i: (0, 0)),
            pl.BlockSpec((1, d), lambda bi, i: (0, 0)),
        ],
        out_specs=pl.BlockSpec((None, tm, d), lambda bi, i: (bi, i, 0)),
        compiler_params=_cparams(("parallel", "parallel")),
        name="xattn",
    )(x3, g_pre, w_q, kv3, w_out, g_post)


def _convffn_kernel(x_ref, xh_ref, gpre_ref, wg_ref, wu_ref, cw_ref, wd_ref, gpost_ref,
                    o_ref, hn_ref, hh_ref, gate_ref, acc_ref):
    i = pl.program_id(1)
    f = pl.program_id(2)
    tm = x_ref.shape[0]
    halo = SUBLANE

    @pl.when(f == 0)
    def _():
        hn_ref[...] = _rms(x_ref[...], gpre_ref[...]).astype(BF16)
        keep = jnp.where(i > 0, 1.0, 0.0)
        hh_ref[...] = (_rms(xh_ref[...], gpre_ref[...]) * keep).astype(BF16)
        acc_ref[...] = jnp.zeros_like(acc_ref)

    hn = hn_ref[...]
    wg = wg_ref[...]
    gate_ref[0:halo, :] = jnp.dot(hh_ref[...], wg, preferred_element_type=F32)
    gate_ref[halo:halo + tm, :] = jnp.dot(hn, wg, preferred_element_type=F32)
    up = jnp.dot(hn, wu_ref[...], preferred_element_type=F32)
    cw = cw_ref[...]
    base = halo - (FFN_CONV - 1)
    conv = gate_ref[base:base + tm, :] * cw[0:1, :]
    for j in range(1, FFN_CONV):
        conv = conv + gate_ref[base + j:base + j + tm, :] * cw[j:j + 1, :]
    act = (_gelu_tanh(conv) * up).astype(BF16)
    acc_ref[...] += jnp.dot(act, wd_ref[...], preferred_element_type=F32)

    @pl.when(f == pl.num_programs(2) - 1)
    def _():
        o_ref[...] = x_ref[...] + _rms(acc_ref[...], gpost_ref[...])


def _convffn(x3, g_pre, w_up, conv_w_t, w_down, g_post, tm=512, tf=512):
    b, s, d = x3.shape
    fh = w_down.shape[0]
    nf = fh // tf
    hb = tm // SUBLANE
    return pl.pallas_call(
        _convffn_kernel,
        out_shape=jax.ShapeDtypeStruct((b, s, d), F32),
        grid=(b, s // tm, nf),
        in_specs=[
            pl.BlockSpec((None, tm, d), lambda bi, i, f: (bi, i, 0)),
            pl.BlockSpec((None, SUBLANE, d), lambda bi, i, f: (bi, jnp.maximum(i * hb - 1, 0), 0)),
            pl.BlockSpec((1, d), lambda bi, i, f: (0, 0)),
            pl.BlockSpec((d, tf), lambda bi, i, f: (0, f)),
            pl.BlockSpec((d, tf), lambda bi, i, f: (0, nf + f)),
            pl.BlockSpec((FFN_CONV, tf), lambda bi, i, f: (0, f)),
            pl.BlockSpec((tf, d), lambda bi, i, f: (f, 0)),
            pl.BlockSpec((1, d), lambda bi, i, f: (0, 0)),
        ],
        out_specs=pl.BlockSpec((None, tm, d), lambda bi, i, f: (bi, i, 0)),
        scratch_shapes=[
            pltpu.VMEM((tm, d), BF16),
            pltpu.VMEM((SUBLANE, d), BF16),
            pltpu.VMEM((tm + SUBLANE, tf), F32),
            pltpu.VMEM((tm, d), F32),
        ],
        compiler_params=_cparams(("parallel", "parallel", "arbitrary")),
        name="convffn",
    )(x3, x3, g_pre, w_up, w_up, conv_w_t, w_down, g_post)


def _split_w_in(w_in_l):
    sizes = (DN_HEADS * DN_DK, DN_HEADS * DN_DK, DN_HEADS * DN_DV, DN_HEADS * DN_DV, DN_HEADS,
             DN_HEADS, GLA_HEADS * GLA_DK, GLA_HEADS * GLA_DK, GLA_HEADS * GLA_DV, GLA_RANK,
             GLA_HEADS * GLA_DV, S5_WIDTH, 3 * D_MODEL)
    parts = []
    acc = 0
    for n in sizes:
        parts.append(w_in_l[:, acc:acc + n])
        acc += n
    (dq, dk, dv, dz, db, da, gq, gk, gv, glow, gr, su, gates) = parts
    big = jnp.concatenate([dq, dk, dv, dz, gq, gk, gv, gr, su, gates], axis=1).astype(BF16)
    pad = jnp.zeros((w_in_l.shape[0], LANE - 2 * DN_HEADS - GLA_RANK), w_in_l.dtype)
    small = jnp.concatenate([db, da, glow, pad], axis=1).astype(BF16)
    return big, small


def kernel(x, mem, norm_mix_pre, norm_mix_post, w_in, dn_conv_w, dn_a_log, dn_dt_bias, dn_norm_w, w_dn_out, gla_w_up, gla_b_up, gla_norm_w, w_gla_out, s5_lam_re, s5_lam_im, s5_log_step, s5_b_re, s5_b_im, s5_c_re, s5_c_im, s5_d, w_s5_glu, w_mix_out, norm_xa_pre, norm_xa_post, norm_mem, w_xa_q, w_xa_kv, w_xa_out, norm_ffn_pre, norm_ffn_post, w_ffn_up, ffn_conv_w, w_ffn_down):
    b, s, d = x.shape
    t = b * s
    depth = w_in.shape[0]
    mem2 = mem.reshape(-1, d)
    row = lambda a: a.reshape(1, -1).astype(F32)

    for l in range(depth):
        w_big, w_small = _split_w_in(w_in[l])
        z2, zs2, su2 = _in_proj(x.reshape(t, d), row(norm_mix_pre[l]), w_big, w_small)
        z3 = z2.reshape(b, s, -1)
        zs3 = zs2.reshape(b, s, -1)

        oa = _deltanet(z3, zs3, dn_a_log[l], dn_dt_bias[l], dn_conv_w[l].T.astype(F32),
                       row(dn_norm_w[l]))

        w_up_pad = jnp.zeros((zs2.shape[1], GLA_HEADS * GLA_DK), F32)
        w_up_pad = w_up_pad.at[_SM_GLOW:_SM_GLOW + GLA_RANK].set(gla_w_up[l]).astype(BF16)
        ob = _gla(z3, zs3, w_up_pad, row(gla_b_up[l]), row(gla_norm_w[l]))

        mats = _s5_matrices(s5_lam_re[l], s5_lam_im[l], s5_log_step[l], s5_b_re[l], s5_b_im[l],
                            s5_c_re[l], s5_c_im[l], s5_d[l])
        ys = _s5(su2.reshape(b, s, -1), mats)

        x = _mix_out(oa.reshape(t, -1), ob.reshape(t, -1), ys.reshape(t, -1), z2, x.reshape(t, d),
                     w_dn_out[l].astype(BF16), w_gla_out[l].astype(BF16), w_s5_glu[l].astype(BF16),
                     w_mix_out[l].astype(BF16), row(norm_mix_post[l])).reshape(b, s, d)

        kv = _norm_proj(mem2, row(norm_mem[l]), w_xa_kv[l].astype(BF16))
        x = _xattn(x, row(norm_xa_pre[l]), w_xa_q[l].astype(BF16), kv.reshape(b, -1, kv.shape[-1]),
                   w_xa_out[l].astype(BF16), row(norm_xa_post[l]))

        x = _convffn(x, row(norm_ffn_pre[l]), w_ffn_up[l].astype(BF16), ffn_conv_w[l].T.astype(F32),
                     w_ffn_down[l].astype(BF16), row(norm_ffn_post[l]))
    return x
```

```python
import jax
import jax.numpy as jnp
from jax import lax
from jax.experimental import pallas as pl
from jax.experimental.pallas import tpu as pltpu

F32 = jnp.float32
BF16 = jnp.bfloat16

NORM_EPS = 1e-6

D_MODEL = 2048
DN_HEADS, DN_DK, DN_DV, DN_CONV = 8, 128, 128, 4
GLA_HEADS, GLA_DK, GLA_DV, GLA_RANK, GLA_TAU = 4, 128, 256, 16, 16.0
S5_GROUP, S5_GROUPS, S5_STATE = 16, 64, 64
S5_WIDTH = S5_GROUP * S5_GROUPS
XA_HEADS, XA_DH = 4, 128
FFN_HIDDEN, FFN_CONV = 5632, 3

CHUNK = 64
GLA_SUB = 16
S5_L = 16
DN_HG = 4
DN_BASE = 8
LANE = 128
S5_LB = LANE // S5_GROUP
SUBLANE = 8
VMEM_LIMIT = 56 * 1024 * 1024

_OFF_DQ, _OFF_DK, _OFF_DV, _OFF_DZ = 0, 1024, 2048, 3072
_OFF_GQ, _OFF_GK, _OFF_GV, _OFF_GR = 4096, 4608, 5120, 6144
_OFF_SU, _OFF_GATES = 7168, 8192
_BIG_WIDTH = 14336
_SM_DB, _SM_DA, _SM_GLOW = 0, 8, 16


def _cparams(sem):
    return pltpu.CompilerParams(dimension_semantics=sem, vmem_limit_bytes=VMEM_LIMIT)


def _rms(xf, g):
    return xf * lax.rsqrt(jnp.mean(xf * xf, axis=-1, keepdims=True) + NORM_EPS) * g


def _dot(a, b):
    return jnp.dot(a.astype(BF16), b.astype(BF16), preferred_element_type=F32)


def _dot_nt(a, b):
    return lax.dot_general(a.astype(BF16), b.astype(BF16), (((1,), (1,)), ((), ())),
                           preferred_element_type=F32)


def _dot_tn(a, b):
    return lax.dot_general(a.astype(BF16), b.astype(BF16), (((0,), (0,)), ((), ())),
                           preferred_element_type=F32)


def _sigmoid(x):
    return 0.5 * jnp.tanh(0.5 * x) + 0.5


def _silu(x):
    return x * _sigmoid(x)


def _softplus(x):
    return jnp.maximum(x, 0.0) + jnp.log(1.0 + jnp.exp(-jnp.abs(x)))


def _gelu_tanh(x):
    c = 0.7978845608028654
    return 0.5 * x * (1.0 + jnp.tanh(c * (x + 0.044715 * (x * x * x))))


def _in_proj_kernel(x_ref, g_ref, w_ref, ws_ref, o_ref, os_ref, su_ref, hn_ref):
    j = pl.program_id(1)
    tn = o_ref.shape[1]

    @pl.when(j == 0)
    def _():
        hn = _rms(x_ref[...], g_ref[...]).astype(BF16)
        hn_ref[...] = hn
        os_ref[...] = jnp.dot(hn, ws_ref[...], preferred_element_type=F32)

    acc = jnp.dot(hn_ref[...], w_ref[...], preferred_element_type=F32)
    o_ref[...] = acc.astype(o_ref.dtype)

    @pl.when(jnp.logical_and(j >= _OFF_SU // tn, j < (_OFF_SU + S5_WIDTH) // tn))
    def _():
        su_ref[...] = acc


def _in_proj(x2, g, w_big, w_small, tm=1024, tn=1024):
    t, d = x2.shape
    n = w_big.shape[1]
    ns = w_small.shape[1]
    j0, nsu = _OFF_SU // tn, S5_WIDTH // tn
    return pl.pallas_call(
        _in_proj_kernel,
        out_shape=(jax.ShapeDtypeStruct((t, n), BF16), jax.ShapeDtypeStruct((t, ns), F32),
                   jax.ShapeDtypeStruct((t, S5_WIDTH), F32)),
        grid=(t // tm, n // tn),
        in_specs=[
            pl.BlockSpec((tm, d), lambda i, j: (i, 0)),
            pl.BlockSpec((1, d), lambda i, j: (0, 0)),
            pl.BlockSpec((d, tn), lambda i, j: (0, j)),
            pl.BlockSpec((d, ns), lambda i, j: (0, 0)),
        ],
        out_specs=(
            pl.BlockSpec((tm, tn), lambda i, j: (i, j)),
            pl.BlockSpec((tm, ns), lambda i, j: (i, 0)),
            pl.BlockSpec((tm, tn), lambda i, j: (i, jnp.clip(j - j0, 0, nsu - 1))),
        ),
        scratch_shapes=[pltpu.VMEM((tm, d), BF16)],
        compiler_params=_cparams(("parallel", "arbitrary")),
        name="in_proj",
    )(x2, g, w_big, w_small)


def _norm_proj_kernel(x_ref, g_ref, w_ref, o_ref):
    hn = _rms(x_ref[...], g_ref[...]).astype(BF16)
    o_ref[...] = jnp.dot(hn, w_ref[...], preferred_element_type=F32).astype(o_ref.dtype)


def _norm_proj(x2, g, w, tm=512):
    t, d = x2.shape
    n = w.shape[1]
    return pl.pallas_call(
        _norm_proj_kernel,
        out_shape=jax.ShapeDtypeStruct((t, n), BF16),
        grid=(t // tm,),
        in_specs=[
            pl.BlockSpec((tm, d), lambda i: (i, 0)),
            pl.BlockSpec((1, d), lambda i: (0, 0)),
            pl.BlockSpec((d, n), lambda i: (0, 0)),
        ],
        out_specs=pl.BlockSpec((tm, n), lambda i: (i, 0)),
        compiler_params=_cparams(("parallel",)),
        name="mem_kv_proj",
    )(x2, g, w)


def _dot_hilo_rhs(a01, b):
    hi = b.astype(BF16)
    lo = (b - hi.astype(F32)).astype(BF16)
    return (jnp.dot(a01, hi, preferred_element_type=F32) + jnp.dot(a01, lo, preferred_element_type=F32))


def _dot3(a, b):
    ah = a.astype(BF16)
    al = (a - ah.astype(F32)).astype(BF16)
    bh = b.astype(BF16)
    bl = (b - bh.astype(F32)).astype(BF16)
    return (jnp.dot(ah, bh, preferred_element_type=F32)
            + (jnp.dot(ah, bl, preferred_element_type=F32) + jnp.dot(al, bh, preferred_element_type=F32)))


def _dot_hilo_lhs(a, b01):
    hi = a.astype(BF16)
    lo = (a - hi.astype(F32)).astype(BF16)
    return (jnp.dot(hi, b01, preferred_element_type=F32) + jnp.dot(lo, b01, preferred_element_type=F32))


def _deltanet_kernel(q_ref, k_ref, v_ref, zg_ref, grow_ref, gcol_ref, alr_ref, dtr_ref, alc_ref,
                     dtc_ref, cwq_ref, cwk_ref, cwv_ref, nw_ref, o_ref, state_ref, tail_ref, ext_ref):
    tt = q_ref.shape[0]
    c = CHUNK
    hg = DN_HG
    dk, dv = DN_DK, DN_DV
    halo = SUBLANE
    nchunk = tt // c

    @pl.when(pl.program_id(2) == 0)
    def _():
        state_ref[...] = jnp.zeros_like(state_ref)
        tail_ref[...] = jnp.zeros_like(tail_ref)

    def conv_silu(idx, raw_ref, cw_ref):
        raw = raw_ref[...].astype(F32)
        ext_ref[idx, 0:halo, :] = tail_ref[idx]
        ext_ref[idx, halo:halo + tt, :] = raw
        tail_ref[idx] = raw[tt - halo:tt, :]
        cw = cw_ref[...]
        base = halo - (DN_CONV - 1)
        y = ext_ref[idx, base:base + tt, :] * cw[0:1, :]
        for j in range(1, DN_CONV):
            y = y + ext_ref[idx, base + j:base + j + tt, :] * cw[j:j + 1, :]
        return _silu(y)

    q_all = conv_silu(0, q_ref, cwq_ref)
    k_all = conv_silu(1, k_ref, cwk_ref)
    v_all = conv_silu(2, v_ref, cwv_ref)

    g_rows = -jnp.exp(alr_ref[...]) * _softplus(grow_ref[...] + dtr_ref[...])
    gcol = gcol_ref[...]
    beta_cols = _sigmoid(gcol)
    g_cols = -jnp.exp(alc_ref[...]) * _softplus(gcol + dtc_ref[...])

    ri = lax.broadcasted_iota(jnp.int32, (c, c), 0)
    ci = lax.broadcasted_iota(jnp.int32, (c, c), 1)
    causal = ri >= ci
    strict = ri > ci
    ident = jnp.where(ri == ci, 1.0, 0.0).astype(F32)
    tril01 = jnp.where(causal, 1.0, 0.0).astype(BF16)
    triu01 = jnp.where(ri <= ci, 1.0, 0.0).astype(BF16)

    items = [(hh, n) for n in range(nchunk) for hh in range(hg)]
    qs, lower, a_qk, e_col, k_dec, g_last, vb = {}, {}, {}, {}, {}, {}, {}
    for n in range(nchunk):
        sl = slice(n * c, (n + 1) * c)
        gcs_c = _dot_hilo_rhs(tril01, g_cols[sl])
        gcs_r = _dot_hilo_lhs(g_rows[:, sl], triu01)
        for hh in range(hg):
            it = (hh, n)
            hs = slice(hh * dk, (hh + 1) * dk)
            qc, kc = q_all[sl, hs], k_all[sl, hs]
            qc = qc * (lax.rsqrt(jnp.sum(qc * qc, axis=-1, keepdims=True) + NORM_EPS) * (dk ** -0.5))
            kc = kc * lax.rsqrt(jnp.sum(kc * kc, axis=-1, keepdims=True) + NORM_EPS)
            beta_col = beta_cols[sl, hh:hh + 1]
            gc_col = gcs_c[:, hg + hh:hg + hh + 1]
            gc_row = gcs_r[hh:hh + 1, :]
            gc_end = gcs_r[hh:hh + 1, c - 1:c]
            decay = jnp.where(causal, jnp.exp(jnp.where(causal, gc_col - gc_row, 0.0)), 0.0)
            kb = kc * beta_col
            res = _dot_nt(jnp.concatenate([kb, qc], axis=0), kc)
            lower[it] = jnp.where(strict, res[0:c] * decay, 0.0)
            a_qk[it] = jnp.where(causal, res[c:2 * c] * decay, 0.0)
            e_col[it] = jnp.exp(gc_col)
            k_dec[it] = kc * jnp.exp(gc_end - gc_col)
            g_last[it] = jnp.exp(gc_end)
            vb[it] = jnp.concatenate([v_all[sl, hh * dv:(hh + 1) * dv] * beta_col, kb * e_col[it]], axis=1)
            qs[it] = qc

    base_shift = DN_BASE.bit_length() - 1
    same_base = (ri >> base_shift) == (ci >> base_shift)
    inv, pw = {}, {}
    for it in items:
        lb = jnp.where(same_base, lower[it], 0.0)
        inv[it] = ident - lb
        pw[it] = _dot3(lb, lb)
    span = 2
    while span < DN_BASE:
        for it in items:
            if span * 2 < DN_BASE:
                res = _dot3(jnp.concatenate([inv[it], pw[it]], axis=0), pw[it])
                inv[it] = inv[it] + res[0:c]
                pw[it] = res[c:2 * c]
            else:
                inv[it] = inv[it] + _dot3(inv[it], pw[it])
        span *= 2
    width = 2 * DN_BASE
    while width <= c:
        sh = width.bit_length() - 1
        same_w = (ri >> sh) == (ci >> sh)
        same_half = (ri >> (sh - 1)) == (ci >> (sh - 1))
        off = {it: jnp.where(same_w, jnp.where(same_half, 0.0, lower[it]), 0.0) for it in items}
        mid = {it: _dot3(off[it], inv[it]) for it in items}
        for it in items:
            inv[it] = inv[it] - _dot3(inv[it], mid[it])
        width *= 2

    uw = {it: _dot3(inv[it], vb[it]) for it in items}
    a_uw = {it: _dot(a_qk[it], uw[it]) for it in items}
    k_uw = {it: _dot_tn(k_dec[it], uw[it]) for it in items}

    nw = nw_ref[...]
    state = [state_ref[hh] for hh in range(hg)]
    for n in range(nchunk):
        sl = slice(n * c, (n + 1) * c)
        for hh in range(hg):
            it = (hh, n)
            q_eff = qs[it] * e_col[it] - a_uw[it][:, dv:dv + dk]
            res = _dot(jnp.concatenate([q_eff, k_uw[it][:, dv:dv + dk]], axis=0), state[hh])
            o = res[0:c] + a_uw[it][:, 0:dv]
            state[hh] = state[hh] * g_last[it] - res[c:c + dk] + k_uw[it][:, 0:dv]
            hs = slice(hh * dv, (hh + 1) * dv)
            o = _rms(o, nw) * _silu(zg_ref[sl, hs].astype(F32))
            o_ref[sl, hs] = o.astype(o_ref.dtype)
    for hh in range(hg):
        state_ref[hh] = state[hh]


def _deltanet(z3, zs3, dn_a_log, dn_dt_bias, conv_w_t, norm_w, tt=256):
    b, s, _ = z3.shape
    nh, hg = DN_HEADS, DN_HG
    ngrp = nh // hg
    wq = hg * DN_DK
    bq, bk, bv, bz = _OFF_DQ // wq, _OFF_DK // wq, _OFF_DV // wq, _OFF_DZ // wq
    db = zs3[..., _SM_DB:_SM_DB + nh].reshape(b, s, ngrp, hg)
    da = zs3[..., _SM_DA:_SM_DA + nh].reshape(b, s, ngrp, hg)
    grow = jnp.pad(da.transpose(0, 2, 3, 1), ((0, 0), (0, 0), (0, SUBLANE - hg), (0, 0)))
    gcol = jnp.pad(jnp.concatenate([db, da], axis=-1).transpose(0, 2, 1, 3),
                   ((0, 0), (0, 0), (0, 0), (0, LANE - 2 * hg)))
    al = dn_a_log.astype(F32).reshape(ngrp, hg)
    dtb = dn_dt_bias.astype(F32).reshape(ngrp, hg)
    alr = jnp.pad(al, ((0, 0), (0, SUBLANE - hg)))[:, :, None]
    dtr = jnp.pad(dtb, ((0, 0), (0, SUBLANE - hg)))[:, :, None]
    alc = jnp.pad(al, ((0, 0), (hg, LANE - 2 * hg)))[:, None, :]
    dtc = jnp.pad(dtb, ((0, 0), (hg, LANE - 2 * hg)))[:, None, :]
    return pl.pallas_call(
        _deltanet_kernel,
        out_shape=jax.ShapeDtypeStruct((b, s, nh * DN_DV), BF16),
        grid=(b, ngrp, s // tt),
        in_specs=[
            pl.BlockSpec((None, tt, wq), lambda bi, g, i: (bi, i, bq + g)),
            pl.BlockSpec((None, tt, wq), lambda bi, g, i: (bi, i, bk + g)),
            pl.BlockSpec((None, tt, wq), lambda bi, g, i: (bi, i, bv + g)),
            pl.BlockSpec((None, tt, wq), lambda bi, g, i: (bi, i, bz + g)),
            pl.BlockSpec((None, None, SUBLANE, tt), lambda bi, g, i: (bi, g, 0, i)),
            pl.BlockSpec((None, None, tt, LANE), lambda bi, g, i: (bi, g, i, 0)),
            pl.BlockSpec((None, SUBLANE, 1), lambda bi, g, i: (g, 0, 0)),
            pl.BlockSpec((None, SUBLANE, 1), lambda bi, g, i: (g, 0, 0)),
            pl.BlockSpec((None, 1, LANE), lambda bi, g, i: (g, 0, 0)),
            pl.BlockSpec((None, 1, LANE), lambda bi, g, i: (g, 0, 0)),
            pl.BlockSpec((DN_CONV, wq), lambda bi, g, i: (0, g)),
            pl.BlockSpec((DN_CONV, wq), lambda bi, g, i: (0, ngrp + g)),
            pl.BlockSpec((DN_CONV, wq), lambda bi, g, i: (0, 2 * ngrp + g)),
            pl.BlockSpec((1, DN_DV), lambda bi, g, i: (0, 0)),
        ],
        out_specs=pl.BlockSpec((None, tt, wq), lambda bi, g, i: (bi, i, g)),
        scratch_shapes=[
            pltpu.VMEM((hg, DN_DK, DN_DV), F32),
            pltpu.VMEM((3, SUBLANE, wq), F32),
            pltpu.VMEM((3, tt + SUBLANE, wq), F32),
        ],
        compiler_params=_cparams(("parallel", "parallel", "arbitrary")),
        name="deltanet",
    )(z3, z3, z3, z3, grow, gcol, alr, dtr, alc, dtc, conv_w_t, conv_w_t, conv_w_t, norm_w)


def _gla_kernel(q_ref, k_ref, v_ref, r_ref, zs_ref, wup_ref, bup_ref, nw_ref, o_ref, state_ref):
    tt = q_ref.shape[0]
    c = CHUNK
    sub = GLA_SUB

    @pl.when(pl.program_id(2) == 0)
    def _():
        state_ref[...] = jnp.zeros_like(state_ref)

    logit = _dot(zs_ref[...], wup_ref[...]) + bup_ref[...]
    log_a = -_softplus(-logit) * (1.0 / GLA_TAU)

    ri = lax.broadcasted_iota(jnp.int32, (c, c), 0)
    ci = lax.broadcasted_iota(jnp.int32, (c, c), 1)
    tril = jnp.where(ri >= ci, 1.0, 0.0).astype(BF16)
    sub_rows = lax.broadcasted_iota(jnp.int32, (sub, 1), 0)
    key_rows = lax.broadcasted_iota(jnp.int32, (c, 1), 0)
    blk_cols = lax.broadcasted_iota(jnp.int32, (sub, c), 1)

    nw = nw_ref[...]
    scale = GLA_DK ** -0.5
    nchunk = tt // c
    nblk = c // sub
    chunks = range(nchunk)
    bcum = [_dot_hilo_rhs(tril, log_a[n * c:(n + 1) * c]) for n in chunks]
    qc = [q_ref[n * c:(n + 1) * c, :].astype(F32) * scale for n in chunks]
    kc = [k_ref[n * c:(n + 1) * c, :].astype(F32) for n in chunks]
    vc = [v_ref[n * c:(n + 1) * c, :] for n in chunks]
    b_last = [bcum[n][c - 1:c, :] for n in chunks]
    q_dec = [qc[n] * jnp.exp(bcum[n]) for n in chunks]
    k_dec = [kc[n] * jnp.exp(b_last[n] - bcum[n]) for n in chunks]
    kv = [_dot_tn(vc[n], k_dec[n]) for n in chunks]

    off = {}
    for n in chunks:
        for bi in range(1, nblk):
            r0 = bi * sub
            b_ref = bcum[n][r0 - 1:r0, :]
            early = key_rows < r0
            q_f = qc[n][r0:r0 + sub] * jnp.exp(bcum[n][r0:r0 + sub] - b_ref)
            k_f = jnp.where(early, kc[n] * jnp.exp(jnp.where(early, b_ref - bcum[n], 0.0)), 0.0)
            off[n, bi] = _dot_nt(q_f, k_f)

    a_qk = []
    for n in chunks:
        row_blocks = []
        for bi in range(nblk):
            r0 = bi * sub
            q_i = qc[n][r0:r0 + sub]
            b_i = bcum[n][r0:r0 + sub]
            blk = off[n, bi] if bi > 0 else jnp.zeros((sub, c), F32)
            for j in range(sub):
                kj = kc[n][r0 + j:r0 + j + 1, :]
                bj = bcum[n][r0 + j:r0 + j + 1, :]
                live = sub_rows >= j
                dec = jnp.exp(jnp.where(live, b_i - bj, 0.0))
                col = jnp.sum(jnp.where(live, q_i * kj * dec, 0.0), axis=1, keepdims=True)
                blk = jnp.where(blk_cols == r0 + j, col, blk)
            row_blocks.append(blk)
        a_qk.append(jnp.concatenate(row_blocks, axis=0))

    intra = [_dot(a_qk[n], vc[n]) for n in chunks]
    state_t = state_ref[...]
    for n in chunks:
        sl = slice(n * c, (n + 1) * c)
        o = _dot_nt(q_dec[n], state_t) + intra[n]
        state_t = state_t * jnp.exp(b_last[n]) + kv[n]
        o = _rms(o, nw) * _silu(r_ref[sl, :].astype(F32))
        o_ref[sl, :] = o.astype(o_ref.dtype)
    state_ref[...] = state_t


def _gla(z3, zs3, w_up_pad, b_up, norm_w, tt=256):
    b, s, _ = z3.shape
    hq, hk = _OFF_GQ // GLA_DK, _OFF_GK // GLA_DK
    hv, hr = _OFF_GV // GLA_DV, _OFF_GR // GLA_DV
    ns = zs3.shape[-1]
    return pl.pallas_call(
        _gla_kernel,
        out_shape=jax.ShapeDtypeStruct((b, s, GLA_HEADS * GLA_DV), BF16),
        grid=(b, GLA_HEADS, s // tt),
        in_specs=[
            pl.BlockSpec((None, tt, GLA_DK), lambda bi, h, i: (bi, i, hq + h)),
            pl.BlockSpec((None, tt, GLA_DK), lambda bi, h, i: (bi, i, hk + h)),
            pl.BlockSpec((None, tt, GLA_DV), lambda bi, h, i: (bi, i, hv + h)),
            pl.BlockSpec((None, tt, GLA_DV), lambda bi, h, i: (bi, i, hr + h)),
            pl.BlockSpec((None, tt, ns), lambda bi, h, i: (bi, i, 0)),
            pl.BlockSpec((ns, GLA_DK), lambda bi, h, i: (0, h)),
            pl.BlockSpec((1, GLA_DK), lambda bi, h, i: (0, h)),
            pl.BlockSpec((1, GLA_DV), lambda bi, h, i: (0, 0)),
        ],
        out_specs=pl.BlockSpec((None, tt, GLA_DV), lambda bi, h, i: (bi, i, h)),
        scratch_shapes=[pltpu.VMEM((GLA_DV, GLA_DK), F32)],
        compiler_params=_cparams(("parallel", "parallel", "arbitrary")),
        name="gla",
    )(z3, z3, z3, z3, zs3, w_up_pad, b_up, norm_w)


def _s5_kernel(su_ref, kblk_ref, vre_ref, vim_ref, wre_ref, wim_ref, al_ref, d_ref,
               o_ref, m_ref, v_ref, w_ref, e_ref, xp_ref):
    ll = S5_L
    rows = e_ref.shape[0]
    nst = S5_LB * S5_STATE

    @pl.when(pl.program_id(1) == 0)
    def _():
        zero_blk = jnp.zeros((LANE, LANE), BF16)
        for s in range(ll):
            for i in range(ll):
                m_ref[s * LANE:(s + 1) * LANE, i * LANE:(i + 1) * LANE] = (
                    kblk_ref[i - s] if i >= s else zero_blk)
        rblk = 2 * LANE
        vr = lax.broadcasted_iota(jnp.int32, (rblk, nst), 0)
        vc = lax.broadcasted_iota(jnp.int32, (rblk, nst), 1)
        keep_v = ((vr & (LANE - 1)) >> 4) == (vc >> 6)
        for r in range(ll * LANE // rblk):
            rs = slice(r * rblk, (r + 1) * rblk)
            v_ref[rs, 0:nst] = jnp.where(keep_v, jnp.tile(vre_ref[rs, :], (1, nst // LANE)), 0.0).astype(BF16)
            v_ref[rs, nst:2 * nst] = jnp.where(keep_v, jnp.tile(vim_ref[rs, :], (1, nst // LANE)), 0.0).astype(BF16)
        wr = lax.broadcasted_iota(jnp.int32, (nst, rblk), 0)
        wc = lax.broadcasted_iota(jnp.int32, (nst, rblk), 1)
        keep_w = (wr >> 6) == ((wc & (LANE - 1)) >> 4)
        for r in range(ll * LANE // rblk):
            cs = slice(r * rblk, (r + 1) * rblk)
            w_ref[0:nst, cs] = jnp.where(keep_w, jnp.tile(wre_ref[:, cs], (S5_LB, 1)), 0.0).astype(BF16)
            w_ref[nst:2 * nst, cs] = jnp.where(keep_w, jnp.tile(wim_ref[:, cs], (S5_LB, 1)), 0.0).astype(BF16)

    u_list = [su_ref[pl.ds(s, rows, stride=ll), :] for s in range(ll)]
    u_cat = jnp.concatenate([u.astype(BF16) for u in u_list], axis=1)
    e_ref[...] = jnp.dot(u_cat, v_ref[...], preferred_element_type=F32)
    ar = al_ref[:, 0:nst]
    ai = al_ref[:, nst:2 * nst]

    def step(cidx, carry):
        xr, xi = carry
        row = pl.ds(cidx, 1)
        xp_ref[row, 0:nst] = xr
        xp_ref[row, nst:2 * nst] = xi
        er = e_ref[row, 0:nst]
        ei = e_ref[row, nst:2 * nst]
        return ar * xr - ai * xi + er, ar * xi + ai * xr + ei

    zero = jnp.zeros((1, nst), F32)
    lax.fori_loop(0, rows, step, (zero, zero), unroll=8)

    y = (jnp.dot(u_cat, m_ref[...], preferred_element_type=F32)
         + jnp.dot(xp_ref[...].astype(BF16), w_ref[...], preferred_element_type=F32))
    d = d_ref[...]
    for i in range(ll):
        yi = y[:, i * LANE:(i + 1) * LANE] + d * u_list[i]
        o_ref[pl.ds(i, rows, stride=ll), :] = _gelu_tanh(yi)


def _s5(su3, mats):
    kblk, vre, vim, wre, wim, al, dskip = mats
    nbatch, s, width = su3.shape
    ll = S5_L
    rows = s // ll
    nblk = width // LANE
    nst2 = 2 * S5_LB * S5_STATE

    return pl.pallas_call(
        _s5_kernel,
        out_shape=jax.ShapeDtypeStruct((nbatch, s, width), F32),
        grid=(nblk, nbatch),
        in_specs=[
            pl.BlockSpec((None, s, LANE), lambda j, b: (b, 0, j)),
            pl.BlockSpec((None, ll, LANE, LANE), lambda j, b: (j, 0, 0, 0)),
            pl.BlockSpec((None, ll * LANE, LANE), lambda j, b: (j, 0, 0)),
            pl.BlockSpec((None, ll * LANE, LANE), lambda j, b: (j, 0, 0)),
            pl.BlockSpec((None, S5_STATE, ll * LANE), lambda j, b: (j, 0, 0)),
            pl.BlockSpec((None, S5_STATE, ll * LANE), lambda j, b: (j, 0, 0)),
            pl.BlockSpec((None, 1, nst2), lambda j, b: (j, 0, 0)),
            pl.BlockSpec((None, 1, LANE), lambda j, b: (j, 0, 0)),
        ],
        out_specs=pl.BlockSpec((None, s, LANE), lambda j, b: (b, 0, j)),
        scratch_shapes=[
            pltpu.VMEM((ll * LANE, ll * LANE), BF16),
            pltpu.VMEM((ll * LANE, nst2), BF16),
            pltpu.VMEM((nst2, ll * LANE), BF16),
            pltpu.VMEM((rows, nst2), F32),
            pltpu.VMEM((rows, nst2), F32),
        ],
        compiler_params=_cparams(("parallel", "arbitrary")),
        name="s5",
    )(su3, kblk, vre, vim, wre, wim, al, dskip)


def _s5_matrices(lam_re, lam_im, log_step, b_re, b_im, c_re, c_im, d_skip):
    g, p, hh, ll, lb = S5_GROUPS, S5_STATE, S5_GROUP, S5_L, S5_LB
    nb = g // lb
    lr = jnp.minimum(lam_re.astype(F32), -1e-4)
    li = lam_im.astype(F32)
    dt = jnp.exp(log_step.astype(F32))[:, None]
    mag = jnp.exp(lr * dt)
    ar = mag * jnp.cos(li * dt)
    ai = mag * jnp.sin(li * dt)
    nr = ar - 1.0
    den = lr * lr + li * li
    zr = (nr * lr + ai * li) / den
    zi = (ai * lr - nr * li) / den
    br = b_re.astype(F32)
    bi = b_im.astype(F32)
    bbr = zr[..., None] * br - zi[..., None] * bi
    bbi = zr[..., None] * bi + zi[..., None] * br
    pr = [jnp.ones_like(ar)]
    pi = [jnp.zeros_like(ai)]
    for _ in range(ll):
        pr.append(pr[-1] * ar - pi[-1] * ai)
        pi.append(pr[-2] * ai + pi[-1] * ar)
    pr_rev = jnp.stack(pr[ll - 1::-1], axis=0)
    pi_rev = jnp.stack(pi[ll - 1::-1], axis=0)
    pr = jnp.stack(pr, axis=0)
    pi = jnp.stack(pi, axis=0)
    cr = c_re.astype(F32)
    cim = c_im.astype(F32)

    def blk(a, *shape):
        return a.reshape((nb, lb) + shape)

    pr_d = jnp.moveaxis(pr.reshape(ll + 1, nb, lb, p), 0, 1)[:, :, :, None, None, :]
    pi_d = jnp.moveaxis(pi.reshape(ll + 1, nb, lb, p), 0, 1)[:, :, :, None, None, :]
    cr_b = blk(cr, hh, p)[:, None, :, None, :, :]
    ci_b = blk(cim, hh, p)[:, None, :, None, :, :]
    qr = cr_b * pr_d - ci_b * pi_d
    qi = cr_b * pi_d + ci_b * pr_d
    bbr_b = blk(jnp.swapaxes(bbr, 1, 2), hh, p)[:, None, :, :, None, :]
    bbi_b = blk(jnp.swapaxes(bbi, 1, 2), hh, p)[:, None, :, :, None, :]
    kd = jnp.sum(qr[:, :ll] * bbr_b - qi[:, :ll] * bbi_b, axis=-1)
    eye = jnp.eye(lb, dtype=F32)[None, None, :, None, :, None]
    kblk = (kd[:, :, :, :, None, :] * eye).reshape(nb, ll, lb * hh, lb * hh)
    prs = jnp.moveaxis(pr_rev.reshape(ll, nb, lb, p), 0, 1)[:, :, :, None, :]
    pis = jnp.moveaxis(pi_rev.reshape(ll, nb, lb, p), 0, 1)[:, :, :, None, :]
    bbr_v, bbi_v = bbr_b[:, :, :, :, 0, :], bbi_b[:, :, :, :, 0, :]
    vre = jnp.tile((prs * bbr_v - pis * bbi_v).reshape(nb, ll * lb * hh, p), (1, 1, 2))
    vim = jnp.tile((prs * bbi_v + pis * bbr_v).reshape(nb, ll * lb * hh, p), (1, 1, 2))
    wre = jnp.moveaxis(qr[:, 1:ll + 1, :, 0], -1, 1).reshape(nb, p, ll * lb * hh)
    wim = -jnp.moveaxis(qi[:, 1:ll + 1, :, 0], -1, 1).reshape(nb, p, ll * lb * hh)

    al = jnp.concatenate([pr[ll].reshape(nb, 1, lb * p), pi[ll].reshape(nb, 1, lb * p)], axis=2)
    dskip = d_skip.astype(F32).reshape(nb, 1, lb * hh)
    return kblk.astype(BF16), vre, vim, wre, wim, al, dskip


def _mix_out_kernel(oa_ref, ob_ref, ys_ref, ga_ref, gb_ref, gc_ref, x_ref,
                    wa_ref, wb_ref, wca_ref, wcb_ref, wm_ref, gpost_ref, o_ref, acc_ref):
    n = pl.program_id(1)

    @pl.when(n == 0)
    def _():
        acc_ref[...] = jnp.zeros_like(acc_ref)

    oa, ob = oa_ref[...], ob_ref[...]
    ys = ys_ref[...].astype(BF16)
    tn = wa_ref.shape[1]
    halves = [slice(h * (tn // 2), (h + 1) * (tn // 2)) for h in range(2)]
    merged = []
    for hs in halves:
        y_a = jnp.dot(oa, wa_ref[:, hs], preferred_element_type=F32)
        y_b = jnp.dot(ob, wb_ref[:, hs], preferred_element_type=F32)
        glu_a = jnp.dot(ys, wca_ref[:, hs], preferred_element_type=F32)
        glu_b = jnp.dot(ys, wcb_ref[:, hs], preferred_element_type=F32)
        y_c = glu_a * _sigmoid(glu_b)
        merged.append((_sigmoid(ga_ref[:, hs].astype(F32)) * y_a + _sigmoid(gb_ref[:, hs].astype(F32)) * y_b
                       + _sigmoid(gc_ref[:, hs].astype(F32)) * y_c).astype(BF16))
    acc_ref[...] += (jnp.dot(merged[0], wm_ref[halves[0], :], preferred_element_type=F32)
                     + jnp.dot(merged[1], wm_ref[halves[1], :], preferred_element_type=F32))

    @pl.when(n == pl.num_programs(1) - 1)
    def _():
        o_ref[...] = x_ref[...] + _rms(acc_ref[...], gpost_ref[...])


def _mix_out(oa, ob, ys, z2, x2, w_dn_out, w_gla_out, w_s5_glu, w_mix_out, g_post, tm=512, tn=512):
    t, d = x2.shape
    nsteps = d // tn
    gblk = _OFF_GATES // tn
    ka, kb, kc = oa.shape[1], ob.shape[1], ys.shape[1]
    return pl.pallas_call(
        _mix_out_kernel,
        out_shape=jax.ShapeDtypeStruct((t, d), F32),
        grid=(t // tm, nsteps),
        in_specs=[
            pl.BlockSpec((tm, ka), lambda i, n: (i, 0)),
            pl.BlockSpec((tm, kb), lambda i, n: (i, 0)),
            pl.BlockSpec((tm, kc), lambda i, n: (i, 0)),
            pl.BlockSpec((tm, tn), lambda i, n: (i, gblk + n)),
            pl.BlockSpec((tm, tn), lambda i, n: (i, gblk + nsteps + n)),
            pl.BlockSpec((tm, tn), lambda i, n: (i, gblk + 2 * nsteps + n)),
            pl.BlockSpec((tm, d), lambda i, n: (i, 0)),
            pl.BlockSpec((ka, tn), lambda i, n: (0, n)),
            pl.BlockSpec((kb, tn), lambda i, n: (0, n)),
            pl.BlockSpec((kc, tn), lambda i, n: (0, n)),
            pl.BlockSpec((kc, tn), lambda i, n: (0, nsteps + n)),
            pl.BlockSpec((tn, d), lambda i, n: (n, 0)),
            pl.BlockSpec((1, d), lambda i, n: (0, 0)),
        ],
        out_specs=pl.BlockSpec((tm, d), lambda i, n: (i, 0)),
        scratch_shapes=[pltpu.VMEM((tm, d), F32)],
        compiler_params=_cparams(("parallel", "arbitrary")),
        name="mix_out",
    )(oa, ob, ys, z2, z2, z2, x2, w_dn_out, w_gla_out, w_s5_glu, w_s5_glu, w_mix_out, g_post)


def _xattn_kernel(x_ref, gpre_ref, wq_ref, kv_ref, wo_ref, gpost_ref, o_ref):
    x = x_ref[...]
    hn = _rms(x, gpre_ref[...]).astype(BF16)
    q = jnp.dot(hn, wq_ref[...], preferred_element_type=F32)
    kv = kv_ref[...]
    outs = []
    for h in range(XA_HEADS):
        qh = q[:, h * XA_DH:(h + 1) * XA_DH]
        kh = kv[:, h * XA_DH:(h + 1) * XA_DH]
        vh = kv[:, (XA_HEADS + h) * XA_DH:(XA_HEADS + h + 1) * XA_DH]
        s = _dot_nt(qh, kh) * (XA_DH ** -0.5)
        s = s - jnp.max(s, axis=-1, keepdims=True)
        p = jnp.exp(s)
        p = p / jnp.sum(p, axis=-1, keepdims=True)
        outs.append(_dot(p, vh))
    o = jnp.concatenate(outs, axis=1)
    y = jnp.dot(o.astype(BF16), wo_ref[...], preferred_element_type=F32)
    o_ref[...] = x + _rms(y, gpost_ref[...])


def _xattn(x3, g_pre, w_q, kv3, w_out, g_post, tm=512):
    b, s, d = x3.shape
    m, nkv = kv3.shape[1], kv3.shape[2]
    nq = w_q.shape[1]
    return pl.pallas_call(
        _xattn_kernel,
        out_shape=jax.ShapeDtypeStruct((b, s, d), F32),
        grid=(b, s // tm),
        in_specs=[
            pl.BlockSpec((None, tm, d), lambda bi, i: (bi, i, 0)),
            pl.BlockSpec((1, d), lambda bi, i: (0, 0)),
            pl.BlockSpec((d, nq), lambda bi, i: (0, 0)),
            pl.BlockSpec((None, m, nkv), lambda bi, i: (bi, 0, 0)),
            pl.BlockSpec((nq, d), lambda bi, i: (0, 0)),
            pl.BlockSpec((1, d), lambda bi, i: (0, 0)),
        ],
        out_specs=pl.BlockSpec((None, tm, d), lambda bi, i: (bi, i, 0)),
        compiler_params=_cparams(("parallel", "parallel")),
        name="xattn",
    )(x3, g_pre, w_q, kv3, w_out, g_post)


def _convffn_kernel(x_ref, xh_ref, gpre_ref, wg_ref, wu_ref, cw_ref, wd_ref, gpost_ref,
                    o_ref, hn_ref, hh_ref, gate_ref, acc_ref):
    i = pl.program_id(1)
    f = pl.program_id(2)
    tm = x_ref.shape[0]
    halo = SUBLANE

    @pl.when(f == 0)
    def _():
        hn_ref[...] = _rms(x_ref[...], gpre_ref[...]).astype(BF16)
        keep = jnp.where(i > 0, 1.0, 0.0)
        hh_ref[...] = (_rms(xh_ref[...], gpre_ref[...]) * keep).astype(BF16)
        acc_ref[...] = jnp.zeros_like(acc_ref)

    hn = hn_ref[...]
    wg = wg_ref[...]
    gate_ref[0:halo, :] = jnp.dot(hh_ref[...], wg, preferred_element_type=F32)
    gate_ref[halo:halo + tm, :] = jnp.dot(hn, wg, preferred_element_type=F32)
    up = jnp.dot(hn, wu_ref[...], preferred_element_type=F32)
    cw = cw_ref[...]
    base = halo - (FFN_CONV - 1)
    conv = gate_ref[base:base + tm, :] * cw[0:1, :]
    for j in range(1, FFN_CONV):
        conv = conv + gate_ref[base + j:base + j + tm, :] * cw[j:j + 1, :]
    act = (_gelu_tanh(conv) * up).astype(BF16)
    acc_ref[...] += jnp.dot(act, wd_ref[...], preferred_element_type=F32)

    @pl.when(f == pl.num_programs(2) - 1)
    def _():
        o_ref[...] = x_ref[...] + _rms(acc_ref[...], gpost_ref[...])


def _convffn(x3, g_pre, w_up, conv_w_t, w_down, g_post, tm=512, tf=512):
    b, s, d = x3.shape
    fh = w_down.shape[0]
    nf = fh // tf
    hb = tm // SUBLANE
    return pl.pallas_call(
        _convffn_kernel,
        out_shape=jax.ShapeDtypeStruct((b, s, d), F32),
        grid=(b, s // tm, nf),
        in_specs=[
            pl.BlockSpec((None, tm, d), lambda bi, i, f: (bi, i, 0)),
            pl.BlockSpec((None, SUBLANE, d), lambda bi, i, f: (bi, jnp.maximum(i * hb - 1, 0), 0)),
            pl.BlockSpec((1, d), lambda bi, i, f: (0, 0)),
            pl.BlockSpec((d, tf), lambda bi, i, f: (0, f)),
            pl.BlockSpec((d, tf), lambda bi, i, f: (0, nf + f)),
            pl.BlockSpec((FFN_CONV, tf), lambda bi, i, f: (0, f)),
            pl.BlockSpec((tf, d), lambda bi, i, f: (f, 0)),
            pl.BlockSpec((1, d), lambda bi, i, f: (0, 0)),
        ],
        out_specs=pl.BlockSpec((None, tm, d), lambda bi, i, f: (bi, i, 0)),
        scratch_shapes=[
            pltpu.VMEM((tm, d), BF16),
            pltpu.VMEM((SUBLANE, d), BF16),
            pltpu.VMEM((tm + SUBLANE, tf), F32),
            pltpu.VMEM((tm, d), F32),
        ],
        compiler_params=_cparams(("parallel", "parallel", "arbitrary")),
        name="convffn",
    )(x3, x3, g_pre, w_up, w_up, conv_w_t, w_down, g_post)


def _split_w_in(w_in_l):
    sizes = (DN_HEADS * DN_DK, DN_HEADS * DN_DK, DN_HEADS * DN_DV, DN_HEADS * DN_DV, DN_HEADS,
             DN_HEADS, GLA_HEADS * GLA_DK, GLA_HEADS * GLA_DK, GLA_HEADS * GLA_DV, GLA_RANK,
             GLA_HEADS * GLA_DV, S5_WIDTH, 3 * D_MODEL)
    parts = []
    acc = 0
    for n in sizes:
        parts.append(w_in_l[:, acc:acc + n])
        acc += n
    (dq, dk, dv, dz, db, da, gq, gk, gv, glow, gr, su, gates) = parts
    big = jnp.concatenate([dq, dk, dv, dz, gq, gk, gv, gr, su, gates], axis=1).astype(BF16)
    pad = jnp.zeros((w_in_l.shape[0], LANE - 2 * DN_HEADS - GLA_RANK), w_in_l.dtype)
    small = jnp.concatenate([db, da, glow, pad], axis=1).astype(BF16)
    return big, small


def kernel(x, mem, norm_mix_pre, norm_mix_post, w_in, dn_conv_w, dn_a_log, dn_dt_bias, dn_norm_w, w_dn_out, gla_w_up, gla_b_up, gla_norm_w, w_gla_out, s5_lam_re, s5_lam_im, s5_log_step, s5_b_re, s5_b_im, s5_c_re, s5_c_im, s5_d, w_s5_glu, w_mix_out, norm_xa_pre, norm_xa_post, norm_mem, w_xa_q, w_xa_kv, w_xa_out, norm_ffn_pre, norm_ffn_post, w_ffn_up, ffn_conv_w, w_ffn_down):
    b, s, d = x.shape
    t = b * s
    depth = w_in.shape[0]
    mem2 = mem.reshape(-1, d)
    row = lambda a: a.reshape(1, -1).astype(F32)

    for l in range(depth):
        w_big, w_small = _split_w_in(w_in[l])
        z2, zs2, su2 = _in_proj(x.reshape(t, d), row(norm_mix_pre[l]), w_big, w_small)
        z3 = z2.reshape(b, s, -1)
        zs3 = zs2.reshape(b, s, -1)

        oa = _deltanet(z3, zs3, dn_a_log[l], dn_dt_bias[l], dn_conv_w[l].T.astype(F32),
                       row(dn_norm_w[l]))

        w_up_pad = jnp.zeros((zs2.shape[1], GLA_HEADS * GLA_DK), F32)
        w_up_pad = w_up_pad.at[_SM_GLOW:_SM_GLOW + GLA_RANK].set(gla_w_up[l]).astype(BF16)
        ob = _gla(z3, zs3, w_up_pad, row(gla_b_up[l]), row(gla_norm_w[l]))

        mats = _s5_matrices(s5_lam_re[l], s5_lam_im[l], s5_log_step[l], s5_b_re[l], s5_b_im[l],
                            s5_c_re[l], s5_c_im[l], s5_d[l])
        ys = _s5(su2.reshape(b, s, -1), mats)

        x = _mix_out(oa.reshape(t, -1), ob.reshape(t, -1), ys.reshape(t, -1), z2, x.reshape(t, d),
                     w_dn_out[l].astype(BF16), w_gla_out[l].astype(BF16), w_s5_glu[l].astype(BF16),
                     w_mix_out[l].astype(BF16), row(norm_mix_post[l])).reshape(b, s, d)

        kv = _norm_proj(mem2, row(norm_mem[l]), w_xa_kv[l].astype(BF16))
        x = _xattn(x, row(norm_xa_pre[l]), w_xa_q[l].astype(BF16), kv.reshape(b, -1, kv.shape[-1]),
                   w_xa_out[l].astype(BF16), row(norm_xa_post[l]))

        x = _convffn(x, row(norm_ffn_pre[l]), w_ffn_up[l].astype(BF16), ffn_conv_w[l].T.astype(F32),
                     w_ffn_down[l].astype(BF16), row(norm_ffn_post[l]))
    return x
```

```python
import jax
import jax.numpy as jnp
from jax import lax
from jax.experimental import pallas as pl
from jax.experimental.pallas import tpu as pltpu

F32 = jnp.float32
BF16 = jnp.bfloat16

NORM_EPS = 1e-6

D_MODEL = 2048
DN_HEADS, DN_DK, DN_DV, DN_CONV = 8, 128, 128, 4
GLA_HEADS, GLA_DK, GLA_DV, GLA_RANK, GLA_TAU = 4, 128, 256, 16, 16.0
S5_GROUP, S5_GROUPS, S5_STATE = 16, 64, 64
S5_WIDTH = S5_GROUP * S5_GROUPS
XA_HEADS, XA_DH = 4, 128
FFN_HIDDEN, FFN_CONV = 5632, 3

CHUNK = 64
GLA_SUB = 16
S5_L = 16
DN_HG = 4
DN_BASE = 8
LANE = 128
S5_LB = LANE // S5_GROUP
SUBLANE = 8
VMEM_LIMIT = 56 * 1024 * 1024

_OFF_DQ, _OFF_DK, _OFF_DV, _OFF_DZ = 0, 1024, 2048, 3072
_OFF_GQ, _OFF_GK, _OFF_GV, _OFF_GR = 4096, 4608, 5120, 6144
_OFF_SU, _OFF_GATES = 7168, 8192
_BIG_WIDTH = 14336
_SM_GLOW = 16


def _cparams(sem):
    return pltpu.CompilerParams(dimension_semantics=sem, vmem_limit_bytes=VMEM_LIMIT)


def _rms(xf, g):
    return xf * lax.rsqrt(jnp.mean(xf * xf, axis=-1, keepdims=True) + NORM_EPS) * g


def _dot(a, b):
    return jnp.dot(a.astype(BF16), b.astype(BF16), preferred_element_type=F32)


def _dot_nt(a, b):
    return lax.dot_general(a.astype(BF16), b.astype(BF16), (((1,), (1,)), ((), ())),
                           preferred_element_type=F32)


def _dot_tn(a, b):
    return lax.dot_general(a.astype(BF16), b.astype(BF16), (((0,), (0,)), ((), ())),
                           preferred_element_type=F32)


def _sigmoid(x):
    return 0.5 * jnp.tanh(0.5 * x) + 0.5


def _silu(x):
    return x * _sigmoid(x)


def _softplus(x):
    return jnp.maximum(x, 0.0) + jnp.log(1.0 + jnp.exp(-jnp.abs(x)))


def _gelu_tanh(x):
    c = 0.7978845608028654
    return 0.5 * x * (1.0 + jnp.tanh(c * (x + 0.044715 * (x * x * x))))


def _in_proj_kernel(x_ref, g_ref, w_ref, ws_ref, o_ref, os_ref, su_ref, hn_ref):
    j = pl.program_id(1)
    tn = o_ref.shape[1]

    @pl.when(j == 0)
    def _():
        hn = _rms(x_ref[...], g_ref[...]).astype(BF16)
        hn_ref[...] = hn
        os_ref[...] = jnp.dot(hn, ws_ref[...], preferred_element_type=F32)

    acc = jnp.dot(hn_ref[...], w_ref[...], preferred_element_type=F32)
    o_ref[...] = acc.astype(o_ref.dtype)

    @pl.when(jnp.logical_and(j >= _OFF_SU // tn, j < (_OFF_SU + S5_WIDTH) // tn))
    def _():
        su_ref[...] = acc


def _in_proj(x2, g, w_big, w_small, tm=1024, tn=1024):
    t, d = x2.shape
    n = w_big.shape[1]
    ns = w_small.shape[1]
    j0, nsu = _OFF_SU // tn, S5_WIDTH // tn
    return pl.pallas_call(
        _in_proj_kernel,
        out_shape=(jax.ShapeDtypeStruct((t, n), BF16), jax.ShapeDtypeStruct((t, ns), F32),
                   jax.ShapeDtypeStruct((t, S5_WIDTH), F32)),
        grid=(t // tm, n // tn),
        in_specs=[
            pl.BlockSpec((tm, d), lambda i, j: (i, 0)),
            pl.BlockSpec((1, d), lambda i, j: (0, 0)),
            pl.BlockSpec((d, tn), lambda i, j: (0, j)),
            pl.BlockSpec((d, ns), lambda i, j: (0, 0)),
        ],
        out_specs=(
            pl.BlockSpec((tm, tn), lambda i, j: (i, j)),
            pl.BlockSpec((tm, ns), lambda i, j: (i, 0)),
            pl.BlockSpec((tm, tn), lambda i, j: (i, jnp.clip(j - j0, 0, nsu - 1))),
        ),
        scratch_shapes=[pltpu.VMEM((tm, d), BF16)],
        compiler_params=_cparams(("parallel", "arbitrary")),
        name="in_proj",
    )(x2, g, w_big, w_small)


def _norm_proj_kernel(x_ref, g_ref, w_ref, o_ref):
    hn = _rms(x_ref[...], g_ref[...]).astype(BF16)
    o_ref[...] = jnp.dot(hn, w_ref[...], preferred_element_type=F32).astype(o_ref.dtype)


def _norm_proj(x2, g, w, tm=512):
    t, d = x2.shape
    n = w.shape[1]
    return pl.pallas_call(
        _norm_proj_kernel,
        out_shape=jax.ShapeDtypeStruct((t, n), BF16),
        grid=(t // tm,),
        in_specs=[
            pl.BlockSpec((tm, d), lambda i: (i, 0)),
            pl.BlockSpec((1, d), lambda i: (0, 0)),
            pl.BlockSpec((d, n), lambda i: (0, 0)),
        ],
        out_specs=pl.BlockSpec((tm, n), lambda i: (i, 0)),
        compiler_params=_cparams(("parallel",)),
        name="mem_kv_proj",
    )(x2, g, w)


def _dot_hilo_rhs(a01, b):
    hi = b.astype(BF16)
    lo = (b - hi.astype(F32)).astype(BF16)
    return (jnp.dot(a01, hi, preferred_element_type=F32) + jnp.dot(a01, lo, preferred_element_type=F32))


def _dot3(a, b):
    ah = a.astype(BF16)
    al = (a - ah.astype(F32)).astype(BF16)
    bh = b.astype(BF16)
    bl = (b - bh.astype(F32)).astype(BF16)
    return (jnp.dot(ah, bh, preferred_element_type=F32)
            + (jnp.dot(ah, bl, preferred_element_type=F32) + jnp.dot(al, bh, preferred_element_type=F32)))


def _dot_hilo_lhs(a, b01):
    hi = a.astype(BF16)
    lo = (a - hi.astype(F32)).astype(BF16)
    return (jnp.dot(hi, b01, preferred_element_type=F32) + jnp.dot(lo, b01, preferred_element_type=F32))


def _deltanet_kernel(q_ref, k_ref, v_ref, zg_ref, zs_ref, alr_ref, dtr_ref, alc_ref,
                     dtc_ref, cwq_ref, cwk_ref, cwv_ref, nw_ref, o_ref, state_ref, tail_ref, ext_ref):
    tt = q_ref.shape[0]
    c = CHUNK
    hg = DN_HG
    dk, dv = DN_DK, DN_DV
    halo = SUBLANE
    nchunk = tt // c

    @pl.when(pl.program_id(2) == 0)
    def _():
        state_ref[...] = jnp.zeros_like(state_ref)
        tail_ref[...] = jnp.zeros_like(tail_ref)

    def conv_silu(idx, raw_ref, cw_ref):
        raw = raw_ref[...].astype(F32)
        ext_ref[idx, 0:halo, :] = tail_ref[idx]
        ext_ref[idx, halo:halo + tt, :] = raw
        tail_ref[idx] = raw[tt - halo:tt, :]
        cw = cw_ref[...]
        base = halo - (DN_CONV - 1)
        y = ext_ref[idx, base:base + tt, :] * cw[0:1, :]
        for j in range(1, DN_CONV):
            y = y + ext_ref[idx, base + j:base + j + tt, :] * cw[j:j + 1, :]
        return _silu(y)

    q_all = conv_silu(0, q_ref, cwq_ref)
    k_all = conv_silu(1, k_ref, cwk_ref)
    v_all = conv_silu(2, v_ref, cwv_ref)

    grp = pl.program_id(1)
    gcol = pltpu.roll(zs_ref[...], (LANE - 2 * hg * grp) % LANE, 1)
    grow = gcol.T[0:SUBLANE, :]
    g_rows = -jnp.exp(alr_ref[...]) * _softplus(grow + dtr_ref[...])
    beta_cols = _sigmoid(gcol)
    g_cols = -jnp.exp(alc_ref[...]) * _softplus(gcol + dtc_ref[...])

    ri = lax.broadcasted_iota(jnp.int32, (c, c), 0)
    ci = lax.broadcasted_iota(jnp.int32, (c, c), 1)
    causal = ri >= ci
    strict = ri > ci
    ident = jnp.where(ri == ci, 1.0, 0.0).astype(F32)
    tril01 = jnp.where(causal, 1.0, 0.0).astype(BF16)
    triu01 = jnp.where(ri <= ci, 1.0, 0.0).astype(BF16)

    items = [(hh, n) for n in range(nchunk) for hh in range(hg)]
    qs, lower, a_qk, e_col, k_dec, g_last, vb = {}, {}, {}, {}, {}, {}, {}
    for n in range(nchunk):
        sl = slice(n * c, (n + 1) * c)
        gcs_c = _dot_hilo_rhs(tril01, g_cols[sl])
        gcs_r = _dot_hilo_lhs(g_rows[:, sl], triu01)
        for hh in range(hg):
            it = (hh, n)
            hs = slice(hh * dk, (hh + 1) * dk)
            qc, kc = q_all[sl, hs], k_all[sl, hs]
            qc = qc * (lax.rsqrt(jnp.sum(qc * qc, axis=-1, keepdims=True) + NORM_EPS) * (dk ** -0.5))
            kc = kc * lax.rsqrt(jnp.sum(kc * kc, axis=-1, keepdims=True) + NORM_EPS)
            beta_col = beta_cols[sl, hh:hh + 1]
            gc_col = gcs_c[:, hg + hh:hg + hh + 1]
            gc_row = gcs_r[hg + hh:hg + hh + 1, :]
            gc_end = gcs_r[hg + hh:hg + hh + 1, c - 1:c]
            decay = jnp.where(causal, jnp.exp(jnp.where(causal, gc_col - gc_row, 0.0)), 0.0)
            kb = kc * beta_col
            res = _dot_nt(jnp.concatenate([kb, qc], axis=0), kc)
            lower[it] = jnp.where(strict, res[0:c] * decay, 0.0)
            a_qk[it] = jnp.where(causal, res[c:2 * c] * decay, 0.0)
            e_col[it] = jnp.exp(gc_col)
            k_dec[it] = kc * jnp.exp(gc_end - gc_col)
            g_last[it] = jnp.exp(gc_end)
            vb[it] = jnp.concatenate([v_all[sl, hh * dv:(hh + 1) * dv] * beta_col, kb * e_col[it]], axis=1)
            qs[it] = qc

    base_shift = DN_BASE.bit_length() - 1
    same_base = (ri >> base_shift) == (ci >> base_shift)
    inv, pw = {}, {}
    for it in items:
        lb = jnp.where(same_base, lower[it], 0.0)
        inv[it] = ident - lb
        pw[it] = _dot3(lb, lb)
    span = 2
    while span < DN_BASE:
        for it in items:
            if span * 2 < DN_BASE:
                res = _dot3(jnp.concatenate([inv[it], pw[it]], axis=0), pw[it])
                inv[it] = inv[it] + res[0:c]
                pw[it] = res[c:2 * c]
            else:
                inv[it] = inv[it] + _dot3(inv[it], pw[it])
        span *= 2
    width = 2 * DN_BASE
    while width <= c:
        sh = width.bit_length() - 1
        same_w = (ri >> sh) == (ci >> sh)
        same_half = (ri >> (sh - 1)) == (ci >> (sh - 1))
        off = {it: jnp.where(same_w, jnp.where(same_half, 0.0, lower[it]), 0.0) for it in items}
        mid = {it: _dot3(off[it], inv[it]) for it in items}
        for it in items:
            inv[it] = inv[it] - _dot3(inv[it], mid[it])
        width *= 2

    uw = {it: _dot3(inv[it], vb[it]) for it in items}
    a_uw = {it: _dot(a_qk[it], uw[it]) for it in items}
    k_uw = {it: _dot_tn(k_dec[it], uw[it]) for it in items}

    nw = nw_ref[...]
    state = [state_ref[hh] for hh in range(hg)]
    for n in range(nchunk):
        sl = slice(n * c, (n + 1) * c)
        for hh in range(hg):
            it = (hh, n)
            q_eff = qs[it] * e_col[it] - a_uw[it][:, dv:dv + dk]
            res = _dot(jnp.concatenate([q_eff, k_uw[it][:, dv:dv + dk]], axis=0), state[hh])
            o = res[0:c] + a_uw[it][:, 0:dv]
            state[hh] = state[hh] * g_last[it] - res[c:c + dk] + k_uw[it][:, 0:dv]
            hs = slice(hh * dv, (hh + 1) * dv)
            o = _rms(o, nw) * _silu(zg_ref[sl, hs].astype(F32))
            o_ref[sl, hs] = o.astype(o_ref.dtype)
    for hh in range(hg):
        state_ref[hh] = state[hh]


def _deltanet(z3, zs3, dn_a_log, dn_dt_bias, conv_w_t, norm_w, tt=256):
    b, s, _ = z3.shape
    nh, hg = DN_HEADS, DN_HG
    ngrp = nh // hg
    wq = hg * DN_DK
    bq, bk, bv, bz = _OFF_DQ // wq, _OFF_DK // wq, _OFF_DV // wq, _OFF_DZ // wq
    al = dn_a_log.astype(F32).reshape(ngrp, hg)
    dtb = dn_dt_bias.astype(F32).reshape(ngrp, hg)
    alr = jnp.pad(al, ((0, 0), (hg, SUBLANE - 2 * hg)))[:, :, None]
    dtr = jnp.pad(dtb, ((0, 0), (hg, SUBLANE - 2 * hg)))[:, :, None]
    alc = jnp.pad(al, ((0, 0), (hg, LANE - 2 * hg)))[:, None, :]
    dtc = jnp.pad(dtb, ((0, 0), (hg, LANE - 2 * hg)))[:, None, :]
    return pl.pallas_call(
        _deltanet_kernel,
        out_shape=jax.ShapeDtypeStruct((b, s, nh * DN_DV), BF16),
        grid=(b, ngrp, s // tt),
        in_specs=[
            pl.BlockSpec((None, tt, wq), lambda bi, g, i: (bi, i, bq + g)),
            pl.BlockSpec((None, tt, wq), lambda bi, g, i: (bi, i, bk + g)),
            pl.BlockSpec((None, tt, wq), lambda bi, g, i: (bi, i, bv + g)),
            pl.BlockSpec((None, tt, wq), lambda bi, g, i: (bi, i, bz + g)),
            pl.BlockSpec((None, tt, LANE), lambda bi, g, i: (bi, i, 0)),
            pl.BlockSpec((None, SUBLANE, 1), lambda bi, g, i: (g, 0, 0)),
            pl.BlockSpec((None, SUBLANE, 1), lambda bi, g, i: (g, 0, 0)),
            pl.BlockSpec((None, 1, LANE), lambda bi, g, i: (g, 0, 0)),
            pl.BlockSpec((None, 1, LANE), lambda bi, g, i: (g, 0, 0)),
            pl.BlockSpec((DN_CONV, wq), lambda bi, g, i: (0, g)),
            pl.BlockSpec((DN_CONV, wq), lambda bi, g, i: (0, ngrp + g)),
            pl.BlockSpec((DN_CONV, wq), lambda bi, g, i: (0, 2 * ngrp + g)),
            pl.BlockSpec((1, DN_DV), lambda bi, g, i: (0, 0)),
        ],
        out_specs=pl.BlockSpec((None, tt, wq), lambda bi, g, i: (bi, i, g)),
        scratch_shapes=[
            pltpu.VMEM((hg, DN_DK, DN_DV), F32),
            pltpu.VMEM((3, SUBLANE, wq), F32),
            pltpu.VMEM((3, tt + SUBLANE, wq), F32),
        ],
        compiler_params=_cparams(("parallel", "parallel", "arbitrary")),
        name="deltanet",
    )(z3, z3, z3, z3, zs3, alr, dtr, alc, dtc, conv_w_t, conv_w_t, conv_w_t, norm_w)


def _gla_kernel(q_ref, k_ref, v_ref, r_ref, zs_ref, wup_ref, bup_ref, nw_ref, o_ref, state_ref):
    tt = q_ref.shape[0]
    c = CHUNK
    sub = GLA_SUB

    @pl.when(pl.program_id(2) == 0)
    def _():
        state_ref[...] = jnp.zeros_like(state_ref)

    logit = _dot(zs_ref[...], wup_ref[...]) + bup_ref[...]
    log_a = -_softplus(-logit) * (1.0 / GLA_TAU)

    ri = lax.broadcasted_iota(jnp.int32, (c, c), 0)
    ci = lax.broadcasted_iota(jnp.int32, (c, c), 1)
    tril = jnp.where(ri >= ci, 1.0, 0.0).astype(BF16)
    sub_rows = lax.broadcasted_iota(jnp.int32, (sub, 1), 0)
    key_rows = lax.broadcasted_iota(jnp.int32, (c, 1), 0)
    blk_cols = lax.broadcasted_iota(jnp.int32, (sub, c), 1)

    nw = nw_ref[...]
    scale = GLA_DK ** -0.5
    nchunk = tt // c
    nblk = c // sub
    chunks = range(nchunk)
    bcum = [_dot_hilo_rhs(tril, log_a[n * c:(n + 1) * c]) for n in chunks]
    qc = [q_ref[n * c:(n + 1) * c, :].astype(F32) * scale for n in chunks]
    kc = [k_ref[n * c:(n + 1) * c, :].astype(F32) for n in chunks]
    vc = [v_ref[n * c:(n + 1) * c, :] for n in chunks]
    b_last = [bcum[n][c - 1:c, :] for n in chunks]
    q_dec = [qc[n] * jnp.exp(bcum[n]) for n in chunks]
    k_dec = [kc[n] * jnp.exp(b_last[n] - bcum[n]) for n in chunks]
    kv = [_dot_tn(vc[n], k_dec[n]) for n in chunks]

    off = {}
    for n in chunks:
        for bi in range(1, nblk):
            r0 = bi * sub
            b_ref = bcum[n][r0 - 1:r0, :]
            early = key_rows < r0
            q_f = qc[n][r0:r0 + sub] * jnp.exp(bcum[n][r0:r0 + sub] - b_ref)
            k_f = jnp.where(early, kc[n] * jnp.exp(jnp.where(early, b_ref - bcum[n], 0.0)), 0.0)
            off[n, bi] = _dot_nt(q_f, k_f)

    a_qk = []
    for n in chunks:
        row_blocks = []
        for bi in range(nblk):
            r0 = bi * sub
            q_i = qc[n][r0:r0 + sub]
            b_i = bcum[n][r0:r0 + sub]
            blk = off[n, bi] if bi > 0 else jnp.zeros((sub, c), F32)
            for j in range(sub):
                kj = kc[n][r0 + j:r0 + j + 1, :]
                bj = bcum[n][r0 + j:r0 + j + 1, :]
                live = sub_rows >= j
                dec = jnp.exp(jnp.where(live, b_i - bj, 0.0))
                col = jnp.sum(jnp.where(live, q_i * kj * dec, 0.0), axis=1, keepdims=True)
                blk = jnp.where(blk_cols == r0 + j, col, blk)
            row_blocks.append(blk)
        a_qk.append(jnp.concatenate(row_blocks, axis=0))

    intra = [_dot(a_qk[n], vc[n]) for n in chunks]
    state_t = state_ref[...]
    for n in chunks:
        sl = slice(n * c, (n + 1) * c)
        o = _dot_nt(q_dec[n], state_t) + intra[n]
        state_t = state_t * jnp.exp(b_last[n]) + kv[n]
        o = _rms(o, nw) * _silu(r_ref[sl, :].astype(F32))
        o_ref[sl, :] = o.astype(o_ref.dtype)
    state_ref[...] = state_t


def _gla(z3, zs3, w_up_pad, b_up, norm_w, tt=256):
    b, s, _ = z3.shape
    hq, hk = _OFF_GQ // GLA_DK, _OFF_GK // GLA_DK
    hv, hr = _OFF_GV // GLA_DV, _OFF_GR // GLA_DV
    ns = zs3.shape[-1]
    return pl.pallas_call(
        _gla_kernel,
        out_shape=jax.ShapeDtypeStruct((b, s, GLA_HEADS * GLA_DV), BF16),
        grid=(b, GLA_HEADS, s // tt),
        in_specs=[
            pl.BlockSpec((None, tt, GLA_DK), lambda bi, h, i: (bi, i, hq + h)),
            pl.BlockSpec((None, tt, GLA_DK), lambda bi, h, i: (bi, i, hk + h)),
            pl.BlockSpec((None, tt, GLA_DV), lambda bi, h, i: (bi, i, hv + h)),
            pl.BlockSpec((None, tt, GLA_DV), lambda bi, h, i: (bi, i, hr + h)),
            pl.BlockSpec((None, tt, ns), lambda bi, h, i: (bi, i, 0)),
            pl.BlockSpec((ns, GLA_DK), lambda bi, h, i: (0, h)),
            pl.BlockSpec((1, GLA_DK), lambda bi, h, i: (0, h)),
            pl.BlockSpec((1, GLA_DV), lambda bi, h, i: (0, 0)),
        ],
        out_specs=pl.BlockSpec((None, tt, GLA_DV), lambda bi, h, i: (bi, i, h)),
        scratch_shapes=[pltpu.VMEM((GLA_DV, GLA_DK), F32)],
        compiler_params=_cparams(("parallel", "parallel", "arbitrary")),
        name="gla",
    )(z3, z3, z3, z3, zs3, w_up_pad, b_up, norm_w)


def _s5_kernel(su_ref, kblk_ref, vre_ref, vim_ref, wre_ref, wim_ref, al_ref, d_ref,
               o_ref, m_ref, v_ref, w_ref, e_ref, xp_ref):
    ll = S5_L
    rows = e_ref.shape[0]
    nst = S5_LB * S5_STATE

    @pl.when(pl.program_id(1) == 0)
    def _():
        zero_blk = jnp.zeros((LANE, LANE), BF16)
        for s in range(ll):
            for i in range(ll):
                m_ref[s * LANE:(s + 1) * LANE, i * LANE:(i + 1) * LANE] = (
                    kblk_ref[i - s] if i >= s else zero_blk)
        rblk = 2 * LANE
        vr = lax.broadcasted_iota(jnp.int32, (rblk, nst), 0)
        vc = lax.broadcasted_iota(jnp.int32, (rblk, nst), 1)
        keep_v = ((vr & (LANE - 1)) >> 4) == (vc >> 6)
        for r in range(ll * LANE // rblk):
            rs = slice(r * rblk, (r + 1) * rblk)
            v_ref[rs, 0:nst] = jnp.where(keep_v, jnp.tile(vre_ref[rs, :], (1, nst // LANE)), 0.0).astype(BF16)
            v_ref[rs, nst:2 * nst] = jnp.where(keep_v, jnp.tile(vim_ref[rs, :], (1, nst // LANE)), 0.0).astype(BF16)
        wr = lax.broadcasted_iota(jnp.int32, (nst, rblk), 0)
        wc = lax.broadcasted_iota(jnp.int32, (nst, rblk), 1)
        keep_w = (wr >> 6) == ((wc & (LANE - 1)) >> 4)
        for r in range(ll * LANE // rblk):
            cs = slice(r * rblk, (r + 1) * rblk)
            w_ref[0:nst, cs] = jnp.where(keep_w, jnp.tile(wre_ref[:, cs], (S5_LB, 1)), 0.0).astype(BF16)
            w_ref[nst:2 * nst, cs] = jnp.where(keep_w, jnp.tile(wim_ref[:, cs], (S5_LB, 1)), 0.0).astype(BF16)

    u_list = [su_ref[pl.ds(s, rows, stride=ll), :] for s in range(ll)]
    u_cat = jnp.concatenate([u.astype(BF16) for u in u_list], axis=1)
    e_ref[...] = jnp.dot(u_cat, v_ref[...], preferred_element_type=F32)
    ar = al_ref[:, 0:nst]
    ai = al_ref[:, nst:2 * nst]

    def step(cidx, carry):
        xr, xi = carry
        row = pl.ds(cidx, 1)
        xp_ref[row, 0:nst] = xr
        xp_ref[row, nst:2 * nst] = xi
        er = e_ref[row, 0:nst]
        ei = e_ref[row, nst:2 * nst]
        return ar * xr - ai * xi + er, ar * xi + ai * xr + ei

    zero = jnp.zeros((1, nst), F32)
    lax.fori_loop(0, rows, step, (zero, zero), unroll=8)

    y = (jnp.dot(u_cat, m_ref[...], preferred_element_type=F32)
         + jnp.dot(xp_ref[...].astype(BF16), w_ref[...], preferred_element_type=F32))
    d = d_ref[...]
    for i in range(ll):
        yi = y[:, i * LANE:(i + 1) * LANE] + d * u_list[i]
        o_ref[pl.ds(i, rows, stride=ll), :] = _gelu_tanh(yi)


def _s5(su2, mats, layer, nbatch):
    kblk, vre, vim, wre, wim, al, dskip = mats
    t, width = su2.shape
    s = t // nbatch
    ll = S5_L
    rows = s // ll
    nblk = width // LANE
    nst2 = 2 * S5_LB * S5_STATE

    return pl.pallas_call(
        _s5_kernel,
        out_shape=jax.ShapeDtypeStruct((t, width), F32),
        grid=(nblk, nbatch),
        in_specs=[
            pl.BlockSpec((s, LANE), lambda j, b: (b, j)),
            pl.BlockSpec((None, None, ll, LANE, LANE), lambda j, b: (layer, j, 0, 0, 0)),
            pl.BlockSpec((None, None, ll * LANE, LANE), lambda j, b: (layer, j, 0, 0)),
            pl.BlockSpec((None, None, ll * LANE, LANE), lambda j, b: (layer, j, 0, 0)),
            pl.BlockSpec((None, None, S5_STATE, ll * LANE), lambda j, b: (layer, j, 0, 0)),
            pl.BlockSpec((None, None, S5_STATE, ll * LANE), lambda j, b: (layer, j, 0, 0)),
            pl.BlockSpec((None, None, 1, nst2), lambda j, b: (layer, j, 0, 0)),
            pl.BlockSpec((None, None, 1, LANE), lambda j, b: (layer, j, 0, 0)),
        ],
        out_specs=pl.BlockSpec((s, LANE), lambda j, b: (b, j)),
        scratch_shapes=[
            pltpu.VMEM((ll * LANE, ll * LANE), BF16),
            pltpu.VMEM((ll * LANE, nst2), BF16),
            pltpu.VMEM((nst2, ll * LANE), BF16),
            pltpu.VMEM((rows, nst2), F32),
            pltpu.VMEM((rows, nst2), F32),
        ],
        compiler_params=_cparams(("parallel", "arbitrary")),
        name="s5",
    )(su2, kblk, vre, vim, wre, wim, al, dskip)


def _s5_matrices(lam_re, lam_im, log_step, b_re, b_im, c_re, c_im, d_skip):
    g, p, hh, ll, lb = S5_GROUPS, S5_STATE, S5_GROUP, S5_L, S5_LB
    nb = g // lb
    lr = jnp.minimum(lam_re.astype(F32), -1e-4)
    li = lam_im.astype(F32)
    dt = jnp.exp(log_step.astype(F32))[:, None]
    mag = jnp.exp(lr * dt)
    ar = mag * jnp.cos(li * dt)
    ai = mag * jnp.sin(li * dt)
    nr = ar - 1.0
    den = lr * lr + li * li
    zr = (nr * lr + ai * li) / den
    zi = (ai * lr - nr * li) / den
    br = b_re.astype(F32)
    bi = b_im.astype(F32)
    bbr = zr[..., None] * br - zi[..., None] * bi
    bbi = zr[..., None] * bi + zi[..., None] * br
    pr = [jnp.ones_like(ar)]
    pi = [jnp.zeros_like(ai)]
    for _ in range(ll):
        pr.append(pr[-1] * ar - pi[-1] * ai)
        pi.append(pr[-2] * ai + pi[-1] * ar)
    pr_rev = jnp.stack(pr[ll - 1::-1], axis=0)
    pi_rev = jnp.stack(pi[ll - 1::-1], axis=0)
    pr = jnp.stack(pr, axis=0)
    pi = jnp.stack(pi, axis=0)
    cr = c_re.astype(F32)
    cim = c_im.astype(F32)

    def blk(a, *shape):
        return a.reshape((nb, lb) + shape)

    pr_d = jnp.moveaxis(pr.reshape(ll + 1, nb, lb, p), 0, 1)[:, :, :, None, None, :]
    pi_d = jnp.moveaxis(pi.reshape(ll + 1, nb, lb, p), 0, 1)[:, :, :, None, None, :]
    cr_b = blk(cr, hh, p)[:, None, :, None, :, :]
    ci_b = blk(cim, hh, p)[:, None, :, None, :, :]
    qr = cr_b * pr_d - ci_b * pi_d
    qi = cr_b * pi_d + ci_b * pr_d
    bbr_b = blk(jnp.swapaxes(bbr, 1, 2), hh, p)[:, None, :, :, None, :]
    bbi_b = blk(jnp.swapaxes(bbi, 1, 2), hh, p)[:, None, :, :, None, :]
    kd = jnp.sum(qr[:, :ll] * bbr_b - qi[:, :ll] * bbi_b, axis=-1)
    eye = jnp.eye(lb, dtype=F32)[None, None, :, None, :, None]
    kblk = (kd[:, :, :, :, None, :] * eye).reshape(nb, ll, lb * hh, lb * hh)
    prs = jnp.moveaxis(pr_rev.reshape(ll, nb, lb, p), 0, 1)[:, :, :, None, :]
    pis = jnp.moveaxis(pi_rev.reshape(ll, nb, lb, p), 0, 1)[:, :, :, None, :]
    bbr_v, bbi_v = bbr_b[:, :, :, :, 0, :], bbi_b[:, :, :, :, 0, :]
    vre = jnp.tile((prs * bbr_v - pis * bbi_v).reshape(nb, ll * lb * hh, p), (1, 1, 2))
    vim = jnp.tile((prs * bbi_v + pis * bbr_v).reshape(nb, ll * lb * hh, p), (1, 1, 2))
    wre = jnp.moveaxis(qr[:, 1:ll + 1, :, 0], -1, 1).reshape(nb, p, ll * lb * hh)
    wim = -jnp.moveaxis(qi[:, 1:ll + 1, :, 0], -1, 1).reshape(nb, p, ll * lb * hh)

    al = jnp.concatenate([pr[ll].reshape(nb, 1, lb * p), pi[ll].reshape(nb, 1, lb * p)], axis=2)
    dskip = d_skip.astype(F32).reshape(nb, 1, lb * hh)
    return kblk.astype(BF16), vre, vim, wre, wim, al, dskip


def _mix_out_kernel(oa_ref, ob_ref, ys_ref, ga_ref, gb_ref, gc_ref, x_ref,
                    wa_ref, wb_ref, wca_ref, wcb_ref, wm_ref, gpost_ref, o_ref, acc_ref):
    n = pl.program_id(1)

    @pl.when(n == 0)
    def _():
        acc_ref[...] = jnp.zeros_like(acc_ref)

    oa, ob = oa_ref[...], ob_ref[...]
    ys = ys_ref[...].astype(BF16)
    tn = wa_ref.shape[1]
    halves = [slice(h * (tn // 2), (h + 1) * (tn // 2)) for h in range(2)]
    merged = []
    for hs in halves:
        y_a = jnp.dot(oa, wa_ref[:, hs], preferred_element_type=F32)
        y_b = jnp.dot(ob, wb_ref[:, hs], preferred_element_type=F32)
        glu_a = jnp.dot(ys, wca_ref[:, hs], preferred_element_type=F32)
        glu_b = jnp.dot(ys, wcb_ref[:, hs], preferred_element_type=F32)
        y_c = glu_a * _sigmoid(glu_b)
        merged.append((_sigmoid(ga_ref[:, hs].astype(F32)) * y_a + _sigmoid(gb_ref[:, hs].astype(F32)) * y_b
                       + _sigmoid(gc_ref[:, hs].astype(F32)) * y_c).astype(BF16))
    acc_ref[...] += (jnp.dot(merged[0], wm_ref[halves[0], :], preferred_element_type=F32)
                     + jnp.dot(merged[1], wm_ref[halves[1], :], preferred_element_type=F32))

    @pl.when(n == pl.num_programs(1) - 1)
    def _():
        o_ref[...] = x_ref[...] + _rms(acc_ref[...], gpost_ref[...])


def _mix_out(oa, ob, ys, z2, x2, w_dn_out, w_gla_out, w_s5_glu, w_mix_out, g_post, tm=512, tn=512):
    t, d = x2.shape
    nsteps = d // tn
    gblk = _OFF_GATES // tn
    ka, kb, kc = oa.shape[1], ob.shape[1], ys.shape[1]
    return pl.pallas_call(
        _mix_out_kernel,
        out_shape=jax.ShapeDtypeStruct((t, d), F32),
        grid=(t // tm, nsteps),
        in_specs=[
            pl.BlockSpec((tm, ka), lambda i, n: (i, 0)),
            pl.BlockSpec((tm, kb), lambda i, n: (i, 0)),
            pl.BlockSpec((tm, kc), lambda i, n: (i, 0)),
            pl.BlockSpec((tm, tn), lambda i, n: (i, gblk + n)),
            pl.BlockSpec((tm, tn), lambda i, n: (i, gblk + nsteps + n)),
            pl.BlockSpec((tm, tn), lambda i, n: (i, gblk + 2 * nsteps + n)),
            pl.BlockSpec((tm, d), lambda i, n: (i, 0)),
            pl.BlockSpec((ka, tn), lambda i, n: (0, n)),
            pl.BlockSpec((kb, tn), lambda i, n: (0, n)),
            pl.BlockSpec((kc, tn), lambda i, n: (0, n)),
            pl.BlockSpec((kc, tn), lambda i, n: (0, nsteps + n)),
            pl.BlockSpec((tn, d), lambda i, n: (n, 0)),
            pl.BlockSpec((1, d), lambda i, n: (0, 0)),
        ],
        out_specs=pl.BlockSpec((tm, d), lambda i, n: (i, 0)),
        scratch_shapes=[pltpu.VMEM((tm, d), F32)],
        compiler_params=_cparams(("parallel", "arbitrary")),
        name="mix_out",
    )(oa, ob, ys, z2, z2, z2, x2, w_dn_out, w_gla_out, w_s5_glu, w_s5_glu, w_mix_out, g_post)


def _xattn_kernel(x_ref, gpre_ref, wq_ref, kv_ref, wo_ref, gpost_ref, o_ref):
    x = x_ref[...]
    hn = _rms(x, gpre_ref[...]).astype(BF16)
    q = jnp.dot(hn, wq_ref[...], preferred_element_type=F32)
    kv = kv_ref[...]
    outs = []
    for h in range(XA_HEADS):
        qh = q[:, h * XA_DH:(h + 1) * XA_DH]
        kh = kv[:, h * XA_DH:(h + 1) * XA_DH]
        vh = kv[:, (XA_HEADS + h) * XA_DH:(XA_HEADS + h + 1) * XA_DH]
        s = _dot_nt(qh, kh) * (XA_DH ** -0.5)
        s = s - jnp.max(s, axis=-1, keepdims=True)
        p = jnp.exp(s)
        p = p / jnp.sum(p, axis=-1, keepdims=True)
        outs.append(_dot(p, vh))
    o = jnp.concatenate(outs, axis=1)
    y = jnp.dot(o.astype(BF16), wo_ref[...], preferred_element_type=F32)
    o_ref[...] = x + _rms(y, gpost_ref[...])


def _xattn(x3, g_pre, w_q, kv3, w_out, g_post, tm=512):
    b, s, d = x3.shape
    m, nkv = kv3.shape[1], kv3.shape[2]
    nq = w_q.shape[1]
    return pl.pallas_call(
        _xattn_kernel,
        out_shape=jax.ShapeDtypeStruct((b, s, d), F32),
        grid=(b, s // tm),
        in_specs=[
            pl.BlockSpec((None, tm, d), lambda bi, i: (bi, i, 0)),
            pl.BlockSpec((1, d), lambda bi, i: (0, 0)),
            pl.BlockSpec((d, nq), lambda bi, i: (0, 0)),
            pl.BlockSpec((None, m, nkv), lambda bi, i: (bi, 0, 0)),
            pl.BlockSpec((nq, d), lambda bi, i: (0, 0)),
            pl.BlockSpec((1, d), lambda bi, i: (0, 0)),
        ],
        out_specs=pl.BlockSpec((None, tm, d), lambda bi, i: (bi, i, 0)),
        compiler_params=_cparams(("parallel", "parallel")),
        name="xattn",
    )(x3, g_pre, w_q, kv3, w_out, g_post)


def _convffn_kernel(x_ref, xh_ref, gpre_ref, wg_ref, wu_ref, cw_ref, wd_ref, gpost_ref,
                    o_ref, hn_ref, hh_ref, gate_ref, acc_ref):
    i = pl.program_id(1)
    f = pl.program_id(2)
    tm = x_ref.shape[0]
    halo = SUBLANE

    @pl.when(f == 0)
    def _():
        hn_ref[...] = _rms(x_ref[...], gpre_ref[...]).astype(BF16)
        keep = jnp.where(i > 0, 1.0, 0.0)
        hh_ref[...] = (_rms(xh_ref[...], gpre_ref[...]) * keep).astype(BF16)
        acc_ref[...] = jnp.zeros_like(acc_ref)

    hn = hn_ref[...]
    wg = wg_ref[...]
    gate_ref[0:halo, :] = jnp.dot(hh_ref[...], wg, preferred_element_type=F32)
    gate_ref[halo:halo + tm, :] = jnp.dot(hn, wg, preferred_element_type=F32)
    up = jnp.dot(hn, wu_ref[...], preferred_element_type=F32)
    cw = cw_ref[...]
    base = halo - (FFN_CONV - 1)
    conv = gate_ref[base:base + tm, :] * cw[0:1, :]
    for j in range(1, FFN_CONV):
        conv = conv + gate_ref[base + j:base + j + tm, :] * cw[j:j + 1, :]
    act = (_gelu_tanh(conv) * up).astype(BF16)
    acc_ref[...] += jnp.dot(act, wd_ref[...], preferred_element_type=F32)

    @pl.when(f == pl.num_programs(2) - 1)
    def _():
        o_ref[...] = x_ref[...] + _rms(acc_ref[...], gpost_ref[...])


def _convffn(x3, g_pre, w_up, conv_w_t, w_down, g_post, tm=512, tf=512):
    b, s, d = x3.shape
    fh = w_down.shape[0]
    nf = fh // tf
    hb = tm // SUBLANE
    return pl.pallas_call(
        _convffn_kernel,
        out_shape=jax.ShapeDtypeStruct((b, s, d), F32),
        grid=(b, s // tm, nf),
        in_specs=[
            pl.BlockSpec((None, tm, d), lambda bi, i, f: (bi, i, 0)),
            pl.BlockSpec((None, SUBLANE, d), lambda bi, i, f: (bi, jnp.maximum(i * hb - 1, 0), 0)),
            pl.BlockSpec((1, d), lambda bi, i, f: (0, 0)),
            pl.BlockSpec((d, tf), lambda bi, i, f: (0, f)),
            pl.BlockSpec((d, tf), lambda bi, i, f: (0, nf + f)),
            pl.BlockSpec((FFN_CONV, tf), lambda bi, i, f: (0, f)),
            pl.BlockSpec((tf, d), lambda bi, i, f: (f, 0)),
            pl.BlockSpec((1, d), lambda bi, i, f: (0, 0)),
        ],
        out_specs=pl.BlockSpec((None, tm, d), lambda bi, i, f: (bi, i, 0)),
        scratch_shapes=[
            pltpu.VMEM((tm, d), BF16),
            pltpu.VMEM((SUBLANE, d), BF16),
            pltpu.VMEM((tm + SUBLANE, tf), F32),
            pltpu.VMEM((tm, d), F32),
        ],
        compiler_params=_cparams(("parallel", "parallel", "arbitrary")),
        name="convffn",
    )(x3, x3, g_pre, w_up, w_up, conv_w_t, w_down, g_post)


def _split_w_in(w_in_l):
    sizes = (DN_HEADS * DN_DK, DN_HEADS * DN_DK, DN_HEADS * DN_DV, DN_HEADS * DN_DV, DN_HEADS,
             DN_HEADS, GLA_HEADS * GLA_DK, GLA_HEADS * GLA_DK, GLA_HEADS * GLA_DV, GLA_RANK,
             GLA_HEADS * GLA_DV, S5_WIDTH, 3 * D_MODEL)
    parts = []
    acc = 0
    for n in sizes:
        parts.append(w_in_l[:, acc:acc + n])
        acc += n
    (dq, dk, dv, dz, db, da, gq, gk, gv, glow, gr, su, gates) = parts
    big = jnp.concatenate([dq, dk, dv, dz, gq, gk, gv, gr, su, gates], axis=1).astype(BF16)
    pad = jnp.zeros((w_in_l.shape[0], LANE - 2 * DN_HEADS - GLA_RANK), w_in_l.dtype)
    per_group = []
    for g in range(DN_HEADS // DN_HG):
        per_group += [db[:, g * DN_HG:(g + 1) * DN_HG], da[:, g * DN_HG:(g + 1) * DN_HG]]
    small = jnp.concatenate(per_group + [glow, pad], axis=1).astype(BF16)
    return big, small


def kernel(x, mem, norm_mix_pre, norm_mix_post, w_in, dn_conv_w, dn_a_log, dn_dt_bias, dn_norm_w, w_dn_out, gla_w_up, gla_b_up, gla_norm_w, w_gla_out, s5_lam_re, s5_lam_im, s5_log_step, s5_b_re, s5_b_im, s5_c_re, s5_c_im, s5_d, w_s5_glu, w_mix_out, norm_xa_pre, norm_xa_post, norm_mem, w_xa_q, w_xa_kv, w_xa_out, norm_ffn_pre, norm_ffn_post, w_ffn_up, ffn_conv_w, w_ffn_down):
    b, s, d = x.shape
    t = b * s
    depth = w_in.shape[0]
    mem2 = mem.reshape(-1, d)
    row = lambda a: a.reshape(1, -1).astype(F32)
    s5_mats = jax.vmap(_s5_matrices)(s5_lam_re, s5_lam_im, s5_log_step, s5_b_re, s5_b_im,
                                     s5_c_re, s5_c_im, s5_d)

    for l in range(depth):
        w_big, w_small = _split_w_in(w_in[l])
        z2, zs2, su2 = _in_proj(x.reshape(t, d), row(norm_mix_pre[l]), w_big, w_small)
        z3 = z2.reshape(b, s, -1)
        zs3 = zs2.reshape(b, s, -1)

        oa = _deltanet(z3, zs3, dn_a_log[l], dn_dt_bias[l], dn_conv_w[l].T.astype(F32),
                       row(dn_norm_w[l]))

        w_up_pad = jnp.zeros((zs2.shape[1], GLA_HEADS * GLA_DK), F32)
        w_up_pad = w_up_pad.at[_SM_GLOW:_SM_GLOW + GLA_RANK].set(gla_w_up[l]).astype(BF16)
        ob = _gla(z3, zs3, w_up_pad, row(gla_b_up[l]), row(gla_norm_w[l]))

        ys = _s5(su2, s5_mats, l, nbatch=b)

        x = _mix_out(oa.reshape(t, -1), ob.reshape(t, -1), ys, z2, x.reshape(t, d),
                     w_dn_out[l].astype(BF16), w_gla_out[l].astype(BF16), w_s5_glu[l].astype(BF16),
                     w_mix_out[l].astype(BF16), row(norm_mix_post[l])).reshape(b, s, d)

        kv = _norm_proj(mem2, row(norm_mem[l]), w_xa_kv[l].astype(BF16))
        x = _xattn(x, row(norm_xa_pre[l]), w_xa_q[l].astype(BF16), kv.reshape(b, -1, kv.shape[-1]),
                   w_xa_out[l].astype(BF16), row(norm_xa_post[l]))

        x = _convffn(x, row(norm_ffn_pre[l]), w_ffn_up[l].astype(BF16), ffn_conv_w[l].T.astype(F32),
                     w_ffn_down[l].astype(BF16), row(norm_ffn_post[l]))
    return x
```

```python
import jax
import jax.numpy as jnp
from jax import lax
from jax.experimental import pallas as pl
from jax.experimental.pallas import tpu as pltpu

F32 = jnp.float32
BF16 = jnp.bfloat16

NORM_EPS = 1e-6

D_MODEL = 2048
DN_HEADS, DN_DK, DN_DV, DN_CONV = 8, 128, 128, 4
GLA_HEADS, GLA_DK, GLA_DV, GLA_RANK, GLA_TAU = 4, 128, 256, 16, 16.0
S5_GROUP, S5_GROUPS, S5_STATE = 16, 64, 64
S5_WIDTH = S5_GROUP * S5_GROUPS
XA_HEADS, XA_DH = 4, 128
FFN_HIDDEN, FFN_CONV = 5632, 3

CHUNK = 64
GLA_SUB = 16
S5_L = 16
DN_HG = 4
DN_BASE = 8
LANE = 128
S5_LB = LANE // S5_GROUP
SUBLANE = 8
VMEM_LIMIT = 56 * 1024 * 1024

_OFF_DQ, _OFF_DK, _OFF_DV, _OFF_DZ = 0, 1024, 2048, 3072
_OFF_GQ, _OFF_GK, _OFF_GV, _OFF_GR = 4096, 4608, 5120, 6144
_OFF_SU, _OFF_GATES = 7168, 8192
_BIG_WIDTH = 14336
_SM_GLOW = 16


def _cparams(sem):
    return pltpu.CompilerParams(dimension_semantics=sem, vmem_limit_bytes=VMEM_LIMIT)


def _rms(xf, g):
    return xf * lax.rsqrt(jnp.mean(xf * xf, axis=-1, keepdims=True) + NORM_EPS) * g


def _dot(a, b):
    return jnp.dot(a.astype(BF16), b.astype(BF16), preferred_element_type=F32)


def _dot_nt(a, b):
    return lax.dot_general(a.astype(BF16), b.astype(BF16), (((1,), (1,)), ((), ())),
                           preferred_element_type=F32)


def _dot_tn(a, b):
    return lax.dot_general(a.astype(BF16), b.astype(BF16), (((0,), (0,)), ((), ())),
                           preferred_element_type=F32)


def _sigmoid(x):
    return 0.5 * jnp.tanh(0.5 * x) + 0.5


def _silu(x):
    return x * _sigmoid(x)


def _softplus(x):
    return jnp.maximum(x, 0.0) + jnp.log(1.0 + jnp.exp(-jnp.abs(x)))


def _gelu_tanh(x):
    c = 0.7978845608028654
    return 0.5 * x * (1.0 + jnp.tanh(c * (x + 0.044715 * (x * x * x))))


def _in_proj_kernel(x_ref, g_ref, w_ref, ws_ref, o_ref, os_ref, su_ref, hn_ref):
    j = pl.program_id(1)
    tn = o_ref.shape[1]

    @pl.when(j == 0)
    def _():
        hn = _rms(x_ref[...], g_ref[...]).astype(BF16)
        hn_ref[...] = hn
        os_ref[...] = jnp.dot(hn, ws_ref[...], preferred_element_type=F32)

    acc = jnp.dot(hn_ref[...], w_ref[...], preferred_element_type=F32)
    o_ref[...] = acc.astype(o_ref.dtype)

    @pl.when(jnp.logical_and(j >= _OFF_SU // tn, j < (_OFF_SU + S5_WIDTH) // tn))
    def _():
        su_ref[...] = acc


def _in_proj(x2, g, w_big, w_small, layer, tm=1024, tn=1024):
    t, d = x2.shape
    n = w_big.shape[2]
    ns = w_small.shape[2]
    j0, nsu = _OFF_SU // tn, S5_WIDTH // tn
    return pl.pallas_call(
        _in_proj_kernel,
        out_shape=(jax.ShapeDtypeStruct((t, n), BF16), jax.ShapeDtypeStruct((t, ns), F32),
                   jax.ShapeDtypeStruct((t, S5_WIDTH), F32)),
        grid=(t // tm, n // tn),
        in_specs=[
            pl.BlockSpec((tm, d), lambda i, j: (i, 0)),
            pl.BlockSpec((1, d), lambda i, j: (0, 0)),
            pl.BlockSpec((None, d, tn), lambda i, j: (layer, 0, j)),
            pl.BlockSpec((None, d, ns), lambda i, j: (layer, 0, 0)),
        ],
        out_specs=(
            pl.BlockSpec((tm, tn), lambda i, j: (i, j)),
            pl.BlockSpec((tm, ns), lambda i, j: (i, 0)),
            pl.BlockSpec((tm, tn), lambda i, j: (i, jnp.clip(j - j0, 0, nsu - 1))),
        ),
        scratch_shapes=[pltpu.VMEM((tm, d), BF16)],
        compiler_params=_cparams(("parallel", "arbitrary")),
        name="in_proj",
    )(x2, g, w_big, w_small)


def _norm_proj_kernel(x_ref, g_ref, w_ref, o_ref):
    hn = _rms(x_ref[...], g_ref[...]).astype(BF16)
    o_ref[...] = jnp.dot(hn, w_ref[...], preferred_element_type=F32).astype(o_ref.dtype)


def _norm_proj(x2, g, w, layer, tm=512):
    t, d = x2.shape
    n = w.shape[2]
    return pl.pallas_call(
        _norm_proj_kernel,
        out_shape=jax.ShapeDtypeStruct((t, n), BF16),
        grid=(t // tm,),
        in_specs=[
            pl.BlockSpec((tm, d), lambda i: (i, 0)),
            pl.BlockSpec((1, d), lambda i: (0, 0)),
            pl.BlockSpec((None, d, n), lambda i: (layer, 0, 0)),
        ],
        out_specs=pl.BlockSpec((tm, n), lambda i: (i, 0)),
        compiler_params=_cparams(("parallel",)),
        name="mem_kv_proj",
    )(x2, g, w)


def _dot_hilo_rhs(a01, b):
    hi = b.astype(BF16)
    lo = (b - hi.astype(F32)).astype(BF16)
    return (jnp.dot(a01, hi, preferred_element_type=F32) + jnp.dot(a01, lo, preferred_element_type=F32))


def _dot3(a, b):
    ah = a.astype(BF16)
    al = (a - ah.astype(F32)).astype(BF16)
    bh = b.astype(BF16)
    bl = (b - bh.astype(F32)).astype(BF16)
    return (jnp.dot(ah, bh, preferred_element_type=F32)
            + (jnp.dot(ah, bl, preferred_element_type=F32) + jnp.dot(al, bh, preferred_element_type=F32)))


def _dot_hilo_lhs(a, b01):
    hi = a.astype(BF16)
    lo = (a - hi.astype(F32)).astype(BF16)
    return (jnp.dot(hi, b01, preferred_element_type=F32) + jnp.dot(lo, b01, preferred_element_type=F32))


def _deltanet_kernel(q_ref, k_ref, v_ref, zg_ref, zs_ref, alr_ref, dtr_ref, alc_ref,
                     dtc_ref, cwq_ref, cwk_ref, cwv_ref, nw_ref, o_ref, state_ref, tail_ref, ext_ref):
    tt = q_ref.shape[0]
    c = CHUNK
    hg = DN_HG
    dk, dv = DN_DK, DN_DV
    halo = SUBLANE
    nchunk = tt // c

    @pl.when(pl.program_id(2) == 0)
    def _():
        state_ref[...] = jnp.zeros_like(state_ref)
        tail_ref[...] = jnp.zeros_like(tail_ref)

    def conv_silu(idx, raw_ref, cw_ref):
        raw = raw_ref[...].astype(F32)
        ext_ref[idx, 0:halo, :] = tail_ref[idx]
        ext_ref[idx, halo:halo + tt, :] = raw
        tail_ref[idx] = raw[tt - halo:tt, :]
        cw = cw_ref[...]
        base = halo - (DN_CONV - 1)
        y = ext_ref[idx, base:base + tt, :] * cw[0:1, :]
        for j in range(1, DN_CONV):
            y = y + ext_ref[idx, base + j:base + j + tt, :] * cw[j:j + 1, :]
        return _silu(y)

    q_all = conv_silu(0, q_ref, cwq_ref)
    k_all = conv_silu(1, k_ref, cwk_ref)
    v_all = conv_silu(2, v_ref, cwv_ref)

    grp = pl.program_id(1)
    gcol = pltpu.roll(zs_ref[...], (LANE - 2 * hg * grp) % LANE, 1)
    grow = gcol.T[0:SUBLANE, :]
    g_rows = -jnp.exp(alr_ref[...]) * _softplus(grow + dtr_ref[...])
    beta_cols = _sigmoid(gcol)
    g_cols = -jnp.exp(alc_ref[...]) * _softplus(gcol + dtc_ref[...])

    ri = lax.broadcasted_iota(jnp.int32, (c, c), 0)
    ci = lax.broadcasted_iota(jnp.int32, (c, c), 1)
    causal = ri >= ci
    strict = ri > ci
    ident = jnp.where(ri == ci, 1.0, 0.0).astype(F32)
    tril01 = jnp.where(causal, 1.0, 0.0).astype(BF16)
    triu01 = jnp.where(ri <= ci, 1.0, 0.0).astype(BF16)

    items = [(hh, n) for n in range(nchunk) for hh in range(hg)]
    qs, lower, a_qk, e_col, k_dec, g_last, vb = {}, {}, {}, {}, {}, {}, {}
    for n in range(nchunk):
        sl = slice(n * c, (n + 1) * c)
        gcs_c = _dot_hilo_rhs(tril01, g_cols[sl])
        gcs_r = _dot_hilo_lhs(g_rows[:, sl], triu01)
        for hh in range(hg):
            it = (hh, n)
            hs = slice(hh * dk, (hh + 1) * dk)
            qc, kc = q_all[sl, hs], k_all[sl, hs]
            qc = qc * (lax.rsqrt(jnp.sum(qc * qc, axis=-1, keepdims=True) + NORM_EPS) * (dk ** -0.5))
            kc = kc * lax.rsqrt(jnp.sum(kc * kc, axis=-1, keepdims=True) + NORM_EPS)
            beta_col = beta_cols[sl, hh:hh + 1]
            gc_col = gcs_c[:, hg + hh:hg + hh + 1]
            gc_row = gcs_r[hg + hh:hg + hh + 1, :]
            gc_end = gcs_r[hg + hh:hg + hh + 1, c - 1:c]
            decay = jnp.where(causal, jnp.exp(jnp.where(causal, gc_col - gc_row, 0.0)), 0.0)
            kb = kc * beta_col
            res = _dot_nt(jnp.concatenate([kb, qc], axis=0), kc)
            lower[it] = jnp.where(strict, res[0:c] * decay, 0.0)
            a_qk[it] = jnp.where(causal, res[c:2 * c] * decay, 0.0)
            e_col[it] = jnp.exp(gc_col)
            k_dec[it] = kc * jnp.exp(gc_end - gc_col)
            g_last[it] = jnp.exp(gc_end)
            vb[it] = jnp.concatenate([v_all[sl, hh * dv:(hh + 1) * dv] * beta_col, kb * e_col[it]], axis=1)
            qs[it] = qc

    base_shift = DN_BASE.bit_length() - 1
    same_base = (ri >> base_shift) == (ci >> base_shift)
    inv, pw = {}, {}
    for it in items:
        lb = jnp.where(same_base, lower[it], 0.0)
        inv[it] = ident - lb
        pw[it] = _dot3(lb, lb)
    span = 2
    while span < DN_BASE:
        for it in items:
            if span * 2 < DN_BASE:
                res = _dot3(jnp.concatenate([inv[it], pw[it]], axis=0), pw[it])
                inv[it] = inv[it] + res[0:c]
                pw[it] = res[c:2 * c]
            else:
                inv[it] = inv[it] + _dot3(inv[it], pw[it])
        span *= 2
    width = 2 * DN_BASE
    while width <= c:
        sh = width.bit_length() - 1
        same_w = (ri >> sh) == (ci >> sh)
        same_half = (ri >> (sh - 1)) == (ci >> (sh - 1))
        off = {it: jnp.where(same_w, jnp.where(same_half, 0.0, lower[it]), 0.0) for it in items}
        mid = {it: _dot3(off[it], inv[it]) for it in items}
        for it in items:
            inv[it] = inv[it] - _dot3(inv[it], mid[it])
        width *= 2

    uw = {it: _dot3(inv[it], vb[it]) for it in items}
    a_uw = {it: _dot(a_qk[it], uw[it]) for it in items}
    k_uw = {it: _dot_tn(k_dec[it], uw[it]) for it in items}

    nw = nw_ref[...]
    state = [state_ref[hh] for hh in range(hg)]
    for n in range(nchunk):
        sl = slice(n * c, (n + 1) * c)
        for hh in range(hg):
            it = (hh, n)
            q_eff = qs[it] * e_col[it] - a_uw[it][:, dv:dv + dk]
            res = _dot(jnp.concatenate([q_eff, k_uw[it][:, dv:dv + dk]], axis=0), state[hh])
            o = res[0:c] + a_uw[it][:, 0:dv]
            state[hh] = state[hh] * g_last[it] - res[c:c + dk] + k_uw[it][:, 0:dv]
            hs = slice(hh * dv, (hh + 1) * dv)
            o = _rms(o, nw) * _silu(zg_ref[sl, hs].astype(F32))
            o_ref[sl, hs] = o.astype(o_ref.dtype)
    for hh in range(hg):
        state_ref[hh] = state[hh]


def _deltanet(z3, zs3, dn_a_log, dn_dt_bias, conv_w_t, norm_w, tt=256):
    b, s, _ = z3.shape
    nh, hg = DN_HEADS, DN_HG
    ngrp = nh // hg
    wq = hg * DN_DK
    bq, bk, bv, bz = _OFF_DQ // wq, _OFF_DK // wq, _OFF_DV // wq, _OFF_DZ // wq
    al = dn_a_log.astype(F32).reshape(ngrp, hg)
    dtb = dn_dt_bias.astype(F32).reshape(ngrp, hg)
    alr = jnp.pad(al, ((0, 0), (hg, SUBLANE - 2 * hg)))[:, :, None]
    dtr = jnp.pad(dtb, ((0, 0), (hg, SUBLANE - 2 * hg)))[:, :, None]
    alc = jnp.pad(al, ((0, 0), (hg, LANE - 2 * hg)))[:, None, :]
    dtc = jnp.pad(dtb, ((0, 0), (hg, LANE - 2 * hg)))[:, None, :]
    return pl.pallas_call(
        _deltanet_kernel,
        out_shape=jax.ShapeDtypeStruct((b, s, nh * DN_DV), BF16),
        grid=(b, ngrp, s // tt),
        in_specs=[
            pl.BlockSpec((None, tt, wq), lambda bi, g, i: (bi, i, bq + g)),
            pl.BlockSpec((None, tt, wq), lambda bi, g, i: (bi, i, bk + g)),
            pl.BlockSpec((None, tt, wq), lambda bi, g, i: (bi, i, bv + g)),
            pl.BlockSpec((None, tt, wq), lambda bi, g, i: (bi, i, bz + g)),
            pl.BlockSpec((None, tt, LANE), lambda bi, g, i: (bi, i, 0)),
            pl.BlockSpec((None, SUBLANE, 1), lambda bi, g, i: (g, 0, 0)),
            pl.BlockSpec((None, SUBLANE, 1), lambda bi, g, i: (g, 0, 0)),
            pl.BlockSpec((None, 1, LANE), lambda bi, g, i: (g, 0, 0)),
            pl.BlockSpec((None, 1, LANE), lambda bi, g, i: (g, 0, 0)),
            pl.BlockSpec((DN_CONV, wq), lambda bi, g, i: (0, g)),
            pl.BlockSpec((DN_CONV, wq), lambda bi, g, i: (0, ngrp + g)),
            pl.BlockSpec((DN_CONV, wq), lambda bi, g, i: (0, 2 * ngrp + g)),
            pl.BlockSpec((1, DN_DV), lambda bi, g, i: (0, 0)),
        ],
        out_specs=pl.BlockSpec((None, tt, wq), lambda bi, g, i: (bi, i, g)),
        scratch_shapes=[
            pltpu.VMEM((hg, DN_DK, DN_DV), F32),
            pltpu.VMEM((3, SUBLANE, wq), F32),
            pltpu.VMEM((3, tt + SUBLANE, wq), F32),
        ],
        compiler_params=_cparams(("parallel", "parallel", "arbitrary")),
        name="deltanet",
    )(z3, z3, z3, z3, zs3, alr, dtr, alc, dtc, conv_w_t, conv_w_t, conv_w_t, norm_w)


def _gla_kernel(q_ref, k_ref, v_ref, r_ref, zs_ref, wup_ref, bup_ref, nw_ref, o_ref, state_ref):
    tt = q_ref.shape[0]
    c = CHUNK
    sub = GLA_SUB

    @pl.when(pl.program_id(2) == 0)
    def _():
        state_ref[...] = jnp.zeros_like(state_ref)

    logit = _dot(zs_ref[...], wup_ref[...]) + bup_ref[...]
    log_a = -_softplus(-logit) * (1.0 / GLA_TAU)

    ri = lax.broadcasted_iota(jnp.int32, (c, c), 0)
    ci = lax.broadcasted_iota(jnp.int32, (c, c), 1)
    tril = jnp.where(ri >= ci, 1.0, 0.0).astype(BF16)
    sub_rows = lax.broadcasted_iota(jnp.int32, (sub, 1), 0)
    key_rows = lax.broadcasted_iota(jnp.int32, (c, 1), 0)
    blk_cols = lax.broadcasted_iota(jnp.int32, (sub, c), 1)

    nw = nw_ref[...]
    scale = GLA_DK ** -0.5
    nchunk = tt // c
    nblk = c // sub
    chunks = range(nchunk)
    bcum = [_dot_hilo_rhs(tril, log_a[n * c:(n + 1) * c]) for n in chunks]
    qc = [q_ref[n * c:(n + 1) * c, :].astype(F32) * scale for n in chunks]
    kc = [k_ref[n * c:(n + 1) * c, :].astype(F32) for n in chunks]
    vc = [v_ref[n * c:(n + 1) * c, :] for n in chunks]
    b_last = [bcum[n][c - 1:c, :] for n in chunks]
    q_dec = [qc[n] * jnp.exp(bcum[n]) for n in chunks]
    k_dec = [kc[n] * jnp.exp(b_last[n] - bcum[n]) for n in chunks]
    kv = [_dot_tn(vc[n], k_dec[n]) for n in chunks]

    off = {}
    for n in chunks:
        for bi in range(1, nblk):
            r0 = bi * sub
            b_ref = bcum[n][r0 - 1:r0, :]
            early = key_rows < r0
            q_f = qc[n][r0:r0 + sub] * jnp.exp(bcum[n][r0:r0 + sub] - b_ref)
            k_f = jnp.where(early, kc[n] * jnp.exp(jnp.where(early, b_ref - bcum[n], 0.0)), 0.0)
            off[n, bi] = _dot_nt(q_f, k_f)

    a_qk = []
    for n in chunks:
        row_blocks = []
        for bi in range(nblk):
            r0 = bi * sub
            q_i = qc[n][r0:r0 + sub]
            b_i = bcum[n][r0:r0 + sub]
            blk = off[n, bi] if bi > 0 else jnp.zeros((sub, c), F32)
            for j in range(sub):
                kj = kc[n][r0 + j:r0 + j + 1, :]
                bj = bcum[n][r0 + j:r0 + j + 1, :]
                live = sub_rows >= j
                dec = jnp.exp(jnp.where(live, b_i - bj, 0.0))
                col = jnp.sum(jnp.where(live, q_i * kj * dec, 0.0), axis=1, keepdims=True)
                blk = jnp.where(blk_cols == r0 + j, col, blk)
            row_blocks.append(blk)
        a_qk.append(jnp.concatenate(row_blocks, axis=0))

    intra = [_dot(a_qk[n], vc[n]) for n in chunks]
    state_t = state_ref[...]
    for n in chunks:
        sl = slice(n * c, (n + 1) * c)
        o = _dot_nt(q_dec[n], state_t) + intra[n]
        state_t = state_t * jnp.exp(b_last[n]) + kv[n]
        o = _rms(o, nw) * _silu(r_ref[sl, :].astype(F32))
        o_ref[sl, :] = o.astype(o_ref.dtype)
    state_ref[...] = state_t


def _gla(z3, zs3, w_up_pad, b_up, norm_w, tt=256):
    b, s, _ = z3.shape
    hq, hk = _OFF_GQ // GLA_DK, _OFF_GK // GLA_DK
    hv, hr = _OFF_GV // GLA_DV, _OFF_GR // GLA_DV
    ns = zs3.shape[-1]
    return pl.pallas_call(
        _gla_kernel,
        out_shape=jax.ShapeDtypeStruct((b, s, GLA_HEADS * GLA_DV), BF16),
        grid=(b, GLA_HEADS, s // tt),
        in_specs=[
            pl.BlockSpec((None, tt, GLA_DK), lambda bi, h, i: (bi, i, hq + h)),
            pl.BlockSpec((None, tt, GLA_DK), lambda bi, h, i: (bi, i, hk + h)),
            pl.BlockSpec((None, tt, GLA_DV), lambda bi, h, i: (bi, i, hv + h)),
            pl.BlockSpec((None, tt, GLA_DV), lambda bi, h, i: (bi, i, hr + h)),
            pl.BlockSpec((None, tt, ns), lambda bi, h, i: (bi, i, 0)),
            pl.BlockSpec((ns, GLA_DK), lambda bi, h, i: (0, h)),
            pl.BlockSpec((1, GLA_DK), lambda bi, h, i: (0, h)),
            pl.BlockSpec((1, GLA_DV), lambda bi, h, i: (0, 0)),
        ],
        out_specs=pl.BlockSpec((None, tt, GLA_DV), lambda bi, h, i: (bi, i, h)),
        scratch_shapes=[pltpu.VMEM((GLA_DV, GLA_DK), F32)],
        compiler_params=_cparams(("parallel", "parallel", "arbitrary")),
        name="gla",
    )(z3, z3, z3, z3, zs3, w_up_pad, b_up, norm_w)


def _s5_kernel(su_ref, kblk_ref, vre_ref, vim_ref, wre_ref, wim_ref, al_ref, d_ref,
               o_ref, m_ref, v_ref, w_ref, e_ref, xp_ref):
    ll = S5_L
    rows = e_ref.shape[0]
    nst = S5_LB * S5_STATE

    @pl.when(pl.program_id(1) == 0)
    def _():
        zero_blk = jnp.zeros((LANE, LANE), BF16)
        for s in range(ll):
            for i in range(ll):
                m_ref[s * LANE:(s + 1) * LANE, i * LANE:(i + 1) * LANE] = (
                    kblk_ref[i - s] if i >= s else zero_blk)
        rblk = 2 * LANE
        vr = lax.broadcasted_iota(jnp.int32, (rblk, nst), 0)
        vc = lax.broadcasted_iota(jnp.int32, (rblk, nst), 1)
        keep_v = ((vr & (LANE - 1)) >> 4) == (vc >> 6)
        for r in range(ll * LANE // rblk):
            rs = slice(r * rblk, (r + 1) * rblk)
            v_ref[rs, 0:nst] = jnp.where(keep_v, jnp.tile(vre_ref[rs, :], (1, nst // LANE)), 0.0).astype(BF16)
            v_ref[rs, nst:2 * nst] = jnp.where(keep_v, jnp.tile(vim_ref[rs, :], (1, nst // LANE)), 0.0).astype(BF16)
        wr = lax.broadcasted_iota(jnp.int32, (nst, rblk), 0)
        wc = lax.broadcasted_iota(jnp.int32, (nst, rblk), 1)
        keep_w = (wr >> 6) == ((wc & (LANE - 1)) >> 4)
        for r in range(ll * LANE // rblk):
            cs = slice(r * rblk, (r + 1) * rblk)
            w_ref[0:nst, cs] = jnp.where(keep_w, jnp.tile(wre_ref[:, cs], (S5_LB, 1)), 0.0).astype(BF16)
            w_ref[nst:2 * nst, cs] = jnp.where(keep_w, jnp.tile(wim_ref[:, cs], (S5_LB, 1)), 0.0).astype(BF16)

    u_list = [su_ref[pl.ds(s, rows, stride=ll), :] for s in range(ll)]
    u_cat = jnp.concatenate([u.astype(BF16) for u in u_list], axis=1)
    e_ref[...] = jnp.dot(u_cat, v_ref[...], preferred_element_type=F32)
    ar = al_ref[:, 0:nst]
    ai = al_ref[:, nst:2 * nst]

    def step(cidx, carry):
        xr, xi = carry
        row = pl.ds(cidx, 1)
        xp_ref[row, 0:nst] = xr
        xp_ref[row, nst:2 * nst] = xi
        er = e_ref[row, 0:nst]
        ei = e_ref[row, nst:2 * nst]
        return ar * xr - ai * xi + er, ar * xi + ai * xr + ei

    zero = jnp.zeros((1, nst), F32)
    lax.fori_loop(0, rows, step, (zero, zero), unroll=8)

    y = (jnp.dot(u_cat, m_ref[...], preferred_element_type=F32)
         + jnp.dot(xp_ref[...].astype(BF16), w_ref[...], preferred_element_type=F32))
    d = d_ref[...]
    for i in range(ll):
        yi = y[:, i * LANE:(i + 1) * LANE] + d * u_list[i]
        o_ref[pl.ds(i, rows, stride=ll), :] = _gelu_tanh(yi)


def _s5(su2, mats, layer, nbatch):
    kblk, vre, vim, wre, wim, al, dskip = mats
    t, width = su2.shape
    s = t // nbatch
    ll = S5_L
    rows = s // ll
    nblk = width // LANE
    nst2 = 2 * S5_LB * S5_STATE

    return pl.pallas_call(
        _s5_kernel,
        out_shape=jax.ShapeDtypeStruct((t, width), F32),
        grid=(nblk, nbatch),
        in_specs=[
            pl.BlockSpec((s, LANE), lambda j, b: (b, j)),
            pl.BlockSpec((None, None, ll, LANE, LANE), lambda j, b: (layer, j, 0, 0, 0)),
            pl.BlockSpec((None, None, ll * LANE, LANE), lambda j, b: (layer, j, 0, 0)),
            pl.BlockSpec((None, None, ll * LANE, LANE), lambda j, b: (layer, j, 0, 0)),
            pl.BlockSpec((None, None, S5_STATE, ll * LANE), lambda j, b: (layer, j, 0, 0)),
            pl.BlockSpec((None, None, S5_STATE, ll * LANE), lambda j, b: (layer, j, 0, 0)),
            pl.BlockSpec((None, None, 1, nst2), lambda j, b: (layer, j, 0, 0)),
            pl.BlockSpec((None, None, 1, LANE), lambda j, b: (layer, j, 0, 0)),
        ],
        out_specs=pl.BlockSpec((s, LANE), lambda j, b: (b, j)),
        scratch_shapes=[
            pltpu.VMEM((ll * LANE, ll * LANE), BF16),
            pltpu.VMEM((ll * LANE, nst2), BF16),
            pltpu.VMEM((nst2, ll * LANE), BF16),
            pltpu.VMEM((rows, nst2), F32),
            pltpu.VMEM((rows, nst2), F32),
        ],
        compiler_params=_cparams(("parallel", "arbitrary")),
        name="s5",
    )(su2, kblk, vre, vim, wre, wim, al, dskip)


def _s5_matrices(lam_re, lam_im, log_step, b_re, b_im, c_re, c_im, d_skip):
    g, p, hh, ll, lb = S5_GROUPS, S5_STATE, S5_GROUP, S5_L, S5_LB
    nb = g // lb
    lr = jnp.minimum(lam_re.astype(F32), -1e-4)
    li = lam_im.astype(F32)
    dt = jnp.exp(log_step.astype(F32))[:, None]
    mag = jnp.exp(lr * dt)
    ar = mag * jnp.cos(li * dt)
    ai = mag * jnp.sin(li * dt)
    nr = ar - 1.0
    den = lr * lr + li * li
    zr = (nr * lr + ai * li) / den
    zi = (ai * lr - nr * li) / den
    br = b_re.astype(F32)
    bi = b_im.astype(F32)
    bbr = zr[..., None] * br - zi[..., None] * bi
    bbi = zr[..., None] * bi + zi[..., None] * br
    pr = [jnp.ones_like(ar)]
    pi = [jnp.zeros_like(ai)]
    for _ in range(ll):
        pr.append(pr[-1] * ar - pi[-1] * ai)
        pi.append(pr[-2] * ai + pi[-1] * ar)
    pr_rev = jnp.stack(pr[ll - 1::-1], axis=0)
    pi_rev = jnp.stack(pi[ll - 1::-1], axis=0)
    pr = jnp.stack(pr, axis=0)
    pi = jnp.stack(pi, axis=0)
    cr = c_re.astype(F32)
    cim = c_im.astype(F32)

    def blk(a, *shape):
        return a.reshape((nb, lb) + shape)

    pr_d = jnp.moveaxis(pr.reshape(ll + 1, nb, lb, p), 0, 1)[:, :, :, None, None, :]
    pi_d = jnp.moveaxis(pi.reshape(ll + 1, nb, lb, p), 0, 1)[:, :, :, None, None, :]
    cr_b = blk(cr, hh, p)[:, None, :, None, :, :]
    ci_b = blk(cim, hh, p)[:, None, :, None, :, :]
    qr = cr_b * pr_d - ci_b * pi_d
    qi = cr_b * pi_d + ci_b * pr_d
    bbr_b = blk(jnp.swapaxes(bbr, 1, 2), hh, p)[:, None, :, :, None, :]
    bbi_b = blk(jnp.swapaxes(bbi, 1, 2), hh, p)[:, None, :, :, None, :]
    kd = jnp.sum(qr[:, :ll] * bbr_b - qi[:, :ll] * bbi_b, axis=-1)
    eye = jnp.eye(lb, dtype=F32)[None, None, :, None, :, None]
    kblk = (kd[:, :, :, :, None, :] * eye).reshape(nb, ll, lb * hh, lb * hh)
    prs = jnp.moveaxis(pr_rev.reshape(ll, nb, lb, p), 0, 1)[:, :, :, None, :]
    pis = jnp.moveaxis(pi_rev.reshape(ll, nb, lb, p), 0, 1)[:, :, :, None, :]
    bbr_v, bbi_v = bbr_b[:, :, :, :, 0, :], bbi_b[:, :, :, :, 0, :]
    vre = jnp.tile((prs * bbr_v - pis * bbi_v).reshape(nb, ll * lb * hh, p), (1, 1, 2))
    vim = jnp.tile((prs * bbi_v + pis * bbr_v).reshape(nb, ll * lb * hh, p), (1, 1, 2))
    wre = jnp.moveaxis(qr[:, 1:ll + 1, :, 0], -1, 1).reshape(nb, p, ll * lb * hh)
    wim = -jnp.moveaxis(qi[:, 1:ll + 1, :, 0], -1, 1).reshape(nb, p, ll * lb * hh)

    al = jnp.concatenate([pr[ll].reshape(nb, 1, lb * p), pi[ll].reshape(nb, 1, lb * p)], axis=2)
    dskip = d_skip.astype(F32).reshape(nb, 1, lb * hh)
    return kblk.astype(BF16), vre, vim, wre, wim, al, dskip


def _mix_out_kernel(oa_ref, ob_ref, ys_ref, ga_ref, gb_ref, gc_ref, x_ref,
                    wa_ref, wb_ref, wca_ref, wcb_ref, wm_ref, gpost_ref, o_ref, acc_ref):
    n = pl.program_id(1)

    @pl.when(n == 0)
    def _():
        acc_ref[...] = jnp.zeros_like(acc_ref)

    oa, ob = oa_ref[...], ob_ref[...]
    ys = ys_ref[...].astype(BF16)
    tn = wa_ref.shape[1]
    halves = [slice(h * (tn // 2), (h + 1) * (tn // 2)) for h in range(2)]
    merged = []
    for hs in halves:
        y_a = jnp.dot(oa, wa_ref[:, hs], preferred_element_type=F32)
        y_b = jnp.dot(ob, wb_ref[:, hs], preferred_element_type=F32)
        glu_a = jnp.dot(ys, wca_ref[:, hs], preferred_element_type=F32)
        glu_b = jnp.dot(ys, wcb_ref[:, hs], preferred_element_type=F32)
        y_c = glu_a * _sigmoid(glu_b)
        merged.append((_sigmoid(ga_ref[:, hs].astype(F32)) * y_a + _sigmoid(gb_ref[:, hs].astype(F32)) * y_b
                       + _sigmoid(gc_ref[:, hs].astype(F32)) * y_c).astype(BF16))
    acc_ref[...] += (jnp.dot(merged[0], wm_ref[halves[0], :], preferred_element_type=F32)
                     + jnp.dot(merged[1], wm_ref[halves[1], :], preferred_element_type=F32))

    @pl.when(n == pl.num_programs(1) - 1)
    def _():
        o_ref[...] = x_ref[...] + _rms(acc_ref[...], gpost_ref[...])


def _mix_out(oa, ob, ys, z2, x2, w_dn_out, w_gla_out, w_s5_glu, w_mix_out, g_post, layer, tm=512, tn=512):
    t, d = x2.shape
    nsteps = d // tn
    gblk = _OFF_GATES // tn
    ka, kb, kc = oa.shape[1], ob.shape[1], ys.shape[1]
    return pl.pallas_call(
        _mix_out_kernel,
        out_shape=jax.ShapeDtypeStruct((t, d), F32),
        grid=(t // tm, nsteps),
        in_specs=[
            pl.BlockSpec((tm, ka), lambda i, n: (i, 0)),
            pl.BlockSpec((tm, kb), lambda i, n: (i, 0)),
            pl.BlockSpec((tm, kc), lambda i, n: (i, 0)),
            pl.BlockSpec((tm, tn), lambda i, n: (i, gblk + n)),
            pl.BlockSpec((tm, tn), lambda i, n: (i, gblk + nsteps + n)),
            pl.BlockSpec((tm, tn), lambda i, n: (i, gblk + 2 * nsteps + n)),
            pl.BlockSpec((tm, d), lambda i, n: (i, 0)),
            pl.BlockSpec((None, ka, tn), lambda i, n: (layer, 0, n)),
            pl.BlockSpec((None, kb, tn), lambda i, n: (layer, 0, n)),
            pl.BlockSpec((None, kc, tn), lambda i, n: (layer, 0, n)),
            pl.BlockSpec((None, kc, tn), lambda i, n: (layer, 0, nsteps + n)),
            pl.BlockSpec((None, tn, d), lambda i, n: (layer, n, 0)),
            pl.BlockSpec((1, d), lambda i, n: (0, 0)),
        ],
        out_specs=pl.BlockSpec((tm, d), lambda i, n: (i, 0)),
        scratch_shapes=[pltpu.VMEM((tm, d), F32)],
        compiler_params=_cparams(("parallel", "arbitrary")),
        name="mix_out",
    )(oa, ob, ys, z2, z2, z2, x2, w_dn_out, w_gla_out, w_s5_glu, w_s5_glu, w_mix_out, g_post)


def _xattn_kernel(x_ref, gpre_ref, wq_ref, kv_ref, wo_ref, gpost_ref, o_ref):
    x = x_ref[...]
    hn = _rms(x, gpre_ref[...]).astype(BF16)
    q = jnp.dot(hn, wq_ref[...], preferred_element_type=F32)
    kv = kv_ref[...]
    outs = []
    for h in range(XA_HEADS):
        qh = q[:, h * XA_DH:(h + 1) * XA_DH]
        kh = kv[:, h * XA_DH:(h + 1) * XA_DH]
        vh = kv[:, (XA_HEADS + h) * XA_DH:(XA_HEADS + h + 1) * XA_DH]
        s = _dot_nt(qh, kh) * (XA_DH ** -0.5)
        s = s - jnp.max(s, axis=-1, keepdims=True)
        p = jnp.exp(s)
        p = p / jnp.sum(p, axis=-1, keepdims=True)
        outs.append(_dot(p, vh))
    o = jnp.concatenate(outs, axis=1)
    y = jnp.dot(o.astype(BF16), wo_ref[...], preferred_element_type=F32)
    o_ref[...] = x + _rms(y, gpost_ref[...])


def _xattn(x3, g_pre, w_q, kv3, w_out, g_post, layer, tm=512):
    b, s, d = x3.shape
    m, nkv = kv3.shape[1], kv3.shape[2]
    nq = w_q.shape[2]
    return pl.pallas_call(
        _xattn_kernel,
        out_shape=jax.ShapeDtypeStruct((b, s, d), F32),
        grid=(b, s // tm),
        in_specs=[
            pl.BlockSpec((None, tm, d), lambda bi, i: (bi, i, 0)),
            pl.BlockSpec((1, d), lambda bi, i: (0, 0)),
            pl.BlockSpec((None, d, nq), lambda bi, i: (layer, 0, 0)),
            pl.BlockSpec((None, m, nkv), lambda bi, i: (bi, 0, 0)),
            pl.BlockSpec((None, nq, d), lambda bi, i: (layer, 0, 0)),
            pl.BlockSpec((1, d), lambda bi, i: (0, 0)),
        ],
        out_specs=pl.BlockSpec((None, tm, d), lambda bi, i: (bi, i, 0)),
        compiler_params=_cparams(("parallel", "parallel")),
        name="xattn",
    )(x3, g_pre, w_q, kv3, w_out, g_post)


def _convffn_kernel(x_ref, xh_ref, gpre_ref, wg_ref, wu_ref, cw_ref, wd_ref, gpost_ref,
                    o_ref, hn_ref, hh_ref, gate_ref, acc_ref):
    i = pl.program_id(1)
    f = pl.program_id(2)
    tm = x_ref.shape[0]
    halo = SUBLANE

    @pl.when(f == 0)
    def _():
        hn_ref[...] = _rms(x_ref[...], gpre_ref[...]).astype(BF16)
        keep = jnp.where(i > 0, 1.0, 0.0)
        hh_ref[...] = (_rms(xh_ref[...], gpre_ref[...]) * keep).astype(BF16)
        acc_ref[...] = jnp.zeros_like(acc_ref)

    hn = hn_ref[...]
    wg = wg_ref[...]
    gate_ref[0:halo, :] = jnp.dot(hh_ref[...], wg, preferred_element_type=F32)
    gate_ref[halo:halo + tm, :] = jnp.dot(hn, wg, preferred_element_type=F32)
    up = jnp.dot(hn, wu_ref[...], preferred_element_type=F32)
    cw = cw_ref[...]
    base = halo - (FFN_CONV - 1)
    conv = gate_ref[base:base + tm, :] * cw[0:1, :]
    for j in range(1, FFN_CONV):
        conv = conv + gate_ref[base + j:base + j + tm, :] * cw[j:j + 1, :]
    act = (_gelu_tanh(conv) * up).astype(BF16)
    acc_ref[...] += jnp.dot(act, wd_ref[...], preferred_element_type=F32)

    @pl.when(f == pl.num_programs(2) - 1)
    def _():
        o_ref[...] = x_ref[...] + _rms(acc_ref[...], gpost_ref[...])


def _convffn(x3, g_pre, w_up, conv_w_t, w_down, g_post, layer, tm=512, tf=512):
    b, s, d = x3.shape
    fh = w_down.shape[1]
    nf = fh // tf
    hb = tm // SUBLANE
    return pl.pallas_call(
        _convffn_kernel,
        out_shape=jax.ShapeDtypeStruct((b, s, d), F32),
        grid=(b, s // tm, nf),
        in_specs=[
            pl.BlockSpec((None, tm, d), lambda bi, i, f: (bi, i, 0)),
            pl.BlockSpec((None, SUBLANE, d), lambda bi, i, f: (bi, jnp.maximum(i * hb - 1, 0), 0)),
            pl.BlockSpec((1, d), lambda bi, i, f: (0, 0)),
            pl.BlockSpec((None, d, tf), lambda bi, i, f: (layer, 0, f)),
            pl.BlockSpec((None, d, tf), lambda bi, i, f: (layer, 0, nf + f)),
            pl.BlockSpec((FFN_CONV, tf), lambda bi, i, f: (0, f)),
            pl.BlockSpec((None, tf, d), lambda bi, i, f: (layer, f, 0)),
            pl.BlockSpec((1, d), lambda bi, i, f: (0, 0)),
        ],
        out_specs=pl.BlockSpec((None, tm, d), lambda bi, i, f: (bi, i, 0)),
        scratch_shapes=[
            pltpu.VMEM((tm, d), BF16),
            pltpu.VMEM((SUBLANE, d), BF16),
            pltpu.VMEM((tm + SUBLANE, tf), F32),
            pltpu.VMEM((tm, d), F32),
        ],
        compiler_params=_cparams(("parallel", "parallel", "arbitrary")),
        name="convffn",
    )(x3, x3, g_pre, w_up, w_up, conv_w_t, w_down, g_post)


def _split_w_in(w_in):
    sizes = (DN_HEADS * DN_DK, DN_HEADS * DN_DK, DN_HEADS * DN_DV, DN_HEADS * DN_DV, DN_HEADS,
             DN_HEADS, GLA_HEADS * GLA_DK, GLA_HEADS * GLA_DK, GLA_HEADS * GLA_DV, GLA_RANK,
             GLA_HEADS * GLA_DV, S5_WIDTH, 3 * D_MODEL)
    parts = []
    acc = 0
    for n in sizes:
        parts.append(w_in[..., acc:acc + n])
        acc += n
    (dq, dk, dv, dz, db, da, gq, gk, gv, glow, gr, su, gates) = parts
    big = jnp.concatenate([dq, dk, dv, dz, gq, gk, gv, gr, su, gates], axis=-1).astype(BF16)
    pad = jnp.zeros(w_in.shape[:-1] + (LANE - 2 * DN_HEADS - GLA_RANK,), w_in.dtype)
    per_group = []
    for g in range(DN_HEADS // DN_HG):
        per_group += [db[..., g * DN_HG:(g + 1) * DN_HG], da[..., g * DN_HG:(g + 1) * DN_HG]]
    small = jnp.concatenate(per_group + [glow, pad], axis=-1).astype(BF16)
    return big, small


def kernel(x, mem, norm_mix_pre, norm_mix_post, w_in, dn_conv_w, dn_a_log, dn_dt_bias, dn_norm_w, w_dn_out, gla_w_up, gla_b_up, gla_norm_w, w_gla_out, s5_lam_re, s5_lam_im, s5_log_step, s5_b_re, s5_b_im, s5_c_re, s5_c_im, s5_d, w_s5_glu, w_mix_out, norm_xa_pre, norm_xa_post, norm_mem, w_xa_q, w_xa_kv, w_xa_out, norm_ffn_pre, norm_ffn_post, w_ffn_up, ffn_conv_w, w_ffn_down):
    b, s, d = x.shape
    t = b * s
    depth = w_in.shape[0]
    mem2 = mem.reshape(-1, d)
    row = lambda a: a.reshape(1, -1).astype(F32)
    s5_mats = jax.vmap(_s5_matrices)(s5_lam_re, s5_lam_im, s5_log_step, s5_b_re, s5_b_im,
                                     s5_c_re, s5_c_im, s5_d)

    w_big, w_small = _split_w_in(w_in)
    w_dn_out, w_gla_out, w_s5_glu, w_mix_out, w_xa_q, w_xa_kv, w_xa_out, w_ffn_up, w_ffn_down = (
        w.astype(BF16) for w in (w_dn_out, w_gla_out, w_s5_glu, w_mix_out, w_xa_q, w_xa_kv, w_xa_out,
                                 w_ffn_up, w_ffn_down))

    for l in range(depth):
        z2, zs2, su2 = _in_proj(x.reshape(t, d), row(norm_mix_pre[l]), w_big, w_small, l)
        z3 = z2.reshape(b, s, -1)
        zs3 = zs2.reshape(b, s, -1)

        oa = _deltanet(z3, zs3, dn_a_log[l], dn_dt_bias[l], dn_conv_w[l].T.astype(F32),
                       row(dn_norm_w[l]))

        w_up_pad = jnp.zeros((zs2.shape[1], GLA_HEADS * GLA_DK), F32)
        w_up_pad = w_up_pad.at[_SM_GLOW:_SM_GLOW + GLA_RANK].set(gla_w_up[l]).astype(BF16)
        ob = _gla(z3, zs3, w_up_pad, row(gla_b_up[l]), row(gla_norm_w[l]))

        ys = _s5(su2, s5_mats, l, nbatch=b)

        x = _mix_out(oa.reshape(t, -1), ob.reshape(t, -1), ys, z2, x.reshape(t, d),
                     w_dn_out, w_gla_out, w_s5_glu, w_mix_out, row(norm_mix_post[l]), l).reshape(b, s, d)

        kv = _norm_proj(mem2, row(norm_mem[l]), w_xa_kv, l)
        x = _xattn(x, row(norm_xa_pre[l]), w_xa_q, kv.reshape(b, -1, kv.shape[-1]), w_xa_out,
                   row(norm_xa_post[l]), l)

        x = _convffn(x, row(norm_ffn_pre[l]), w_ffn_up, ffn_conv_w[l].T.astype(F32), w_ffn_down,
                     row(norm_ffn_post[l]), l)
    return x
```

```python
import jax
import jax.numpy as jnp
from jax import lax
from jax.experimental import pallas as pl
from jax.experimental.pallas import tpu as pltpu

F32 = jnp.float32
BF16 = jnp.bfloat16

NORM_EPS = 1e-6

D_MODEL = 2048
DN_HEADS, DN_DK, DN_DV, DN_CONV = 8, 128, 128, 4
GLA_HEADS, GLA_DK, GLA_DV, GLA_RANK, GLA_TAU = 4, 128, 256, 16, 16.0
S5_GROUP, S5_GROUPS, S5_STATE = 16, 64, 64
S5_WIDTH = S5_GROUP * S5_GROUPS
XA_HEADS, XA_DH = 4, 128
FFN_HIDDEN, FFN_CONV = 5632, 3

CHUNK = 64
GLA_SUB = 16
S5_L = 16
DN_HG = 4
DN_BASE = 8
LANE = 128
S5_LB = LANE // S5_GROUP
SUBLANE = 8
VMEM_LIMIT = 56 * 1024 * 1024

_OFF_DQ, _OFF_DK, _OFF_DV, _OFF_DZ = 0, 1024, 2048, 3072
_OFF_GQ, _OFF_GK, _OFF_GV, _OFF_GR = 4096, 4608, 5120, 6144
_OFF_SU, _OFF_GATES = 7168, 8192
_BIG_WIDTH = 14336
_SM_GLOW = 16


def _cparams(sem):
    return pltpu.CompilerParams(dimension_semantics=sem, vmem_limit_bytes=VMEM_LIMIT)


def _rms(xf, g):
    return xf * lax.rsqrt(jnp.mean(xf * xf, axis=-1, keepdims=True) + NORM_EPS) * g


def _dot(a, b):
    return jnp.dot(a.astype(BF16), b.astype(BF16), preferred_element_type=F32)


def _dot_nt(a, b):
    return lax.dot_general(a.astype(BF16), b.astype(BF16), (((1,), (1,)), ((), ())),
                           preferred_element_type=F32)


def _dot_tn(a, b):
    return lax.dot_general(a.astype(BF16), b.astype(BF16), (((0,), (0,)), ((), ())),
                           preferred_element_type=F32)


def _sigmoid(x):
    return 0.5 * jnp.tanh(0.5 * x) + 0.5


def _silu(x):
    return x * _sigmoid(x)


def _softplus(x):
    return jnp.maximum(x, 0.0) + jnp.log(1.0 + jnp.exp(-jnp.abs(x)))


def _gelu_tanh(x):
    c = 0.7978845608028654
    return 0.5 * x * (1.0 + jnp.tanh(c * (x + 0.044715 * (x * x * x))))


def _in_proj_kernel(x_ref, g_ref, w_ref, ws_ref, o_ref, os_ref, su_ref, hn_ref):
    j = pl.program_id(1)
    tn = o_ref.shape[1]

    @pl.when(j == 0)
    def _():
        hn = _rms(x_ref[...], g_ref[...]).astype(BF16)
        hn_ref[...] = hn
        os_ref[...] = jnp.dot(hn, ws_ref[...], preferred_element_type=F32)

    acc = jnp.dot(hn_ref[...], w_ref[...], preferred_element_type=F32)
    o_ref[...] = acc.astype(o_ref.dtype)

    @pl.when(jnp.logical_and(j >= _OFF_SU // tn, j < (_OFF_SU + S5_WIDTH) // tn))
    def _():
        su_ref[...] = acc


def _in_proj(x2, g, w_big, w_small, layer, tm=1024, tn=1024):
    t, d = x2.shape
    n = w_big.shape[2]
    ns = w_small.shape[2]
    j0, nsu = _OFF_SU // tn, S5_WIDTH // tn
    return pl.pallas_call(
        _in_proj_kernel,
        out_shape=(jax.ShapeDtypeStruct((t, n), BF16), jax.ShapeDtypeStruct((t, ns), F32),
                   jax.ShapeDtypeStruct((t, S5_WIDTH), F32)),
        grid=(t // tm, n // tn),
        in_specs=[
            pl.BlockSpec((tm, d), lambda i, j: (i, 0)),
            pl.BlockSpec((1, d), lambda i, j: (0, 0)),
            pl.BlockSpec((None, d, tn), lambda i, j: (layer, 0, j)),
            pl.BlockSpec((None, d, ns), lambda i, j: (layer, 0, 0)),
        ],
        out_specs=(
            pl.BlockSpec((tm, tn), lambda i, j: (i, j)),
            pl.BlockSpec((tm, ns), lambda i, j: (i, 0)),
            pl.BlockSpec((tm, tn), lambda i, j: (i, jnp.clip(j - j0, 0, nsu - 1))),
        ),
        scratch_shapes=[pltpu.VMEM((tm, d), BF16)],
        compiler_params=_cparams(("parallel", "arbitrary")),
        name="in_proj",
    )(x2, g, w_big, w_small)


def _norm_proj_kernel(x_ref, g_ref, w_ref, o_ref):
    hn = _rms(x_ref[...], g_ref[...]).astype(BF16)
    o_ref[...] = jnp.dot(hn, w_ref[...], preferred_element_type=F32).astype(o_ref.dtype)


def _norm_proj(x2, g, w, layer, tm=512):
    t, d = x2.shape
    n = w.shape[2]
    return pl.pallas_call(
        _norm_proj_kernel,
        out_shape=jax.ShapeDtypeStruct((t, n), BF16),
        grid=(t // tm,),
        in_specs=[
            pl.BlockSpec((tm, d), lambda i: (i, 0)),
            pl.BlockSpec((1, d), lambda i: (0, 0)),
            pl.BlockSpec((None, d, n), lambda i: (layer, 0, 0)),
        ],
        out_specs=pl.BlockSpec((tm, n), lambda i: (i, 0)),
        compiler_params=_cparams(("parallel",)),
        name="mem_kv_proj",
    )(x2, g, w)


def _dot_hilo_rhs(a01, b):
    hi = b.astype(BF16)
    lo = (b - hi.astype(F32)).astype(BF16)
    return (jnp.dot(a01, hi, preferred_element_type=F32) + jnp.dot(a01, lo, preferred_element_type=F32))


def _dot3(a, b):
    ah = a.astype(BF16)
    al = (a - ah.astype(F32)).astype(BF16)
    bh = b.astype(BF16)
    bl = (b - bh.astype(F32)).astype(BF16)
    return (jnp.dot(ah, bh, preferred_element_type=F32)
            + (jnp.dot(ah, bl, preferred_element_type=F32) + jnp.dot(al, bh, preferred_element_type=F32)))


def _dot_hilo_lhs(a, b01):
    hi = a.astype(BF16)
    lo = (a - hi.astype(F32)).astype(BF16)
    return (jnp.dot(hi, b01, preferred_element_type=F32) + jnp.dot(lo, b01, preferred_element_type=F32))


def _deltanet_kernel(q_ref, k_ref, v_ref, zg_ref, zs_ref, alr_ref, dtr_ref, alc_ref,
                     dtc_ref, cwq_ref, cwk_ref, cwv_ref, nw_ref, o_ref, state_ref, tail_ref, ext_ref):
    tt = q_ref.shape[0]
    c = CHUNK
    hg = DN_HG
    dk, dv = DN_DK, DN_DV
    halo = SUBLANE
    nchunk = tt // c

    @pl.when(pl.program_id(2) == 0)
    def _():
        state_ref[...] = jnp.zeros_like(state_ref)
        tail_ref[...] = jnp.zeros_like(tail_ref)

    def conv_silu(idx, raw_ref, cw_ref):
        raw = raw_ref[...].astype(F32)
        ext_ref[idx, 0:halo, :] = tail_ref[idx]
        ext_ref[idx, halo:halo + tt, :] = raw
        tail_ref[idx] = raw[tt - halo:tt, :]
        cw = cw_ref[...]
        base = halo - (DN_CONV - 1)
        y = ext_ref[idx, base:base + tt, :] * cw[0:1, :]
        for j in range(1, DN_CONV):
            y = y + ext_ref[idx, base + j:base + j + tt, :] * cw[j:j + 1, :]
        return _silu(y)

    q_all = conv_silu(0, q_ref, cwq_ref)
    k_all = conv_silu(1, k_ref, cwk_ref)
    v_all = conv_silu(2, v_ref, cwv_ref)

    grp = pl.program_id(1)
    gcol = pltpu.roll(zs_ref[...], (LANE - 2 * hg * grp) % LANE, 1)
    grow = gcol.T[0:SUBLANE, :]
    g_rows = -jnp.exp(alr_ref[...]) * _softplus(grow + dtr_ref[...])
    beta_cols = _sigmoid(gcol)
    g_cols = -jnp.exp(alc_ref[...]) * _softplus(gcol + dtc_ref[...])

    ri = lax.broadcasted_iota(jnp.int32, (c, c), 0)
    ci = lax.broadcasted_iota(jnp.int32, (c, c), 1)
    causal = ri >= ci
    strict = ri > ci
    ident = jnp.where(ri == ci, 1.0, 0.0).astype(F32)
    tril01 = jnp.where(causal, 1.0, 0.0).astype(BF16)
    triu01 = jnp.where(ri <= ci, 1.0, 0.0).astype(BF16)

    items = [(hh, n) for n in range(nchunk) for hh in range(hg)]
    qs, lower, a_qk, e_col, k_dec, g_last, vb = {}, {}, {}, {}, {}, {}, {}
    for n in range(nchunk):
        sl = slice(n * c, (n + 1) * c)
        gcs_c = _dot_hilo_rhs(tril01, g_cols[sl])
        gcs_r = _dot_hilo_lhs(g_rows[:, sl], triu01)
        for hh in range(hg):
            it = (hh, n)
            hs = slice(hh * dk, (hh + 1) * dk)
            qc, kc = q_all[sl, hs], k_all[sl, hs]
            qc = qc * (lax.rsqrt(jnp.sum(qc * qc, axis=-1, keepdims=True) + NORM_EPS) * (dk ** -0.5))
            kc = kc * lax.rsqrt(jnp.sum(kc * kc, axis=-1, keepdims=True) + NORM_EPS)
            beta_col = beta_cols[sl, hh:hh + 1]
            gc_col = gcs_c[:, hg + hh:hg + hh + 1]
            gc_row = gcs_r[hg + hh:hg + hh + 1, :]
            gc_end = gcs_r[hg + hh:hg + hh + 1, c - 1:c]
            decay = jnp.where(causal, jnp.exp(jnp.where(causal, gc_col - gc_row, 0.0)), 0.0)
            kb = kc * beta_col
            res = _dot_nt(jnp.concatenate([kb, qc], axis=0), kc)
            lower[it] = jnp.where(strict, res[0:c] * decay, 0.0)
            a_qk[it] = jnp.where(causal, res[c:2 * c] * decay, 0.0)
            e_col[it] = jnp.exp(gc_col)
            k_dec[it] = kc * jnp.exp(gc_end - gc_col)
            g_last[it] = jnp.exp(gc_end)
            vb[it] = jnp.concatenate([v_all[sl, hh * dv:(hh + 1) * dv] * beta_col, kb * e_col[it]], axis=1)
            qs[it] = qc

    base_shift = DN_BASE.bit_length() - 1
    same_base = (ri >> base_shift) == (ci >> base_shift)
    inv, pw = {}, {}
    for it in items:
        lb = jnp.where(same_base, lower[it], 0.0)
        inv[it] = ident - lb
        pw[it] = _dot3(lb, lb)
    span = 2
    while span < DN_BASE:
        for it in items:
            if span * 2 < DN_BASE:
                res = _dot3(jnp.concatenate([inv[it], pw[it]], axis=0), pw[it])
                inv[it] = inv[it] + res[0:c]
                pw[it] = res[c:2 * c]
            else:
                inv[it] = inv[it] + _dot3(inv[it], pw[it])
        span *= 2
    width = 2 * DN_BASE
    while width <= c:
        sh = width.bit_length() - 1
        same_w = (ri >> sh) == (ci >> sh)
        same_half = (ri >> (sh - 1)) == (ci >> (sh - 1))
        off = {it: jnp.where(same_w, jnp.where(same_half, 0.0, lower[it]), 0.0) for it in items}
        mid = {it: _dot3(off[it], inv[it]) for it in items}
        for it in items:
            inv[it] = inv[it] - _dot3(inv[it], mid[it])
        width *= 2

    uw = {it: _dot3(inv[it], vb[it]) for it in items}
    a_uw = {it: _dot(a_qk[it], uw[it]) for it in items}
    k_uw = {it: _dot_tn(k_dec[it], uw[it]) for it in items}

    nw = nw_ref[...]
    state = [state_ref[hh] for hh in range(hg)]
    for n in range(nchunk):
        sl = slice(n * c, (n + 1) * c)
        for hh in range(hg):
            it = (hh, n)
            q_eff = qs[it] * e_col[it] - a_uw[it][:, dv:dv + dk]
            res = _dot(jnp.concatenate([q_eff, k_uw[it][:, dv:dv + dk]], axis=0), state[hh])
            o = res[0:c] + a_uw[it][:, 0:dv]
            state[hh] = state[hh] * g_last[it] - res[c:c + dk] + k_uw[it][:, 0:dv]
            hs = slice(hh * dv, (hh + 1) * dv)
            o = _rms(o, nw) * _silu(zg_ref[sl, hs].astype(F32))
            o_ref[sl, hs] = o.astype(o_ref.dtype)
    for hh in range(hg):
        state_ref[hh] = state[hh]


def _deltanet(z3, zs3, dn_a_log, dn_dt_bias, conv_w_t, norm_w, tt=256):
    b, s, _ = z3.shape
    nh, hg = DN_HEADS, DN_HG
    ngrp = nh // hg
    wq = hg * DN_DK
    bq, bk, bv, bz = _OFF_DQ // wq, _OFF_DK // wq, _OFF_DV // wq, _OFF_DZ // wq
    al = dn_a_log.astype(F32).reshape(ngrp, hg)
    dtb = dn_dt_bias.astype(F32).reshape(ngrp, hg)
    alr = jnp.pad(al, ((0, 0), (hg, SUBLANE - 2 * hg)))[:, :, None]
    dtr = jnp.pad(dtb, ((0, 0), (hg, SUBLANE - 2 * hg)))[:, :, None]
    alc = jnp.pad(al, ((0, 0), (hg, LANE - 2 * hg)))[:, None, :]
    dtc = jnp.pad(dtb, ((0, 0), (hg, LANE - 2 * hg)))[:, None, :]
    return pl.pallas_call(
        _deltanet_kernel,
        out_shape=jax.ShapeDtypeStruct((b, s, nh * DN_DV), BF16),
        grid=(b, ngrp, s // tt),
        in_specs=[
            pl.BlockSpec((None, tt, wq), lambda bi, g, i: (bi, i, bq + g)),
            pl.BlockSpec((None, tt, wq), lambda bi, g, i: (bi, i, bk + g)),
            pl.BlockSpec((None, tt, wq), lambda bi, g, i: (bi, i, bv + g)),
            pl.BlockSpec((None, tt, wq), lambda bi, g, i: (bi, i, bz + g)),
            pl.BlockSpec((None, tt, LANE), lambda bi, g, i: (bi, i, 0)),
            pl.BlockSpec((None, SUBLANE, 1), lambda bi, g, i: (g, 0, 0)),
            pl.BlockSpec((None, SUBLANE, 1), lambda bi, g, i: (g, 0, 0)),
            pl.BlockSpec((None, 1, LANE), lambda bi, g, i: (g, 0, 0)),
            pl.BlockSpec((None, 1, LANE), lambda bi, g, i: (g, 0, 0)),
            pl.BlockSpec((DN_CONV, wq), lambda bi, g, i: (0, g)),
            pl.BlockSpec((DN_CONV, wq), lambda bi, g, i: (0, ngrp + g)),
            pl.BlockSpec((DN_CONV, wq), lambda bi, g, i: (0, 2 * ngrp + g)),
            pl.BlockSpec((1, DN_DV), lambda bi, g, i: (0, 0)),
        ],
        out_specs=pl.BlockSpec((None, tt, wq), lambda bi, g, i: (bi, i, g)),
        scratch_shapes=[
            pltpu.VMEM((hg, DN_DK, DN_DV), F32),
            pltpu.VMEM((3, SUBLANE, wq), F32),
            pltpu.VMEM((3, tt + SUBLANE, wq), F32),
        ],
        compiler_params=_cparams(("parallel", "parallel", "arbitrary")),
        name="deltanet",
    )(z3, z3, z3, z3, zs3, alr, dtr, alc, dtc, conv_w_t, conv_w_t, conv_w_t, norm_w)


def _gla_kernel(q_ref, k_ref, v_ref, r_ref, zs_ref, wup_ref, bup_ref, nw_ref, o_ref, state_ref):
    tt = q_ref.shape[0]
    c = CHUNK
    sub = GLA_SUB

    @pl.when(pl.program_id(2) == 0)
    def _():
        state_ref[...] = jnp.zeros_like(state_ref)

    logit = _dot(zs_ref[...], wup_ref[...]) + bup_ref[...]
    log_a = -_softplus(-logit) * (1.0 / GLA_TAU)

    ri = lax.broadcasted_iota(jnp.int32, (c, c), 0)
    ci = lax.broadcasted_iota(jnp.int32, (c, c), 1)
    tril = jnp.where(ri >= ci, 1.0, 0.0).astype(BF16)
    sub_rows = lax.broadcasted_iota(jnp.int32, (sub, 1), 0)
    key_rows = lax.broadcasted_iota(jnp.int32, (c, 1), 0)
    blk_cols = lax.broadcasted_iota(jnp.int32, (sub, c), 1)

    nw = nw_ref[...]
    scale = GLA_DK ** -0.5
    nchunk = tt // c
    nblk = c // sub
    chunks = range(nchunk)
    bcum = [_dot_hilo_rhs(tril, log_a[n * c:(n + 1) * c]) for n in chunks]
    qc = [q_ref[n * c:(n + 1) * c, :].astype(F32) * scale for n in chunks]
    kc = [k_ref[n * c:(n + 1) * c, :].astype(F32) for n in chunks]
    vc = [v_ref[n * c:(n + 1) * c, :] for n in chunks]
    b_last = [bcum[n][c - 1:c, :] for n in chunks]
    q_dec = [qc[n] * jnp.exp(bcum[n]) for n in chunks]
    k_dec = [kc[n] * jnp.exp(b_last[n] - bcum[n]) for n in chunks]
    kv = [_dot_tn(vc[n], k_dec[n]) for n in chunks]

    off = {}
    for n in chunks:
        for bi in range(1, nblk):
            r0 = bi * sub
            b_ref = bcum[n][r0 - 1:r0, :]
            early = key_rows < r0
            q_f = qc[n][r0:r0 + sub] * jnp.exp(bcum[n][r0:r0 + sub] - b_ref)
            k_f = jnp.where(early, kc[n] * jnp.exp(jnp.where(early, b_ref - bcum[n], 0.0)), 0.0)
            off[n, bi] = _dot_nt(q_f, k_f)

    a_qk = []
    for n in chunks:
        row_blocks = []
        for bi in range(nblk):
            r0 = bi * sub
            q_i = qc[n][r0:r0 + sub]
            b_i = bcum[n][r0:r0 + sub]
            blk = off[n, bi] if bi > 0 else jnp.zeros((sub, c), F32)
            for j in range(sub):
                kj = kc[n][r0 + j:r0 + j + 1, :]
                bj = bcum[n][r0 + j:r0 + j + 1, :]
                live = sub_rows >= j
                dec = jnp.exp(jnp.where(live, b_i - bj, 0.0))
                col = jnp.sum(jnp.where(live, q_i * kj * dec, 0.0), axis=1, keepdims=True)
                blk = jnp.where(blk_cols == r0 + j, col, blk)
            row_blocks.append(blk)
        a_qk.append(jnp.concatenate(row_blocks, axis=0))

    intra = [_dot(a_qk[n], vc[n]) for n in chunks]
    state_t = state_ref[...]
    for n in chunks:
        sl = slice(n * c, (n + 1) * c)
        o = _dot_nt(q_dec[n], state_t) + intra[n]
        state_t = state_t * jnp.exp(b_last[n]) + kv[n]
        o = _rms(o, nw) * _silu(r_ref[sl, :].astype(F32))
        o_ref[sl, :] = o.astype(o_ref.dtype)
    state_ref[...] = state_t


def _gla(z3, zs3, w_up_pad, b_up, norm_w, tt=256):
    b, s, _ = z3.shape
    hq, hk = _OFF_GQ // GLA_DK, _OFF_GK // GLA_DK
    hv, hr = _OFF_GV // GLA_DV, _OFF_GR // GLA_DV
    ns = zs3.shape[-1]
    return pl.pallas_call(
        _gla_kernel,
        out_shape=jax.ShapeDtypeStruct((b, s, GLA_HEADS * GLA_DV), BF16),
        grid=(b, GLA_HEADS, s // tt),
        in_specs=[
            pl.BlockSpec((None, tt, GLA_DK), lambda bi, h, i: (bi, i, hq + h)),
            pl.BlockSpec((None, tt, GLA_DK), lambda bi, h, i: (bi, i, hk + h)),
            pl.BlockSpec((None, tt, GLA_DV), lambda bi, h, i: (bi, i, hv + h)),
            pl.BlockSpec((None, tt, GLA_DV), lambda bi, h, i: (bi, i, hr + h)),
            pl.BlockSpec((None, tt, ns), lambda bi, h, i: (bi, i, 0)),
            pl.BlockSpec((ns, GLA_DK), lambda bi, h, i: (0, h)),
            pl.BlockSpec((1, GLA_DK), lambda bi, h, i: (0, h)),
            pl.BlockSpec((1, GLA_DV), lambda bi, h, i: (0, 0)),
        ],
        out_specs=pl.BlockSpec((None, tt, GLA_DV), lambda bi, h, i: (bi, i, h)),
        scratch_shapes=[pltpu.VMEM((GLA_DV, GLA_DK), F32)],
        compiler_params=_cparams(("parallel", "parallel", "arbitrary")),
        name="gla",
    )(z3, z3, z3, z3, zs3, w_up_pad, b_up, norm_w)


def _s5_kernel(su_ref, kblk_ref, vre_ref, vim_ref, wre_ref, wim_ref, al_ref, d_ref,
               o_ref, m_ref, v_ref, w_ref, e_ref, xp_ref):
    ll = S5_L
    rows = e_ref.shape[0]
    nst = S5_LB * S5_STATE

    @pl.when(pl.program_id(1) == 0)
    def _():
        zero_blk = jnp.zeros((LANE, LANE), BF16)
        for s in range(ll):
            for i in range(ll):
                m_ref[s * LANE:(s + 1) * LANE, i * LANE:(i + 1) * LANE] = (
                    kblk_ref[i - s] if i >= s else zero_blk)
        rblk = 2 * LANE
        vr = lax.broadcasted_iota(jnp.int32, (rblk, nst), 0)
        vc = lax.broadcasted_iota(jnp.int32, (rblk, nst), 1)
        keep_v = ((vr & (LANE - 1)) >> 4) == (vc >> 6)
        for part, vt_ref in enumerate((vre_ref, vim_ref)):
            vt = vt_ref[...]
            v2 = jnp.concatenate([vt, vt], axis=0).T
            for r in range(ll * LANE // rblk):
                rs = slice(r * rblk, (r + 1) * rblk)
                v_ref[rs, part * nst:(part + 1) * nst] = jnp.where(
                    keep_v, jnp.tile(v2[rs, :], (1, nst // LANE)), 0.0).astype(BF16)
        wr = lax.broadcasted_iota(jnp.int32, (nst, rblk), 0)
        wc = lax.broadcasted_iota(jnp.int32, (nst, rblk), 1)
        keep_w = (wr >> 6) == ((wc & (LANE - 1)) >> 4)
        for r in range(ll * LANE // rblk):
            cs = slice(r * rblk, (r + 1) * rblk)
            w_ref[0:nst, cs] = jnp.where(keep_w, jnp.tile(wre_ref[:, cs], (S5_LB, 1)), 0.0).astype(BF16)
            w_ref[nst:2 * nst, cs] = jnp.where(keep_w, jnp.tile(wim_ref[:, cs], (S5_LB, 1)), 0.0).astype(BF16)

    u_list = [su_ref[pl.ds(s, rows, stride=ll), :] for s in range(ll)]
    u_cat = jnp.concatenate([u.astype(BF16) for u in u_list], axis=1)
    e_ref[...] = jnp.dot(u_cat, v_ref[...], preferred_element_type=F32)
    ar = al_ref[:, 0:nst]
    ai = al_ref[:, nst:2 * nst]

    def step(cidx, carry):
        xr, xi = carry
        row = pl.ds(cidx, 1)
        xp_ref[row, 0:nst] = xr
        xp_ref[row, nst:2 * nst] = xi
        er = e_ref[row, 0:nst]
        ei = e_ref[row, nst:2 * nst]
        return ar * xr - ai * xi + er, ar * xi + ai * xr + ei

    zero = jnp.zeros((1, nst), F32)
    lax.fori_loop(0, rows, step, (zero, zero), unroll=8)

    y = (jnp.dot(u_cat, m_ref[...], preferred_element_type=F32)
         + jnp.dot(xp_ref[...].astype(BF16), w_ref[...], preferred_element_type=F32))
    d = d_ref[...]
    for i in range(ll):
        yi = y[:, i * LANE:(i + 1) * LANE] + d * u_list[i]
        o_ref[pl.ds(i, rows, stride=ll), :] = _gelu_tanh(yi)


def _s5(su2, mats, layer, nbatch):
    kblk, vre, vim, wre, wim, al, dskip = mats
    t, width = su2.shape
    s = t // nbatch
    ll = S5_L
    rows = s // ll
    nblk = width // LANE
    nst2 = 2 * S5_LB * S5_STATE

    return pl.pallas_call(
        _s5_kernel,
        out_shape=jax.ShapeDtypeStruct((t, width), F32),
        grid=(nblk, nbatch),
        in_specs=[
            pl.BlockSpec((s, LANE), lambda j, b: (b, j)),
            pl.BlockSpec((None, None, ll, LANE, LANE), lambda j, b: (layer, j, 0, 0, 0)),
            pl.BlockSpec((None, None, S5_STATE, ll * LANE), lambda j, b: (layer, j, 0, 0)),
            pl.BlockSpec((None, None, S5_STATE, ll * LANE), lambda j, b: (layer, j, 0, 0)),
            pl.BlockSpec((None, None, S5_STATE, ll * LANE), lambda j, b: (layer, j, 0, 0)),
            pl.BlockSpec((None, None, S5_STATE, ll * LANE), lambda j, b: (layer, j, 0, 0)),
            pl.BlockSpec((None, None, 1, nst2), lambda j, b: (layer, j, 0, 0)),
            pl.BlockSpec((None, None, 1, LANE), lambda j, b: (layer, j, 0, 0)),
        ],
        out_specs=pl.BlockSpec((s, LANE), lambda j, b: (b, j)),
        scratch_shapes=[
            pltpu.VMEM((ll * LANE, ll * LANE), BF16),
            pltpu.VMEM((ll * LANE, nst2), BF16),
            pltpu.VMEM((nst2, ll * LANE), BF16),
            pltpu.VMEM((rows, nst2), F32),
            pltpu.VMEM((rows, nst2), F32),
        ],
        compiler_params=_cparams(("parallel", "arbitrary")),
        name="s5",
    )(su2, kblk, vre, vim, wre, wim, al, dskip)


def _s5_matrices(lam_re, lam_im, log_step, b_re, b_im, c_re, c_im, d_skip):
    g, p, hh, ll, lb = S5_GROUPS, S5_STATE, S5_GROUP, S5_L, S5_LB
    nb = g // lb
    lr = jnp.minimum(lam_re.astype(F32), -1e-4)
    li = lam_im.astype(F32)
    dt = jnp.exp(log_step.astype(F32))[:, None]
    mag = jnp.exp(lr * dt)
    ar = mag * jnp.cos(li * dt)
    ai = mag * jnp.sin(li * dt)
    nr = ar - 1.0
    den = lr * lr + li * li
    zr = (nr * lr + ai * li) / den
    zi = (ai * lr - nr * li) / den
    br = b_re.astype(F32)
    bi = b_im.astype(F32)
    bbr = zr[..., None] * br - zi[..., None] * bi
    bbi = zr[..., None] * bi + zi[..., None] * br
    pr = [jnp.ones_like(ar)]
    pi = [jnp.zeros_like(ai)]
    for _ in range(ll):
        pr.append(pr[-1] * ar - pi[-1] * ai)
        pi.append(pr[-2] * ai + pi[-1] * ar)
    pr_rev = jnp.stack(pr[ll - 1::-1], axis=0)
    pi_rev = jnp.stack(pi[ll - 1::-1], axis=0)
    pr = jnp.stack(pr, axis=0)
    pi = jnp.stack(pi, axis=0)
    cr = c_re.astype(F32)
    cim = c_im.astype(F32)

    def blk(a, *shape):
        return a.reshape((nb, lb) + shape)

    pr_d = jnp.moveaxis(pr.reshape(ll + 1, nb, lb, p), 0, 1)[:, :, :, None, None, :]
    pi_d = jnp.moveaxis(pi.reshape(ll + 1, nb, lb, p), 0, 1)[:, :, :, None, None, :]
    cr_b = blk(cr, hh, p)[:, None, :, None, :, :]
    ci_b = blk(cim, hh, p)[:, None, :, None, :, :]
    qr = cr_b * pr_d - ci_b * pi_d
    qi = cr_b * pi_d + ci_b * pr_d
    bbr_b = blk(jnp.swapaxes(bbr, 1, 2), hh, p)[:, None, :, :, None, :]
    bbi_b = blk(jnp.swapaxes(bbi, 1, 2), hh, p)[:, None, :, :, None, :]
    hi = lax.Precision.HIGHEST
    kd = (jnp.einsum('ndgop,ngip->ndgio', qr[:, :ll, :, 0], bbr_b[:, 0, :, :, 0], precision=hi)
          - jnp.einsum('ndgop,ngip->ndgio', qi[:, :ll, :, 0], bbi_b[:, 0, :, :, 0], precision=hi))
    eye = jnp.eye(lb, dtype=F32)[None, None, :, None, :, None]
    kblk = (kd[:, :, :, :, None, :] * eye).reshape(nb, ll, lb * hh, lb * hh)
    prs = jnp.moveaxis(pr_rev.reshape(ll, nb, lb, p), 0, 1)[:, :, :, None, :]
    pis = jnp.moveaxis(pi_rev.reshape(ll, nb, lb, p), 0, 1)[:, :, :, None, :]
    bbr_v, bbi_v = bbr_b[:, :, :, :, 0, :], bbi_b[:, :, :, :, 0, :]
    vre = jnp.moveaxis(prs * bbr_v - pis * bbi_v, -1, 1).reshape(nb, p, ll * lb * hh)
    vim = jnp.moveaxis(prs * bbi_v + pis * bbr_v, -1, 1).reshape(nb, p, ll * lb * hh)
    wre = jnp.moveaxis(qr[:, 1:ll + 1, :, 0], -1, 1).reshape(nb, p, ll * lb * hh)
    wim = -jnp.moveaxis(qi[:, 1:ll + 1, :, 0], -1, 1).reshape(nb, p, ll * lb * hh)

    al = jnp.concatenate([pr[ll].reshape(nb, 1, lb * p), pi[ll].reshape(nb, 1, lb * p)], axis=2)
    dskip = d_skip.astype(F32).reshape(nb, 1, lb * hh)
    return kblk.astype(BF16), vre, vim, wre, wim, al, dskip


def _mix_out_kernel(oa_ref, ob_ref, ys_ref, ga_ref, gb_ref, gc_ref, x_ref,
                    wa_ref, wb_ref, wca_ref, wcb_ref, wm_ref, gpost_ref, o_ref, acc_ref):
    n = pl.program_id(1)

    @pl.when(n == 0)
    def _():
        acc_ref[...] = jnp.zeros_like(acc_ref)

    oa, ob = oa_ref[...], ob_ref[...]
    ys = ys_ref[...].astype(BF16)
    tn = wa_ref.shape[1]
    halves = [slice(h * (tn // 2), (h + 1) * (tn // 2)) for h in range(2)]
    merged = []
    for hs in halves:
        y_a = jnp.dot(oa, wa_ref[:, hs], preferred_element_type=F32)
        y_b = jnp.dot(ob, wb_ref[:, hs], preferred_element_type=F32)
        glu_a = jnp.dot(ys, wca_ref[:, hs], preferred_element_type=F32)
        glu_b = jnp.dot(ys, wcb_ref[:, hs], preferred_element_type=F32)
        y_c = glu_a * _sigmoid(glu_b)
        merged.append((_sigmoid(ga_ref[:, hs].astype(F32)) * y_a + _sigmoid(gb_ref[:, hs].astype(F32)) * y_b
                       + _sigmoid(gc_ref[:, hs].astype(F32)) * y_c).astype(BF16))
    acc_ref[...] += (jnp.dot(merged[0], wm_ref[halves[0], :], preferred_element_type=F32)
                     + jnp.dot(merged[1], wm_ref[halves[1], :], preferred_element_type=F32))

    @pl.when(n == pl.num_programs(1) - 1)
    def _():
        o_ref[...] = x_ref[...] + _rms(acc_ref[...], gpost_ref[...])


def _mix_out(oa, ob, ys, z2, x2, w_dn_out, w_gla_out, w_s5_glu, w_mix_out, g_post, layer, tm=512, tn=512):
    t, d = x2.shape
    nsteps = d // tn
    gblk = _OFF_GATES // tn
    ka, kb, kc = oa.shape[1], ob.shape[1], ys.shape[1]
    return pl.pallas_call(
        _mix_out_kernel,
        out_shape=jax.ShapeDtypeStruct((t, d), F32),
        grid=(t // tm, nsteps),
        in_specs=[
            pl.BlockSpec((tm, ka), lambda i, n: (i, 0)),
            pl.BlockSpec((tm, kb), lambda i, n: (i, 0)),
            pl.BlockSpec((tm, kc), lambda i, n: (i, 0)),
            pl.BlockSpec((tm, tn), lambda i, n: (i, gblk + n)),
            pl.BlockSpec((tm, tn), lambda i, n: (i, gblk + nsteps + n)),
            pl.BlockSpec((tm, tn), lambda i, n: (i, gblk + 2 * nsteps + n)),
            pl.BlockSpec((tm, d), lambda i, n: (i, 0)),
            pl.BlockSpec((None, ka, tn), lambda i, n: (layer, 0, n)),
            pl.BlockSpec((None, kb, tn), lambda i, n: (layer, 0, n)),
            pl.BlockSpec((None, kc, tn), lambda i, n: (layer, 0, n)),
            pl.BlockSpec((None, kc, tn), lambda i, n: (layer, 0, nsteps + n)),
            pl.BlockSpec((None, tn, d), lambda i, n: (layer, n, 0)),
            pl.BlockSpec((1, d), lambda i, n: (0, 0)),
        ],
        out_specs=pl.BlockSpec((tm, d), lambda i, n: (i, 0)),
        scratch_shapes=[pltpu.VMEM((tm, d), F32)],
        compiler_params=_cparams(("parallel", "arbitrary")),
        name="mix_out",
    )(oa, ob, ys, z2, z2, z2, x2, w_dn_out, w_gla_out, w_s5_glu, w_s5_glu, w_mix_out, g_post)


def _xattn_kernel(x_ref, gpre_ref, wq_ref, kv_ref, wo_ref, gpost_ref, o_ref):
    x = x_ref[...]
    hn = _rms(x, gpre_ref[...]).astype(BF16)
    q = jnp.dot(hn, wq_ref[...], preferred_element_type=F32)
    kv = kv_ref[...]
    outs = []
    for h in range(XA_HEADS):
        qh = q[:, h * XA_DH:(h + 1) * XA_DH]
        kh = kv[:, h * XA_DH:(h + 1) * XA_DH]
        vh = kv[:, (XA_HEADS + h) * XA_DH:(XA_HEADS + h + 1) * XA_DH]
        s = _dot_nt(qh, kh) * (XA_DH ** -0.5)
        s = s - jnp.max(s, axis=-1, keepdims=True)
        p = jnp.exp(s)
        p = p / jnp.sum(p, axis=-1, keepdims=True)
        outs.append(_dot(p, vh))
    o = jnp.concatenate(outs, axis=1)
    y = jnp.dot(o.astype(BF16), wo_ref[...], preferred_element_type=F32)
    o_ref[...] = x + _rms(y, gpost_ref[...])


def _xattn(x3, g_pre, w_q, kv3, w_out, g_post, layer, tm=512):
    b, s, d = x3.shape
    m, nkv = kv3.shape[1], kv3.shape[2]
    nq = w_q.shape[2]
    return pl.pallas_call(
        _xattn_kernel,
        out_shape=jax.ShapeDtypeStruct((b, s, d), F32),
        grid=(b, s // tm),
        in_specs=[
            pl.BlockSpec((None, tm, d), lambda bi, i: (bi, i, 0)),
            pl.BlockSpec((1, d), lambda bi, i: (0, 0)),
            pl.BlockSpec((None, d, nq), lambda bi, i: (layer, 0, 0)),
            pl.BlockSpec((None, m, nkv), lambda bi, i: (bi, 0, 0)),
            pl.BlockSpec((None, nq, d), lambda bi, i: (layer, 0, 0)),
            pl.BlockSpec((1, d), lambda bi, i: (0, 0)),
        ],
        out_specs=pl.BlockSpec((None, tm, d), lambda bi, i: (bi, i, 0)),
        compiler_params=_cparams(("parallel", "parallel")),
        name="xattn",
    )(x3, g_pre, w_q, kv3, w_out, g_post)


def _convffn_kernel(x_ref, xh_ref, gpre_ref, wg_ref, wu_ref, cw_ref, wd_ref, gpost_ref,
                    o_ref, hn_ref, hh_ref, gate_ref, acc_ref):
    i = pl.program_id(1)
    f = pl.program_id(2)
    tm = x_ref.shape[0]
    halo = SUBLANE

    @pl.when(f == 0)
    def _():
        hn_ref[...] = _rms(x_ref[...], gpre_ref[...]).astype(BF16)
        keep = jnp.where(i > 0, 1.0, 0.0)
        hh_ref[...] = (_rms(xh_ref[...], gpre_ref[...]) * keep).astype(BF16)
        acc_ref[...] = jnp.zeros_like(acc_ref)

    hn = hn_ref[...]
    wg = wg_ref[...]
    gate_ref[0:halo, :] = jnp.dot(hh_ref[...], wg, preferred_element_type=F32)
    gate_ref[halo:halo + tm, :] = jnp.dot(hn, wg, preferred_element_type=F32)
    up = jnp.dot(hn, wu_ref[...], preferred_element_type=F32)
    cw = cw_ref[...]
    base = halo - (FFN_CONV - 1)
    conv = gate_ref[base:base + tm, :] * cw[0:1, :]
    for j in range(1, FFN_CONV):
        conv = conv + gate_ref[base + j:base + j + tm, :] * cw[j:j + 1, :]
    act = (_gelu_tanh(conv) * up).astype(BF16)
    acc_ref[...] += jnp.dot(act, wd_ref[...], preferred_element_type=F32)

    @pl.when(f == pl.num_programs(2) - 1)
    def _():
        o_ref[...] = x_ref[...] + _rms(acc_ref[...], gpost_ref[...])


def _convffn(x3, g_pre, w_up, conv_w_t, w_down, g_post, layer, tm=512, tf=512):
    b, s, d = x3.shape
    fh = w_down.shape[1]
    nf = fh // tf
    hb = tm // SUBLANE
    return pl.pallas_call(
        _convffn_kernel,
        out_shape=jax.ShapeDtypeStruct((b, s, d), F32),
        grid=(b, s // tm, nf),
        in_specs=[
            pl.BlockSpec((None, tm, d), lambda bi, i, f: (bi, i, 0)),
            pl.BlockSpec((None, SUBLANE, d), lambda bi, i, f: (bi, jnp.maximum(i * hb - 1, 0), 0)),
            pl.BlockSpec((1, d), lambda bi, i, f: (0, 0)),
            pl.BlockSpec((None, d, tf), lambda bi, i, f: (layer, 0, f)),
            pl.BlockSpec((None, d, tf), lambda bi, i, f: (layer, 0, nf + f)),
            pl.BlockSpec((FFN_CONV, tf), lambda bi, i, f: (0, f)),
            pl.BlockSpec((None, tf, d), lambda bi, i, f: (layer, f, 0)),
            pl.BlockSpec((1, d), lambda bi, i, f: (0, 0)),
        ],
        out_specs=pl.BlockSpec((None, tm, d), lambda bi, i, f: (bi, i, 0)),
        scratch_shapes=[
            pltpu.VMEM((tm, d), BF16),
            pltpu.VMEM((SUBLANE, d), BF16),
            pltpu.VMEM((tm + SUBLANE, tf), F32),
            pltpu.VMEM((tm, d), F32),
        ],
        compiler_params=_cparams(("parallel", "parallel", "arbitrary")),
        name="convffn",
    )(x3, x3, g_pre, w_up, w_up, conv_w_t, w_down, g_post)


def _split_w_in(w_in):
    sizes = (DN_HEADS * DN_DK, DN_HEADS * DN_DK, DN_HEADS * DN_DV, DN_HEADS * DN_DV, DN_HEADS,
             DN_HEADS, GLA_HEADS * GLA_DK, GLA_HEADS * GLA_DK, GLA_HEADS * GLA_DV, GLA_RANK,
             GLA_HEADS * GLA_DV, S5_WIDTH, 3 * D_MODEL)
    parts = []
    acc = 0
    for n in sizes:
        parts.append(w_in[..., acc:acc + n])
        acc += n
    (dq, dk, dv, dz, db, da, gq, gk, gv, glow, gr, su, gates) = parts
    o1 = sum(sizes[:4])
    o2, o3 = o1 + sizes[4] + sizes[5], o1 + sum(sizes[4:9])
    o4 = o3 + sizes[9]
    big = jnp.concatenate([w_in[..., :o1], w_in[..., o2:o3], w_in[..., o4:]], axis=-1).astype(BF16)
    pad = jnp.zeros(w_in.shape[:-1] + (LANE - 2 * DN_HEADS - GLA_RANK,), w_in.dtype)
    per_group = []
    for g in range(DN_HEADS // DN_HG):
        per_group += [db[..., g * DN_HG:(g + 1) * DN_HG], da[..., g * DN_HG:(g + 1) * DN_HG]]
    small = jnp.concatenate(per_group + [glow, pad], axis=-1).astype(BF16)
    return big, small


def kernel(x, mem, norm_mix_pre, norm_mix_post, w_in, dn_conv_w, dn_a_log, dn_dt_bias, dn_norm_w, w_dn_out, gla_w_up, gla_b_up, gla_norm_w, w_gla_out, s5_lam_re, s5_lam_im, s5_log_step, s5_b_re, s5_b_im, s5_c_re, s5_c_im, s5_d, w_s5_glu, w_mix_out, norm_xa_pre, norm_xa_post, norm_mem, w_xa_q, w_xa_kv, w_xa_out, norm_ffn_pre, norm_ffn_post, w_ffn_up, ffn_conv_w, w_ffn_down):
    b, s, d = x.shape
    t = b * s
    depth = w_in.shape[0]
    mem2 = mem.reshape(-1, d)
    row = lambda a: a.reshape(1, -1).astype(F32)
    s5_mats = jax.vmap(_s5_matrices)(s5_lam_re, s5_lam_im, s5_log_step, s5_b_re, s5_b_im,
                                     s5_c_re, s5_c_im, s5_d)

    w_big, w_small = _split_w_in(w_in)
    w_dn_out, w_gla_out, w_s5_glu, w_mix_out, w_xa_q, w_xa_kv, w_xa_out, w_ffn_up, w_ffn_down = (
        w.astype(BF16) for w in (w_dn_out, w_gla_out, w_s5_glu, w_mix_out, w_xa_q, w_xa_kv, w_xa_out,
                                 w_ffn_up, w_ffn_down))

    for l in range(depth):
        z2, zs2, su2 = _in_proj(x.reshape(t, d), row(norm_mix_pre[l]), w_big, w_small, l)
        z3 = z2.reshape(b, s, -1)
        zs3 = zs2.reshape(b, s, -1)

        oa = _deltanet(z3, zs3, dn_a_log[l], dn_dt_bias[l], dn_conv_w[l].T.astype(F32),
                       row(dn_norm_w[l]))

        w_up_pad = jnp.zeros((zs2.shape[1], GLA_HEADS * GLA_DK), F32)
        w_up_pad = w_up_pad.at[_SM_GLOW:_SM_GLOW + GLA_RANK].set(gla_w_up[l]).astype(BF16)
        ob = _gla(z3, zs3, w_up_pad, row(gla_b_up[l]), row(gla_norm_w[l]))

        ys = _s5(su2, s5_mats, l, nbatch=b)

        x = _mix_out(oa.reshape(t, -1), ob.reshape(t, -1), ys, z2, x.reshape(t, d),
                     w_dn_out, w_gla_out, w_s5_glu, w_mix_out, row(norm_mix_post[l]), l).reshape(b, s, d)

        kv = _norm_proj(mem2, row(norm_mem[l]), w_xa_kv, l)
        x = _xattn(x, row(norm_xa_pre[l]), w_xa_q, kv.reshape(b, -1, kv.shape[-1]), w_xa_out,
                   row(norm_xa_post[l]), l)

        x = _convffn(x, row(norm_ffn_pre[l]), w_ffn_up, ffn_conv_w[l].T.astype(F32), w_ffn_down,
                     row(norm_ffn_post[l]), l)
    return x
```

```python
import jax
import jax.numpy as jnp
from jax import lax
from jax.experimental import pallas as pl
from jax.experimental.pallas import tpu as pltpu

F32 = jnp.float32
BF16 = jnp.bfloat16

NORM_EPS = 1e-6

D_MODEL = 2048
DN_HEADS, DN_DK, DN_DV, DN_CONV = 8, 128, 128, 4
GLA_HEADS, GLA_DK, GLA_DV, GLA_RANK, GLA_TAU = 4, 128, 256, 16, 16.0
S5_GROUP, S5_GROUPS, S5_STATE = 16, 64, 64
S5_WIDTH = S5_GROUP * S5_GROUPS
XA_HEADS, XA_DH = 4, 128
FFN_HIDDEN, FFN_CONV = 5632, 3

CHUNK = 64
GLA_SUB = 16
S5_L = 16
DN_HG = 4
DN_BASE = 8
LANE = 128
S5_LB = LANE // S5_GROUP
SUBLANE = 8
VMEM_LIMIT = 56 * 1024 * 1024

_OFF_DQ, _OFF_DK, _OFF_DV, _OFF_DZ = 0, 1024, 2048, 3072
_OFF_GQ, _OFF_GK, _OFF_GV, _OFF_GR = 4096, 4608, 5120, 6144
_OFF_SU, _OFF_GATES = 7168, 8192
_BIG_WIDTH = 14336
_SM_GLOW = 16


def _cparams(sem):
    return pltpu.CompilerParams(dimension_semantics=sem, vmem_limit_bytes=VMEM_LIMIT)


def _rms(xf, g):
    return xf * lax.rsqrt(jnp.mean(xf * xf, axis=-1, keepdims=True) + NORM_EPS) * g


def _dot(a, b):
    return jnp.dot(a.astype(BF16), b.astype(BF16), preferred_element_type=F32)


def _dot_nt(a, b):
    return lax.dot_general(a.astype(BF16), b.astype(BF16), (((1,), (1,)), ((), ())),
                           preferred_element_type=F32)


def _dot_tn(a, b):
    return lax.dot_general(a.astype(BF16), b.astype(BF16), (((0,), (0,)), ((), ())),
                           preferred_element_type=F32)


def _sigmoid(x):
    return 0.5 * jnp.tanh(0.5 * x) + 0.5


def _silu(x):
    return x * _sigmoid(x)


def _softplus(x):
    return jnp.maximum(x, 0.0) + jnp.log(1.0 + jnp.exp(-jnp.abs(x)))


def _gelu_tanh(x):
    c = 0.7978845608028654
    return 0.5 * x * (1.0 + jnp.tanh(c * (x + 0.044715 * (x * x * x))))


def _in_proj_kernel(x_ref, g_ref, w_ref, ws_ref, o_ref, os_ref, su_ref, hn_ref):
    j = pl.program_id(1)
    tn = o_ref.shape[1]

    @pl.when(j == 0)
    def _():
        hn = _rms(x_ref[...], g_ref[...]).astype(BF16)
        hn_ref[...] = hn
        os_ref[...] = jnp.dot(hn, ws_ref[...], preferred_element_type=F32)

    acc = jnp.dot(hn_ref[...], w_ref[...], preferred_element_type=F32)
    o_ref[...] = acc.astype(o_ref.dtype)

    @pl.when(jnp.logical_and(j >= _OFF_SU // tn, j < (_OFF_SU + S5_WIDTH) // tn))
    def _():
        su_ref[...] = acc


def _in_proj(x2, g, w_big, w_small, layer, tm=1024, tn=1024):
    t, d = x2.shape
    n = w_big.shape[2]
    ns = w_small.shape[2]
    j0, nsu = _OFF_SU // tn, S5_WIDTH // tn
    return pl.pallas_call(
        _in_proj_kernel,
        out_shape=(jax.ShapeDtypeStruct((t, n), BF16), jax.ShapeDtypeStruct((t, ns), F32),
                   jax.ShapeDtypeStruct((t, S5_WIDTH), F32)),
        grid=(t // tm, n // tn),
        in_specs=[
            pl.BlockSpec((tm, d), lambda i, j: (i, 0)),
            pl.BlockSpec((1, d), lambda i, j: (0, 0)),
            pl.BlockSpec((None, d, tn), lambda i, j: (layer, 0, j)),
            pl.BlockSpec((None, d, ns), lambda i, j: (layer, 0, 0)),
        ],
        out_specs=(
            pl.BlockSpec((tm, tn), lambda i, j: (i, j)),
            pl.BlockSpec((tm, ns), lambda i, j: (i, 0)),
            pl.BlockSpec((tm, tn), lambda i, j: (i, jnp.clip(j - j0, 0, nsu - 1))),
        ),
        scratch_shapes=[pltpu.VMEM((tm, d), BF16)],
        compiler_params=_cparams(("parallel", "arbitrary")),
        name="in_proj",
    )(x2, g, w_big, w_small)


def _norm_proj_kernel(x_ref, g_ref, w_ref, o_ref):
    hn = _rms(x_ref[...], g_ref[...]).astype(BF16)
    o_ref[...] = jnp.dot(hn, w_ref[...], preferred_element_type=F32).astype(o_ref.dtype)


def _norm_proj(x2, g, w, layer, tm=512):
    t, d = x2.shape
    n = w.shape[2]
    return pl.pallas_call(
        _norm_proj_kernel,
        out_shape=jax.ShapeDtypeStruct((t, n), BF16),
        grid=(t // tm,),
        in_specs=[
            pl.BlockSpec((tm, d), lambda i: (i, 0)),
            pl.BlockSpec((1, d), lambda i: (0, 0)),
            pl.BlockSpec((None, d, n), lambda i: (layer, 0, 0)),
        ],
        out_specs=pl.BlockSpec((tm, n), lambda i: (i, 0)),
        compiler_params=_cparams(("parallel",)),
        name="mem_kv_proj",
    )(x2, g, w)


def _dot_hilo_rhs(a01, b):
    hi = b.astype(BF16)
    lo = (b - hi.astype(F32)).astype(BF16)
    return (jnp.dot(a01, hi, preferred_element_type=F32) + jnp.dot(a01, lo, preferred_element_type=F32))


def _dot3(a, b):
    ah = a.astype(BF16)
    al = (a - ah.astype(F32)).astype(BF16)
    bh = b.astype(BF16)
    bl = (b - bh.astype(F32)).astype(BF16)
    return (jnp.dot(ah, bh, preferred_element_type=F32)
            + (jnp.dot(ah, bl, preferred_element_type=F32) + jnp.dot(al, bh, preferred_element_type=F32)))


def _dot_hilo_lhs(a, b01):
    hi = a.astype(BF16)
    lo = (a - hi.astype(F32)).astype(BF16)
    return (jnp.dot(hi, b01, preferred_element_type=F32) + jnp.dot(lo, b01, preferred_element_type=F32))


def _deltanet_kernel(q_ref, k_ref, v_ref, zg_ref, zs_ref, alr_ref, dtr_ref, alc_ref,
                     dtc_ref, cwq_ref, cwk_ref, cwv_ref, nw_ref, o_ref, state_ref, tail_ref, ext_ref):
    tt = q_ref.shape[0]
    c = CHUNK
    hg = DN_HG
    dk, dv = DN_DK, DN_DV
    halo = SUBLANE
    nchunk = tt // c

    @pl.when(pl.program_id(2) == 0)
    def _():
        state_ref[...] = jnp.zeros_like(state_ref)
        tail_ref[...] = jnp.zeros_like(tail_ref)

    def conv_silu(idx, raw_ref, cw_ref):
        raw = raw_ref[...].astype(F32)
        ext_ref[idx, 0:halo, :] = tail_ref[idx]
        ext_ref[idx, halo:halo + tt, :] = raw
        tail_ref[idx] = raw[tt - halo:tt, :]
        cw = cw_ref[...]
        base = halo - (DN_CONV - 1)
        y = ext_ref[idx, base:base + tt, :] * cw[0:1, :]
        for j in range(1, DN_CONV):
            y = y + ext_ref[idx, base + j:base + j + tt, :] * cw[j:j + 1, :]
        return _silu(y)

    q_all = conv_silu(0, q_ref, cwq_ref)
    k_all = conv_silu(1, k_ref, cwk_ref)
    v_all = conv_silu(2, v_ref, cwv_ref)

    grp = pl.program_id(1)
    gcol = pltpu.roll(zs_ref[...], (LANE - 2 * hg * grp) % LANE, 1)
    grow = gcol.T[0:SUBLANE, :]
    g_rows = -jnp.exp(alr_ref[...]) * _softplus(grow + dtr_ref[...])
    beta_cols = _sigmoid(gcol)
    g_cols = -jnp.exp(alc_ref[...]) * _softplus(gcol + dtc_ref[...])

    ri = lax.broadcasted_iota(jnp.int32, (c, c), 0)
    ci = lax.broadcasted_iota(jnp.int32, (c, c), 1)
    causal = ri >= ci
    strict = ri > ci
    ident = jnp.where(ri == ci, 1.0, 0.0).astype(F32)
    tril01 = jnp.where(causal, 1.0, 0.0).astype(BF16)
    triu01 = jnp.where(ri <= ci, 1.0, 0.0).astype(BF16)

    items = [(hh, n) for n in range(nchunk) for hh in range(hg)]
    qs, lower, a_qk, e_col, k_dec, g_last, vb = {}, {}, {}, {}, {}, {}, {}
    for n in range(nchunk):
        sl = slice(n * c, (n + 1) * c)
        gcs_c = _dot_hilo_rhs(tril01, g_cols[sl])
        gcs_r = _dot_hilo_lhs(g_rows[:, sl], triu01)
        for hh in range(hg):
            it = (hh, n)
            hs = slice(hh * dk, (hh + 1) * dk)
            qc, kc = q_all[sl, hs], k_all[sl, hs]
            qc = qc * (lax.rsqrt(jnp.sum(qc * qc, axis=-1, keepdims=True) + NORM_EPS) * (dk ** -0.5))
            kc = kc * lax.rsqrt(jnp.sum(kc * kc, axis=-1, keepdims=True) + NORM_EPS)
            beta_col = beta_cols[sl, hh:hh + 1]
            gc_col = gcs_c[:, hg + hh:hg + hh + 1]
            gc_row = gcs_r[hg + hh:hg + hh + 1, :]
            gc_end = gcs_r[hg + hh:hg + hh + 1, c - 1:c]
            decay = jnp.where(causal, jnp.exp(jnp.where(causal, gc_col - gc_row, 0.0)), 0.0)
            kb = kc * beta_col
            res = _dot_nt(jnp.concatenate([kb, qc], axis=0), kc)
            lower[it] = jnp.where(strict, res[0:c] * decay, 0.0)
            a_qk[it] = jnp.where(causal, res[c:2 * c] * decay, 0.0)
            e_col[it] = jnp.exp(gc_col)
            k_dec[it] = kc * jnp.exp(gc_end - gc_col)
            g_last[it] = jnp.exp(gc_end)
            vb[it] = jnp.concatenate([v_all[sl, hh * dv:(hh + 1) * dv] * beta_col, kb * e_col[it]], axis=1)
            qs[it] = qc

    base_shift = DN_BASE.bit_length() - 1
    same_base = (ri >> base_shift) == (ci >> base_shift)
    inv, pw = {}, {}
    for it in items:
        lb = jnp.where(same_base, lower[it], 0.0)
        inv[it] = ident - lb
        pw[it] = _dot3(lb, lb)
    span = 2
    while span < DN_BASE:
        for it in items:
            if span * 2 < DN_BASE:
                res = _dot3(jnp.concatenate([inv[it], pw[it]], axis=0), pw[it])
                inv[it] = inv[it] + res[0:c]
                pw[it] = res[c:2 * c]
            else:
                inv[it] = inv[it] + _dot3(inv[it], pw[it])
        span *= 2
    width = 2 * DN_BASE
    while width <= c:
        sh = width.bit_length() - 1
        same_w = (ri >> sh) == (ci >> sh)
        same_half = (ri >> (sh - 1)) == (ci >> (sh - 1))
        off = {it: jnp.where(same_w, jnp.where(same_half, 0.0, lower[it]), 0.0) for it in items}
        mid = {it: _dot3(off[it], inv[it]) for it in items}
        for it in items:
            inv[it] = inv[it] - _dot3(inv[it], mid[it])
        width *= 2

    uw = {it: _dot3(inv[it], vb[it]) for it in items}
    a_uw = {it: _dot(a_qk[it], uw[it]) for it in items}
    k_uw = {it: _dot_tn(k_dec[it], uw[it]) for it in items}

    nw = nw_ref[...]
    state = [state_ref[hh] for hh in range(hg)]
    for n in range(nchunk):
        sl = slice(n * c, (n + 1) * c)
        for hh in range(hg):
            it = (hh, n)
            q_eff = qs[it] * e_col[it] - a_uw[it][:, dv:dv + dk]
            res = _dot(jnp.concatenate([q_eff, k_uw[it][:, dv:dv + dk]], axis=0), state[hh])
            o = res[0:c] + a_uw[it][:, 0:dv]
            state[hh] = state[hh] * g_last[it] - res[c:c + dk] + k_uw[it][:, 0:dv]
            hs = slice(hh * dv, (hh + 1) * dv)
            o = _rms(o, nw) * _silu(zg_ref[sl, hs].astype(F32))
            o_ref[sl, hs] = o.astype(o_ref.dtype)
    for hh in range(hg):
        state_ref[hh] = state[hh]


def _deltanet(z3, zs3, dn_a_log, dn_dt_bias, conv_w_t, norm_w, tt=256):
    b, s, _ = z3.shape
    nh, hg = DN_HEADS, DN_HG
    ngrp = nh // hg
    wq = hg * DN_DK
    bq, bk, bv, bz = _OFF_DQ // wq, _OFF_DK // wq, _OFF_DV // wq, _OFF_DZ // wq
    al = dn_a_log.astype(F32).reshape(ngrp, hg)
    dtb = dn_dt_bias.astype(F32).reshape(ngrp, hg)
    alr = jnp.pad(al, ((0, 0), (hg, SUBLANE - 2 * hg)))[:, :, None]
    dtr = jnp.pad(dtb, ((0, 0), (hg, SUBLANE - 2 * hg)))[:, :, None]
    alc = jnp.pad(al, ((0, 0), (hg, LANE - 2 * hg)))[:, None, :]
    dtc = jnp.pad(dtb, ((0, 0), (hg, LANE - 2 * hg)))[:, None, :]
    return pl.pallas_call(
        _deltanet_kernel,
        out_shape=jax.ShapeDtypeStruct((b, s, nh * DN_DV), BF16),
        grid=(b, ngrp, s // tt),
        in_specs=[
            pl.BlockSpec((None, tt, wq), lambda bi, g, i: (bi, i, bq + g)),
            pl.BlockSpec((None, tt, wq), lambda bi, g, i: (bi, i, bk + g)),
            pl.BlockSpec((None, tt, wq), lambda bi, g, i: (bi, i, bv + g)),
            pl.BlockSpec((None, tt, wq), lambda bi, g, i: (bi, i, bz + g)),
            pl.BlockSpec((None, tt, LANE), lambda bi, g, i: (bi, i, 0)),
            pl.BlockSpec((None, SUBLANE, 1), lambda bi, g, i: (g, 0, 0)),
            pl.BlockSpec((None, SUBLANE, 1), lambda bi, g, i: (g, 0, 0)),
            pl.BlockSpec((None, 1, LANE), lambda bi, g, i: (g, 0, 0)),
            pl.BlockSpec((None, 1, LANE), lambda bi, g, i: (g, 0, 0)),
            pl.BlockSpec((DN_CONV, wq), lambda bi, g, i: (0, g)),
            pl.BlockSpec((DN_CONV, wq), lambda bi, g, i: (0, ngrp + g)),
            pl.BlockSpec((DN_CONV, wq), lambda bi, g, i: (0, 2 * ngrp + g)),
            pl.BlockSpec((1, DN_DV), lambda bi, g, i: (0, 0)),
        ],
        out_specs=pl.BlockSpec((None, tt, wq), lambda bi, g, i: (bi, i, g)),
        scratch_shapes=[
            pltpu.VMEM((hg, DN_DK, DN_DV), F32),
            pltpu.VMEM((3, SUBLANE, wq), F32),
            pltpu.VMEM((3, tt + SUBLANE, wq), F32),
        ],
        compiler_params=_cparams(("parallel", "parallel", "arbitrary")),
        name="deltanet",
    )(z3, z3, z3, z3, zs3, alr, dtr, alc, dtc, conv_w_t, conv_w_t, conv_w_t, norm_w)


def _gla_kernel(q_ref, k_ref, v_ref, r_ref, zs_ref, wup_ref, bup_ref, nw_ref, o_ref, state_ref):
    tt = q_ref.shape[0]
    c = CHUNK
    sub = GLA_SUB

    @pl.when(pl.program_id(2) == 0)
    def _():
        state_ref[...] = jnp.zeros_like(state_ref)

    logit = _dot(zs_ref[...], wup_ref[...]) + bup_ref[...]
    log_a = -_softplus(-logit) * (1.0 / GLA_TAU)

    ri = lax.broadcasted_iota(jnp.int32, (c, c), 0)
    ci = lax.broadcasted_iota(jnp.int32, (c, c), 1)
    tril = jnp.where(ri >= ci, 1.0, 0.0).astype(BF16)
    grp_rows = lax.broadcasted_iota(jnp.int32, (SUBLANE, 1), 0)
    key_rows = lax.broadcasted_iota(jnp.int32, (c, 1), 0)
    grp_cols = lax.broadcasted_iota(jnp.int32, (SUBLANE, c), 1)

    nw = nw_ref[...]
    scale = GLA_DK ** -0.5
    nchunk = tt // c
    nblk = c // sub
    chunks = range(nchunk)
    bcum = [_dot_hilo_rhs(tril, log_a[n * c:(n + 1) * c]) for n in chunks]
    qc = [q_ref[n * c:(n + 1) * c, :].astype(F32) * scale for n in chunks]
    kc = [k_ref[n * c:(n + 1) * c, :].astype(F32) for n in chunks]
    vc = [v_ref[n * c:(n + 1) * c, :] for n in chunks]
    b_last = [bcum[n][c - 1:c, :] for n in chunks]
    q_dec = [qc[n] * jnp.exp(bcum[n]) for n in chunks]
    k_dec = [kc[n] * jnp.exp(b_last[n] - bcum[n]) for n in chunks]
    kv = [_dot_tn(vc[n], k_dec[n]) for n in chunks]

    off = {}
    for n in chunks:
        for bi in range(1, nblk):
            r0 = bi * sub
            b_ref = bcum[n][r0 - 1:r0, :]
            early = key_rows < r0
            q_f = qc[n][r0:r0 + sub] * jnp.exp(bcum[n][r0:r0 + sub] - b_ref)
            k_f = jnp.where(early, kc[n] * jnp.exp(jnp.where(early, b_ref - bcum[n], 0.0)), 0.0)
            off[n, bi] = _dot_nt(q_f, k_f)

    a_qk = []
    for n in chunks:
        row_blocks = []
        for bi in range(nblk):
            r0 = bi * sub
            blk = off[n, bi] if bi > 0 else jnp.zeros((sub, c), F32)
            for g0 in range(0, sub, SUBLANE):
                q_g = qc[n][r0 + g0:r0 + g0 + SUBLANE]
                b_g = bcum[n][r0 + g0:r0 + g0 + SUBLANE]
                part = blk[g0:g0 + SUBLANE]
                for j in range(g0 + SUBLANE):
                    kj = kc[n][r0 + j:r0 + j + 1, :]
                    bj = bcum[n][r0 + j:r0 + j + 1, :]
                    live = grp_rows + g0 >= j
                    dec = jnp.exp(jnp.where(live, b_g - bj, 0.0))
                    col = jnp.sum(jnp.where(live, q_g * kj * dec, 0.0), axis=1, keepdims=True)
                    part = jnp.where(grp_cols == r0 + j, col, part)
                row_blocks.append(part)
        a_qk.append(jnp.concatenate(row_blocks, axis=0))

    intra = [_dot(a_qk[n], vc[n]) for n in chunks]
    state_t = state_ref[...]
    for n in chunks:
        sl = slice(n * c, (n + 1) * c)
        o = _dot_nt(q_dec[n], state_t) + intra[n]
        state_t = state_t * jnp.exp(b_last[n]) + kv[n]
        o = _rms(o, nw) * _silu(r_ref[sl, :].astype(F32))
        o_ref[sl, :] = o.astype(o_ref.dtype)
    state_ref[...] = state_t


def _gla(z3, zs3, w_up_pad, b_up, norm_w, tt=256):
    b, s, _ = z3.shape
    hq, hk = _OFF_GQ // GLA_DK, _OFF_GK // GLA_DK
    hv, hr = _OFF_GV // GLA_DV, _OFF_GR // GLA_DV
    ns = zs3.shape[-1]
    return pl.pallas_call(
        _gla_kernel,
        out_shape=jax.ShapeDtypeStruct((b, s, GLA_HEADS * GLA_DV), BF16),
        grid=(b, GLA_HEADS, s // tt),
        in_specs=[
            pl.BlockSpec((None, tt, GLA_DK), lambda bi, h, i: (bi, i, hq + h)),
            pl.BlockSpec((None, tt, GLA_DK), lambda bi, h, i: (bi, i, hk + h)),
            pl.BlockSpec((None, tt, GLA_DV), lambda bi, h, i: (bi, i, hv + h)),
            pl.BlockSpec((None, tt, GLA_DV), lambda bi, h, i: (bi, i, hr + h)),
            pl.BlockSpec((None, tt, ns), lambda bi, h, i: (bi, i, 0)),
            pl.BlockSpec((ns, GLA_DK), lambda bi, h, i: (0, h)),
            pl.BlockSpec((1, GLA_DK), lambda bi, h, i: (0, h)),
            pl.BlockSpec((1, GLA_DV), lambda bi, h, i: (0, 0)),
        ],
        out_specs=pl.BlockSpec((None, tt, GLA_DV), lambda bi, h, i: (bi, i, h)),
        scratch_shapes=[pltpu.VMEM((GLA_DV, GLA_DK), F32)],
        compiler_params=_cparams(("parallel", "parallel", "arbitrary")),
        name="gla",
    )(z3, z3, z3, z3, zs3, w_up_pad, b_up, norm_w)


def _s5_kernel(su_ref, kblk_ref, vre_ref, vim_ref, wre_ref, wim_ref, al_ref, d_ref,
               o_ref, m_ref, v_ref, w_ref, e_ref, xp_ref):
    ll = S5_L
    rows = e_ref.shape[0]
    nst = S5_LB * S5_STATE

    @pl.when(pl.program_id(1) == 0)
    def _():
        zero_blk = jnp.zeros((LANE, LANE), BF16)
        for s in range(ll):
            for i in range(ll):
                m_ref[s * LANE:(s + 1) * LANE, i * LANE:(i + 1) * LANE] = (
                    kblk_ref[i - s] if i >= s else zero_blk)
        rblk = 2 * LANE
        vr = lax.broadcasted_iota(jnp.int32, (rblk, nst), 0)
        vc = lax.broadcasted_iota(jnp.int32, (rblk, nst), 1)
        keep_v = ((vr & (LANE - 1)) >> 4) == (vc >> 6)
        for part, vt_ref in enumerate((vre_ref, vim_ref)):
            vt = vt_ref[...]
            v2 = jnp.concatenate([vt, vt], axis=0).T
            for r in range(ll * LANE // rblk):
                rs = slice(r * rblk, (r + 1) * rblk)
                v_ref[rs, part * nst:(part + 1) * nst] = jnp.where(
                    keep_v, jnp.tile(v2[rs, :], (1, nst // LANE)), 0.0).astype(BF16)
        wr = lax.broadcasted_iota(jnp.int32, (nst, rblk), 0)
        wc = lax.broadcasted_iota(jnp.int32, (nst, rblk), 1)
        keep_w = (wr >> 6) == ((wc & (LANE - 1)) >> 4)
        for r in range(ll * LANE // rblk):
            cs = slice(r * rblk, (r + 1) * rblk)
            w_ref[0:nst, cs] = jnp.where(keep_w, jnp.tile(wre_ref[:, cs], (S5_LB, 1)), 0.0).astype(BF16)
            w_ref[nst:2 * nst, cs] = jnp.where(keep_w, jnp.tile(wim_ref[:, cs], (S5_LB, 1)), 0.0).astype(BF16)

    u_list = [su_ref[pl.ds(s, rows, stride=ll), :] for s in range(ll)]
    u_cat = jnp.concatenate([u.astype(BF16) for u in u_list], axis=1)
    e_ref[...] = jnp.dot(u_cat, v_ref[...], preferred_element_type=F32)
    ar = al_ref[:, 0:nst]
    ai = al_ref[:, nst:2 * nst]

    def step(cidx, carry):
        xr, xi = carry
        row = pl.ds(cidx, 1)
        xp_ref[row, 0:nst] = xr
        xp_ref[row, nst:2 * nst] = xi
        er = e_ref[row, 0:nst]
        ei = e_ref[row, nst:2 * nst]
        return ar * xr - ai * xi + er, ar * xi + ai * xr + ei

    zero = jnp.zeros((1, nst), F32)
    lax.fori_loop(0, rows, step, (zero, zero), unroll=8)

    pair = 2 * LANE
    intra = [jnp.dot(u_cat[:, 0:(m + 1) * pair], m_ref[0:(m + 1) * pair, m * pair:(m + 1) * pair],
                     preferred_element_type=F32) for m in range(ll // 2)]
    y = (jnp.concatenate(intra, axis=1)
         + jnp.dot(xp_ref[...].astype(BF16), w_ref[...], preferred_element_type=F32))
    d = d_ref[...]
    for i in range(ll):
        yi = y[:, i * LANE:(i + 1) * LANE] + d * u_list[i]
        o_ref[pl.ds(i, rows, stride=ll), :] = _gelu_tanh(yi)


def _s5(su2, mats, layer, nbatch):
    kblk, vre, vim, wre, wim, al, dskip = mats
    t, width = su2.shape
    s = t // nbatch
    ll = S5_L
    rows = s // ll
    nblk = width // LANE
    nst2 = 2 * S5_LB * S5_STATE

    return pl.pallas_call(
        _s5_kernel,
        out_shape=jax.ShapeDtypeStruct((t, width), F32),
        grid=(nblk, nbatch),
        in_specs=[
            pl.BlockSpec((s, LANE), lambda j, b: (b, j)),
            pl.BlockSpec((None, None, ll, LANE, LANE), lambda j, b: (layer, j, 0, 0, 0)),
            pl.BlockSpec((None, None, S5_STATE, ll * LANE), lambda j, b: (layer, j, 0, 0)),
            pl.BlockSpec((None, None, S5_STATE, ll * LANE), lambda j, b: (layer, j, 0, 0)),
            pl.BlockSpec((None, None, S5_STATE, ll * LANE), lambda j, b: (layer, j, 0, 0)),
            pl.BlockSpec((None, None, S5_STATE, ll * LANE), lambda j, b: (layer, j, 0, 0)),
            pl.BlockSpec((None, None, 1, nst2), lambda j, b: (layer, j, 0, 0)),
            pl.BlockSpec((None, None, 1, LANE), lambda j, b: (layer, j, 0, 0)),
        ],
        out_specs=pl.BlockSpec((s, LANE), lambda j, b: (b, j)),
        scratch_shapes=[
            pltpu.VMEM((ll * LANE, ll * LANE), BF16),
            pltpu.VMEM((ll * LANE, nst2), BF16),
            pltpu.VMEM((nst2, ll * LANE), BF16),
            pltpu.VMEM((rows, nst2), F32),
            pltpu.VMEM((rows, nst2), F32),
        ],
        compiler_params=_cparams(("parallel", "arbitrary")),
        name="s5",
    )(su2, kblk, vre, vim, wre, wim, al, dskip)


def _s5_matrices(lam_re, lam_im, log_step, b_re, b_im, c_re, c_im, d_skip):
    g, p, hh, ll, lb = S5_GROUPS, S5_STATE, S5_GROUP, S5_L, S5_LB
    nb = g // lb
    lr = jnp.minimum(lam_re.astype(F32), -1e-4)
    li = lam_im.astype(F32)
    dt = jnp.exp(log_step.astype(F32))[:, None]
    mag = jnp.exp(lr * dt)
    ar = mag * jnp.cos(li * dt)
    ai = mag * jnp.sin(li * dt)
    nr = ar - 1.0
    den = lr * lr + li * li
    zr = (nr * lr + ai * li) / den
    zi = (ai * lr - nr * li) / den
    br = b_re.astype(F32)
    bi = b_im.astype(F32)
    bbr = zr[..., None] * br - zi[..., None] * bi
    bbi = zr[..., None] * bi + zi[..., None] * br
    pr = [jnp.ones_like(ar)]
    pi = [jnp.zeros_like(ai)]
    for _ in range(ll):
        pr.append(pr[-1] * ar - pi[-1] * ai)
        pi.append(pr[-2] * ai + pi[-1] * ar)
    pr_rev = jnp.stack(pr[ll - 1::-1], axis=0)
    pi_rev = jnp.stack(pi[ll - 1::-1], axis=0)
    pr = jnp.stack(pr, axis=0)
    pi = jnp.stack(pi, axis=0)
    cr = c_re.astype(F32)
    cim = c_im.astype(F32)

    def blk(a, *shape):
        return a.reshape((nb, lb) + shape)

    pr_d = jnp.moveaxis(pr.reshape(ll + 1, nb, lb, p), 0, 1)[:, :, :, None, None, :]
    pi_d = jnp.moveaxis(pi.reshape(ll + 1, nb, lb, p), 0, 1)[:, :, :, None, None, :]
    cr_b = blk(cr, hh, p)[:, None, :, None, :, :]
    ci_b = blk(cim, hh, p)[:, None, :, None, :, :]
    qr = cr_b * pr_d - ci_b * pi_d
    qi = cr_b * pi_d + ci_b * pr_d
    bbr_b = blk(jnp.swapaxes(bbr, 1, 2), hh, p)[:, None, :, :, None, :]
    bbi_b = blk(jnp.swapaxes(bbi, 1, 2), hh, p)[:, None, :, :, None, :]
    kd = (jnp.einsum('ndgop,ngip->ndgio', qr[:, :ll, :, 0], bbr_b[:, 0, :, :, 0])
          - jnp.einsum('ndgop,ngip->ndgio', qi[:, :ll, :, 0], bbi_b[:, 0, :, :, 0]))
    spread = jnp.tile(jnp.eye(hh, dtype=F32), (1, lb))
    kfull = jnp.einsum('ndro,oc->ndrc', kd.reshape(nb, ll, lb * hh, hh), spread)
    rg = lax.broadcasted_iota(jnp.int32, (lb * hh, lb * hh), 0) // hh
    cg = lax.broadcasted_iota(jnp.int32, (lb * hh, lb * hh), 1) // hh
    kblk = jnp.where(rg == cg, kfull, 0.0)
    prs = jnp.moveaxis(pr_rev.reshape(ll, nb, lb, p), 0, 1)[:, :, :, None, :]
    pis = jnp.moveaxis(pi_rev.reshape(ll, nb, lb, p), 0, 1)[:, :, :, None, :]
    bbr_v, bbi_v = bbr_b[:, :, :, :, 0, :], bbi_b[:, :, :, :, 0, :]
    vre = jnp.moveaxis(prs * bbr_v - pis * bbi_v, -1, 1).reshape(nb, p, ll * lb * hh)
    vim = jnp.moveaxis(prs * bbi_v + pis * bbr_v, -1, 1).reshape(nb, p, ll * lb * hh)
    wre = jnp.moveaxis(qr[:, 1:ll + 1, :, 0], -1, 1).reshape(nb, p, ll * lb * hh)
    wim = -jnp.moveaxis(qi[:, 1:ll + 1, :, 0], -1, 1).reshape(nb, p, ll * lb * hh)

    al = jnp.concatenate([pr[ll].reshape(nb, 1, lb * p), pi[ll].reshape(nb, 1, lb * p)], axis=2)
    dskip = d_skip.astype(F32).reshape(nb, 1, lb * hh)
    return kblk.astype(BF16), vre, vim, wre, wim, al, dskip


def _mix_out_kernel(oa_ref, ob_ref, ys_ref, ga_ref, gb_ref, gc_ref, x_ref,
                    wa_ref, wb_ref, wca_ref, wcb_ref, wm_ref, gpost_ref, o_ref, acc_ref):
    n = pl.program_id(1)

    @pl.when(n == 0)
    def _():
        acc_ref[...] = jnp.zeros_like(acc_ref)

    oa, ob = oa_ref[...], ob_ref[...]
    ys = ys_ref[...].astype(BF16)
    tn = wa_ref.shape[1]
    halves = [slice(h * (tn // 2), (h + 1) * (tn // 2)) for h in range(2)]
    merged = []
    for hs in halves:
        y_a = jnp.dot(oa, wa_ref[:, hs], preferred_element_type=F32)
        y_b = jnp.dot(ob, wb_ref[:, hs], preferred_element_type=F32)
        glu_a = jnp.dot(ys, wca_ref[:, hs], preferred_element_type=F32)
        glu_b = jnp.dot(ys, wcb_ref[:, hs], preferred_element_type=F32)
        y_c = glu_a * _sigmoid(glu_b)
        merged.append((_sigmoid(ga_ref[:, hs].astype(F32)) * y_a + _sigmoid(gb_ref[:, hs].astype(F32)) * y_b
                       + _sigmoid(gc_ref[:, hs].astype(F32)) * y_c).astype(BF16))
    acc_ref[...] += (jnp.dot(merged[0], wm_ref[halves[0], :], preferred_element_type=F32)
                     + jnp.dot(merged[1], wm_ref[halves[1], :], preferred_element_type=F32))

    @pl.when(n == pl.num_programs(1) - 1)
    def _():
        o_ref[...] = x_ref[...] + _rms(acc_ref[...], gpost_ref[...])


def _mix_out(oa, ob, ys, z2, x2, w_dn_out, w_gla_out, w_s5_glu, w_mix_out, g_post, layer, tm=512, tn=512):
    t, d = x2.shape
    nsteps = d // tn
    gblk = _OFF_GATES // tn
    ka, kb, kc = oa.shape[1], ob.shape[1], ys.shape[1]
    return pl.pallas_call(
        _mix_out_kernel,
        out_shape=jax.ShapeDtypeStruct((t, d), F32),
        grid=(t // tm, nsteps),
        in_specs=[
            pl.BlockSpec((tm, ka), lambda i, n: (i, 0)),
            pl.BlockSpec((tm, kb), lambda i, n: (i, 0)),
            pl.BlockSpec((tm, kc), lambda i, n: (i, 0)),
            pl.BlockSpec((tm, tn), lambda i, n: (i, gblk + n)),
            pl.BlockSpec((tm, tn), lambda i, n: (i, gblk + nsteps + n)),
            pl.BlockSpec((tm, tn), lambda i, n: (i, gblk + 2 * nsteps + n)),
            pl.BlockSpec((tm, d), lambda i, n: (i, 0)),
            pl.BlockSpec((None, ka, tn), lambda i, n: (layer, 0, n)),
            pl.BlockSpec((None, kb, tn), lambda i, n: (layer, 0, n)),
            pl.BlockSpec((None, kc, tn), lambda i, n: (layer, 0, n)),
            pl.BlockSpec((None, kc, tn), lambda i, n: (layer, 0, nsteps + n)),
            pl.BlockSpec((None, tn, d), lambda i, n: (layer, n, 0)),
            pl.BlockSpec((1, d), lambda i, n: (0, 0)),
        ],
        out_specs=pl.BlockSpec((tm, d), lambda i, n: (i, 0)),
        scratch_shapes=[pltpu.VMEM((tm, d), F32)],
        compiler_params=_cparams(("parallel", "arbitrary")),
        name="mix_out",
    )(oa, ob, ys, z2, z2, z2, x2, w_dn_out, w_gla_out, w_s5_glu, w_s5_glu, w_mix_out, g_post)


def _xattn_kernel(x_ref, gpre_ref, wq_ref, kv_ref, wo_ref, gpost_ref, o_ref):
    x = x_ref[...]
    hn = _rms(x, gpre_ref[...]).astype(BF16)
    q = jnp.dot(hn, wq_ref[...], preferred_element_type=F32)
    kv = kv_ref[...]
    heads = range(XA_HEADS)
    scores = [_dot_nt(q[:, h * XA_DH:(h + 1) * XA_DH], kv[:, h * XA_DH:(h + 1) * XA_DH])
              * (XA_DH ** -0.5) for h in heads]
    probs = []
    for s in scores:
        s = s - jnp.max(s, axis=-1, keepdims=True)
        p = jnp.exp(s)
        probs.append(p / jnp.sum(p, axis=-1, keepdims=True))
    outs = [_dot(probs[h], kv[:, (XA_HEADS + h) * XA_DH:(XA_HEADS + h + 1) * XA_DH]) for h in heads]
    o = jnp.concatenate(outs, axis=1)
    y = jnp.dot(o.astype(BF16), wo_ref[...], preferred_element_type=F32)
    o_ref[...] = x + _rms(y, gpost_ref[...])


def _xattn(x3, g_pre, w_q, kv3, w_out, g_post, layer, tm=512):
    b, s, d = x3.shape
    m, nkv = kv3.shape[1], kv3.shape[2]
    nq = w_q.shape[2]
    return pl.pallas_call(
        _xattn_kernel,
        out_shape=jax.ShapeDtypeStruct((b, s, d), F32),
        grid=(b, s // tm),
        in_specs=[
            pl.BlockSpec((None, tm, d), lambda bi, i: (bi, i, 0)),
            pl.BlockSpec((1, d), lambda bi, i: (0, 0)),
            pl.BlockSpec((None, d, nq), lambda bi, i: (layer, 0, 0)),
            pl.BlockSpec((None, m, nkv), lambda bi, i: (bi, 0, 0)),
            pl.BlockSpec((None, nq, d), lambda bi, i: (layer, 0, 0)),
            pl.BlockSpec((1, d), lambda bi, i: (0, 0)),
        ],
        out_specs=pl.BlockSpec((None, tm, d), lambda bi, i: (bi, i, 0)),
        compiler_params=_cparams(("parallel", "parallel")),
        name="xattn",
    )(x3, g_pre, w_q, kv3, w_out, g_post)


def _convffn_kernel(x_ref, xh_ref, gpre_ref, wg_ref, wu_ref, cw_ref, wd_ref, gpost_ref,
                    o_ref, hn_ref, hh_ref, gate_ref, acc_ref):
    i = pl.program_id(1)
    f = pl.program_id(2)
    tm = x_ref.shape[0]
    halo = SUBLANE

    @pl.when(f == 0)
    def _():
        hn_ref[...] = _rms(x_ref[...], gpre_ref[...]).astype(BF16)
        keep = jnp.where(i > 0, 1.0, 0.0)
        hh_ref[...] = (_rms(xh_ref[...], gpre_ref[...]) * keep).astype(BF16)
        acc_ref[...] = jnp.zeros_like(acc_ref)

    hn = hn_ref[...]
    wg = wg_ref[...]
    gate_ref[0:halo, :] = jnp.dot(hh_ref[...], wg, preferred_element_type=F32)
    gate_ref[halo:halo + tm, :] = jnp.dot(hn, wg, preferred_element_type=F32)
    up = jnp.dot(hn, wu_ref[...], preferred_element_type=F32)
    cw = cw_ref[...]
    base = halo - (FFN_CONV - 1)
    conv = gate_ref[base:base + tm, :] * cw[0:1, :]
    for j in range(1, FFN_CONV):
        conv = conv + gate_ref[base + j:base + j + tm, :] * cw[j:j + 1, :]
    act = (_gelu_tanh(conv) * up).astype(BF16)
    acc_ref[...] += jnp.dot(act, wd_ref[...], preferred_element_type=F32)

    @pl.when(f == pl.num_programs(2) - 1)
    def _():
        o_ref[...] = x_ref[...] + _rms(acc_ref[...], gpost_ref[...])


def _convffn(x3, g_pre, w_up, conv_w_t, w_down, g_post, layer, tm=512, tf=512):
    b, s, d = x3.shape
    fh = w_down.shape[1]
    nf = fh // tf
    hb = tm // SUBLANE
    return pl.pallas_call(
        _convffn_kernel,
        out_shape=jax.ShapeDtypeStruct((b, s, d), F32),
        grid=(b, s // tm, nf),
        in_specs=[
            pl.BlockSpec((None, tm, d), lambda bi, i, f: (bi, i, 0)),
            pl.BlockSpec((None, SUBLANE, d), lambda bi, i, f: (bi, jnp.maximum(i * hb - 1, 0), 0)),
            pl.BlockSpec((1, d), lambda bi, i, f: (0, 0)),
            pl.BlockSpec((None, d, tf), lambda bi, i, f: (layer, 0, f)),
            pl.BlockSpec((None, d, tf), lambda bi, i, f: (layer, 0, nf + f)),
            pl.BlockSpec((FFN_CONV, tf), lambda bi, i, f: (0, f)),
            pl.BlockSpec((None, tf, d), lambda bi, i, f: (layer, f, 0)),
            pl.BlockSpec((1, d), lambda bi, i, f: (0, 0)),
        ],
        out_specs=pl.BlockSpec((None, tm, d), lambda bi, i, f: (bi, i, 0)),
        scratch_shapes=[
            pltpu.VMEM((tm, d), BF16),
            pltpu.VMEM((SUBLANE, d), BF16),
            pltpu.VMEM((tm + SUBLANE, tf), F32),
            pltpu.VMEM((tm, d), F32),
        ],
        compiler_params=_cparams(("parallel", "parallel", "arbitrary")),
        name="convffn",
    )(x3, x3, g_pre, w_up, w_up, conv_w_t, w_down, g_post)


def _split_w_in(w_in):
    sizes = (DN_HEADS * DN_DK, DN_HEADS * DN_DK, DN_HEADS * DN_DV, DN_HEADS * DN_DV, DN_HEADS,
             DN_HEADS, GLA_HEADS * GLA_DK, GLA_HEADS * GLA_DK, GLA_HEADS * GLA_DV, GLA_RANK,
             GLA_HEADS * GLA_DV, S5_WIDTH, 3 * D_MODEL)
    parts = []
    acc = 0
    for n in sizes:
        parts.append(w_in[..., acc:acc + n])
        acc += n
    (dq, dk, dv, dz, db, da, gq, gk, gv, glow, gr, su, gates) = parts
    o1 = sum(sizes[:4])
    o2, o3 = o1 + sizes[4] + sizes[5], o1 + sum(sizes[4:9])
    o4 = o3 + sizes[9]
    big = jnp.concatenate([w_in[..., :o1], w_in[..., o2:o3], w_in[..., o4:]], axis=-1).astype(BF16)
    pad = jnp.zeros(w_in.shape[:-1] + (LANE - 2 * DN_HEADS - GLA_RANK,), w_in.dtype)
    per_group = []
    for g in range(DN_HEADS // DN_HG):
        per_group += [db[..., g * DN_HG:(g + 1) * DN_HG], da[..., g * DN_HG:(g + 1) * DN_HG]]
    small = jnp.concatenate(per_group + [glow, pad], axis=-1).astype(BF16)
    return big, small


def kernel(x, mem, norm_mix_pre, norm_mix_post, w_in, dn_conv_w, dn_a_log, dn_dt_bias, dn_norm_w, w_dn_out, gla_w_up, gla_b_up, gla_norm_w, w_gla_out, s5_lam_re, s5_lam_im, s5_log_step, s5_b_re, s5_b_im, s5_c_re, s5_c_im, s5_d, w_s5_glu, w_mix_out, norm_xa_pre, norm_xa_post, norm_mem, w_xa_q, w_xa_kv, w_xa_out, norm_ffn_pre, norm_ffn_post, w_ffn_up, ffn_conv_w, w_ffn_down):
    b, s, d = x.shape
    t = b * s
    depth = w_in.shape[0]
    mem2 = mem.reshape(-1, d)
    row = lambda a: a.reshape(1, -1).astype(F32)
    s5_mats = jax.vmap(_s5_matrices)(s5_lam_re, s5_lam_im, s5_log_step, s5_b_re, s5_b_im,
                                     s5_c_re, s5_c_im, s5_d)

    w_big, w_small = _split_w_in(w_in)
    w_dn_out, w_gla_out, w_s5_glu, w_mix_out, w_xa_q, w_xa_kv, w_xa_out, w_ffn_up, w_ffn_down = (
        w.astype(BF16) for w in (w_dn_out, w_gla_out, w_s5_glu, w_mix_out, w_xa_q, w_xa_kv, w_xa_out,
                                 w_ffn_up, w_ffn_down))

    for l in range(depth):
        z2, zs2, su2 = _in_proj(x.reshape(t, d), row(norm_mix_pre[l]), w_big, w_small, l)
        z3 = z2.reshape(b, s, -1)
        zs3 = zs2.reshape(b, s, -1)

        oa = _deltanet(z3, zs3, dn_a_log[l], dn_dt_bias[l], dn_conv_w[l].T.astype(F32),
                       row(dn_norm_w[l]))

        w_up_pad = jnp.zeros((zs2.shape[1], GLA_HEADS * GLA_DK), F32)
        w_up_pad = w_up_pad.at[_SM_GLOW:_SM_GLOW + GLA_RANK].set(gla_w_up[l]).astype(BF16)
        ob = _gla(z3, zs3, w_up_pad, row(gla_b_up[l]), row(gla_norm_w[l]))

        ys = _s5(su2, s5_mats, l, nbatch=b)

        x = _mix_out(oa.reshape(t, -1), ob.reshape(t, -1), ys, z2, x.reshape(t, d),
                     w_dn_out, w_gla_out, w_s5_glu, w_mix_out, row(norm_mix_post[l]), l).reshape(b, s, d)

        kv = _norm_proj(mem2, row(norm_mem[l]), w_xa_kv, l)
        x = _xattn(x, row(norm_xa_pre[l]), w_xa_q, kv.reshape(b, -1, kv.shape[-1]), w_xa_out,
                   row(norm_xa_post[l]), l)

        x = _convffn(x, row(norm_ffn_pre[l]), w_ffn_up, ffn_conv_w[l].T.astype(F32), w_ffn_down,
                     row(norm_ffn_post[l]), l)
    return x
```

```python
import jax
import jax.numpy as jnp
from jax import lax
from jax.experimental import pallas as pl
from jax.experimental.pallas import tpu as pltpu

F32 = jnp.float32
BF16 = jnp.bfloat16

NORM_EPS = 1e-6

D_MODEL = 2048
DN_HEADS, DN_DK, DN_DV, DN_CONV = 8, 128, 128, 4
GLA_HEADS, GLA_DK, GLA_DV, GLA_RANK, GLA_TAU = 4, 128, 256, 16, 16.0
S5_GROUP, S5_GROUPS, S5_STATE = 16, 64, 64
S5_WIDTH = S5_GROUP * S5_GROUPS
XA_HEADS, XA_DH = 4, 128
FFN_HIDDEN, FFN_CONV = 5632, 3

CHUNK = 64
GLA_SUB = 16
S5_L = 16
DN_HG = 4
DN_BASE = 8
LANE = 128
S5_LB = LANE // S5_GROUP
SUBLANE = 8
VMEM_LIMIT = 56 * 1024 * 1024

_OFF_DQ, _OFF_DK, _OFF_DV, _OFF_DZ = 0, 1024, 2048, 3072
_OFF_GQ, _OFF_GK, _OFF_GV, _OFF_GR = 4096, 4608, 5120, 6144
_OFF_SU, _OFF_GATES = 7168, 8192
_BIG_WIDTH = 14336
_SM_GLOW = 16


def _cparams(sem):
    return pltpu.CompilerParams(dimension_semantics=sem, vmem_limit_bytes=VMEM_LIMIT)


def _rms(xf, g):
    return xf * lax.rsqrt(jnp.mean(xf * xf, axis=-1, keepdims=True) + NORM_EPS) * g


def _dot(a, b):
    return jnp.dot(a.astype(BF16), b.astype(BF16), preferred_element_type=F32)


def _dot_nt(a, b):
    return lax.dot_general(a.astype(BF16), b.astype(BF16), (((1,), (1,)), ((), ())),
                           preferred_element_type=F32)


def _dot_tn(a, b):
    return lax.dot_general(a.astype(BF16), b.astype(BF16), (((0,), (0,)), ((), ())),
                           preferred_element_type=F32)


def _sigmoid(x):
    return 0.5 * jnp.tanh(0.5 * x) + 0.5


def _silu(x):
    return x * _sigmoid(x)


def _softplus(x):
    return jnp.maximum(x, 0.0) + jnp.log(1.0 + jnp.exp(-jnp.abs(x)))


def _gelu_tanh(x):
    c = 0.7978845608028654
    return 0.5 * x * (1.0 + jnp.tanh(c * (x + 0.044715 * (x * x * x))))


def _in_proj_kernel(x_ref, g_ref, w_ref, ws_ref, o_ref, os_ref, su_ref, hn_ref):
    j = pl.program_id(1)
    tn = o_ref.shape[1]

    @pl.when(j == 0)
    def _():
        hn = _rms(x_ref[...], g_ref[...]).astype(BF16)
        hn_ref[...] = hn
        os_ref[...] = jnp.dot(hn, ws_ref[...], preferred_element_type=F32)

    acc = jnp.dot(hn_ref[...], w_ref[...], preferred_element_type=F32)
    o_ref[...] = acc.astype(o_ref.dtype)

    @pl.when(jnp.logical_and(j >= _OFF_SU // tn, j < (_OFF_SU + S5_WIDTH) // tn))
    def _():
        su_ref[...] = acc


def _in_proj(x2, g, w_big, w_small, layer, tm=1024, tn=1024):
    t, d = x2.shape
    n = w_big.shape[2]
    ns = w_small.shape[2]
    j0, nsu = _OFF_SU // tn, S5_WIDTH // tn
    return pl.pallas_call(
        _in_proj_kernel,
        out_shape=(jax.ShapeDtypeStruct((t, n), BF16), jax.ShapeDtypeStruct((t, ns), F32),
                   jax.ShapeDtypeStruct((t, S5_WIDTH), F32)),
        grid=(t // tm, n // tn),
        in_specs=[
            pl.BlockSpec((tm, d), lambda i, j: (i, 0)),
            pl.BlockSpec((1, d), lambda i, j: (0, 0)),
            pl.BlockSpec((None, d, tn), lambda i, j: (layer, 0, j)),
            pl.BlockSpec((None, d, ns), lambda i, j: (layer, 0, 0)),
        ],
        out_specs=(
            pl.BlockSpec((tm, tn), lambda i, j: (i, j)),
            pl.BlockSpec((tm, ns), lambda i, j: (i, 0)),
            pl.BlockSpec((tm, tn), lambda i, j: (i, jnp.clip(j - j0, 0, nsu - 1))),
        ),
        scratch_shapes=[pltpu.VMEM((tm, d), BF16)],
        compiler_params=_cparams(("parallel", "arbitrary")),
        name="in_proj",
    )(x2, g, w_big, w_small)


def _norm_proj_kernel(x_ref, g_ref, w_ref, o_ref):
    hn = _rms(x_ref[...], g_ref[...]).astype(BF16)
    o_ref[...] = jnp.dot(hn, w_ref[...], preferred_element_type=F32).astype(o_ref.dtype)


def _norm_proj(x2, g, w, layer, tm=512):
    t, d = x2.shape
    n = w.shape[2]
    return pl.pallas_call(
        _norm_proj_kernel,
        out_shape=jax.ShapeDtypeStruct((t, n), BF16),
        grid=(t // tm,),
        in_specs=[
            pl.BlockSpec((tm, d), lambda i: (i, 0)),
            pl.BlockSpec((1, d), lambda i: (0, 0)),
            pl.BlockSpec((None, d, n), lambda i: (layer, 0, 0)),
        ],
        out_specs=pl.BlockSpec((tm, n), lambda i: (i, 0)),
        compiler_params=_cparams(("parallel",)),
        name="mem_kv_proj",
    )(x2, g, w)


def _dot_hilo_rhs(a01, b):
    hi = b.astype(BF16)
    lo = (b - hi.astype(F32)).astype(BF16)
    return (jnp.dot(a01, hi, preferred_element_type=F32) + jnp.dot(a01, lo, preferred_element_type=F32))


def _dot3(a, b):
    ah = a.astype(BF16)
    al = (a - ah.astype(F32)).astype(BF16)
    bh = b.astype(BF16)
    bl = (b - bh.astype(F32)).astype(BF16)
    return (jnp.dot(ah, bh, preferred_element_type=F32)
            + (jnp.dot(ah, bl, preferred_element_type=F32) + jnp.dot(al, bh, preferred_element_type=F32)))


def _dot_hilo_lhs(a, b01):
    hi = a.astype(BF16)
    lo = (a - hi.astype(F32)).astype(BF16)
    return (jnp.dot(hi, b01, preferred_element_type=F32) + jnp.dot(lo, b01, preferred_element_type=F32))


def _deltanet_kernel(q_ref, k_ref, v_ref, zg_ref, zs_ref, alr_ref, dtr_ref, alc_ref,
                     dtc_ref, cwq_ref, cwk_ref, cwv_ref, nw_ref, o_ref, state_ref, tail_ref, ext_ref):
    tt = q_ref.shape[0]
    c = CHUNK
    hg = DN_HG
    dk, dv = DN_DK, DN_DV
    halo = SUBLANE
    nchunk = tt // c

    @pl.when(pl.program_id(2) == 0)
    def _():
        state_ref[...] = jnp.zeros_like(state_ref)
        tail_ref[...] = jnp.zeros_like(tail_ref)

    def conv_silu(idx, raw_ref, cw_ref):
        raw = raw_ref[...].astype(F32)
        ext_ref[idx, 0:halo, :] = tail_ref[idx]
        ext_ref[idx, halo:halo + tt, :] = raw
        tail_ref[idx] = raw[tt - halo:tt, :]
        cw = cw_ref[...]
        base = halo - (DN_CONV - 1)
        y = ext_ref[idx, base:base + tt, :] * cw[0:1, :]
        for j in range(1, DN_CONV):
            y = y + ext_ref[idx, base + j:base + j + tt, :] * cw[j:j + 1, :]
        return _silu(y)

    q_all = conv_silu(0, q_ref, cwq_ref)
    k_all = conv_silu(1, k_ref, cwk_ref)
    v_all = conv_silu(2, v_ref, cwv_ref)

    grp = pl.program_id(1)
    gcol = pltpu.roll(zs_ref[...], (LANE - 2 * hg * grp) % LANE, 1)
    grow = gcol.T[0:SUBLANE, :]
    g_rows = -jnp.exp(alr_ref[...]) * _softplus(grow + dtr_ref[...])
    beta_cols = _sigmoid(gcol)
    g_cols = -jnp.exp(alc_ref[...]) * _softplus(gcol + dtc_ref[...])

    ri = lax.broadcasted_iota(jnp.int32, (c, c), 0)
    ci = lax.broadcasted_iota(jnp.int32, (c, c), 1)
    causal = ri >= ci
    strict = ri > ci
    ident = jnp.where(ri == ci, 1.0, 0.0).astype(F32)
    tril01 = jnp.where(causal, 1.0, 0.0).astype(BF16)
    triu01 = jnp.where(ri <= ci, 1.0, 0.0).astype(BF16)

    items = [(hh, n) for n in range(nchunk) for hh in range(hg)]
    qs, lower, a_qk, e_col, k_dec, g_last, vb = {}, {}, {}, {}, {}, {}, {}
    for n in range(nchunk):
        sl = slice(n * c, (n + 1) * c)
        gcs_c = _dot_hilo_rhs(tril01, g_cols[sl])
        gcs_r = _dot_hilo_lhs(g_rows[:, sl], triu01)
        for hh in range(hg):
            it = (hh, n)
            hs = slice(hh * dk, (hh + 1) * dk)
            qc, kc = q_all[sl, hs], k_all[sl, hs]
            qc = qc * (lax.rsqrt(jnp.sum(qc * qc, axis=-1, keepdims=True) + NORM_EPS) * (dk ** -0.5))
            kc = kc * lax.rsqrt(jnp.sum(kc * kc, axis=-1, keepdims=True) + NORM_EPS)
            beta_col = beta_cols[sl, hh:hh + 1]
            gc_col = gcs_c[:, hg + hh:hg + hh + 1]
            gc_row = gcs_r[hg + hh:hg + hh + 1, :]
            gc_end = gcs_r[hg + hh:hg + hh + 1, c - 1:c]
            decay = jnp.where(causal, jnp.exp(jnp.where(causal, gc_col - gc_row, 0.0)), 0.0)
            kb = kc * beta_col
            res = _dot_nt(jnp.concatenate([kb, qc], axis=0), kc)
            lower[it] = jnp.where(strict, res[0:c] * decay, 0.0)
            a_qk[it] = jnp.where(causal, res[c:2 * c] * decay, 0.0)
            e_col[it] = jnp.exp(gc_col)
            k_dec[it] = kc * jnp.exp(gc_end - gc_col)
            g_last[it] = jnp.exp(gc_end)
            vb[it] = jnp.concatenate([v_all[sl, hh * dv:(hh + 1) * dv] * beta_col, kb * e_col[it]], axis=1)
            qs[it] = qc

    base_shift = DN_BASE.bit_length() - 1
    same_base = (ri >> base_shift) == (ci >> base_shift)
    inv, pw = {}, {}
    for it in items:
        lb = jnp.where(same_base, lower[it], 0.0).astype(BF16)
        inv[it] = ident - lb.astype(F32)
        pw[it] = jnp.dot(lb, lb, preferred_element_type=F32)
    span = 2
    while span < DN_BASE:
        for it in items:
            if span * 2 < DN_BASE:
                res = _dot3(jnp.concatenate([inv[it], pw[it]], axis=0), pw[it])
                inv[it] = inv[it] + res[0:c]
                pw[it] = res[c:2 * c]
            else:
                inv[it] = inv[it] + _dot3(inv[it], pw[it])
        span *= 2
    width = 2 * DN_BASE
    while width <= c:
        sh = width.bit_length() - 1
        same_w = (ri >> sh) == (ci >> sh)
        same_half = (ri >> (sh - 1)) == (ci >> (sh - 1))
        off = {it: jnp.where(same_w, jnp.where(same_half, 0.0, lower[it]), 0.0).astype(BF16)
               for it in items}
        mid = {it: _dot_hilo_rhs(off[it], inv[it]) for it in items}
        for it in items:
            inv[it] = inv[it] - _dot3(inv[it], mid[it])
        width *= 2

    uw = {it: _dot3(inv[it], vb[it]) for it in items}
    a_uw = {it: _dot(a_qk[it], uw[it]) for it in items}
    k_uw = {it: _dot_tn(k_dec[it], uw[it]) for it in items}

    nw = nw_ref[...]
    state = [state_ref[hh] for hh in range(hg)]
    for n in range(nchunk):
        sl = slice(n * c, (n + 1) * c)
        for hh in range(hg):
            it = (hh, n)
            q_eff = qs[it] * e_col[it] - a_uw[it][:, dv:dv + dk]
            res = _dot(jnp.concatenate([q_eff, k_uw[it][:, dv:dv + dk]], axis=0), state[hh])
            o = res[0:c] + a_uw[it][:, 0:dv]
            state[hh] = state[hh] * g_last[it] - res[c:c + dk] + k_uw[it][:, 0:dv]
            hs = slice(hh * dv, (hh + 1) * dv)
            o = _rms(o, nw) * _silu(zg_ref[sl, hs].astype(F32))
            o_ref[sl, hs] = o.astype(o_ref.dtype)
    for hh in range(hg):
        state_ref[hh] = state[hh]


def _deltanet(z3, zs3, dn_a_log, dn_dt_bias, conv_w_t, norm_w, tt=256):
    b, s, _ = z3.shape
    nh, hg = DN_HEADS, DN_HG
    ngrp = nh // hg
    wq = hg * DN_DK
    bq, bk, bv, bz = _OFF_DQ // wq, _OFF_DK // wq, _OFF_DV // wq, _OFF_DZ // wq
    al = dn_a_log.astype(F32).reshape(ngrp, hg)
    dtb = dn_dt_bias.astype(F32).reshape(ngrp, hg)
    alr = jnp.pad(al, ((0, 0), (hg, SUBLANE - 2 * hg)))[:, :, None]
    dtr = jnp.pad(dtb, ((0, 0), (hg, SUBLANE - 2 * hg)))[:, :, None]
    alc = jnp.pad(al, ((0, 0), (hg, LANE - 2 * hg)))[:, None, :]
    dtc = jnp.pad(dtb, ((0, 0), (hg, LANE - 2 * hg)))[:, None, :]
    return pl.pallas_call(
        _deltanet_kernel,
        out_shape=jax.ShapeDtypeStruct((b, s, nh * DN_DV), BF16),
        grid=(b, ngrp, s // tt),
        in_specs=[
            pl.BlockSpec((None, tt, wq), lambda bi, g, i: (bi, i, bq + g)),
            pl.BlockSpec((None, tt, wq), lambda bi, g, i: (bi, i, bk + g)),
            pl.BlockSpec((None, tt, wq), lambda bi, g, i: (bi, i, bv + g)),
            pl.BlockSpec((None, tt, wq), lambda bi, g, i: (bi, i, bz + g)),
            pl.BlockSpec((None, tt, LANE), lambda bi, g, i: (bi, i, 0)),
            pl.BlockSpec((None, SUBLANE, 1), lambda bi, g, i: (g, 0, 0)),
            pl.BlockSpec((None, SUBLANE, 1), lambda bi, g, i: (g, 0, 0)),
            pl.BlockSpec((None, 1, LANE), lambda bi, g, i: (g, 0, 0)),
            pl.BlockSpec((None, 1, LANE), lambda bi, g, i: (g, 0, 0)),
            pl.BlockSpec((DN_CONV, wq), lambda bi, g, i: (0, g)),
            pl.BlockSpec((DN_CONV, wq), lambda bi, g, i: (0, ngrp + g)),
            pl.BlockSpec((DN_CONV, wq), lambda bi, g, i: (0, 2 * ngrp + g)),
            pl.BlockSpec((1, DN_DV), lambda bi, g, i: (0, 0)),
        ],
        out_specs=pl.BlockSpec((None, tt, wq), lambda bi, g, i: (bi, i, g)),
        scratch_shapes=[
            pltpu.VMEM((hg, DN_DK, DN_DV), F32),
            pltpu.VMEM((3, SUBLANE, wq), F32),
            pltpu.VMEM((3, tt + SUBLANE, wq), F32),
        ],
        compiler_params=_cparams(("parallel", "parallel", "arbitrary")),
        name="deltanet",
    )(z3, z3, z3, z3, zs3, alr, dtr, alc, dtc, conv_w_t, conv_w_t, conv_w_t, norm_w)


def _gla_kernel(q_ref, k_ref, v_ref, r_ref, zs_ref, wup_ref, bup_ref, nw_ref, o_ref, state_ref):
    tt = q_ref.shape[0]
    c = CHUNK
    sub = GLA_SUB

    @pl.when(pl.program_id(2) == 0)
    def _():
        state_ref[...] = jnp.zeros_like(state_ref)

    logit = _dot(zs_ref[...], wup_ref[...]) + bup_ref[...]
    log_a = -_softplus(-logit) * (1.0 / GLA_TAU)

    ri = lax.broadcasted_iota(jnp.int32, (c, c), 0)
    ci = lax.broadcasted_iota(jnp.int32, (c, c), 1)
    tril = jnp.where(ri >= ci, 1.0, 0.0).astype(BF16)
    grp_rows = lax.broadcasted_iota(jnp.int32, (SUBLANE, 1), 0)
    key_rows = lax.broadcasted_iota(jnp.int32, (c, 1), 0)
    grp_cols = lax.broadcasted_iota(jnp.int32, (SUBLANE, c), 1)

    nw = nw_ref[...]
    scale = GLA_DK ** -0.5
    nchunk = tt // c
    nblk = c // sub
    chunks = range(nchunk)
    bcum = [_dot_hilo_rhs(tril, log_a[n * c:(n + 1) * c]) for n in chunks]
    qc = [q_ref[n * c:(n + 1) * c, :].astype(F32) * scale for n in chunks]
    kc = [k_ref[n * c:(n + 1) * c, :].astype(F32) for n in chunks]
    vc = [v_ref[n * c:(n + 1) * c, :] for n in chunks]
    b_last = [bcum[n][c - 1:c, :] for n in chunks]
    q_dec = [qc[n] * jnp.exp(bcum[n]) for n in chunks]
    k_dec = [kc[n] * jnp.exp(b_last[n] - bcum[n]) for n in chunks]
    kv = [_dot_tn(vc[n], k_dec[n]) for n in chunks]

    off = {}
    for n in chunks:
        for bi in range(1, nblk):
            r0 = bi * sub
            b_ref = bcum[n][r0 - 1:r0, :]
            early = key_rows < r0
            q_f = qc[n][r0:r0 + sub] * jnp.exp(bcum[n][r0:r0 + sub] - b_ref)
            k_f = jnp.where(early, kc[n] * jnp.exp(jnp.where(early, b_ref - bcum[n], 0.0)), 0.0)
            off[n, bi] = _dot_nt(q_f, k_f)

    a_qk = []
    for n in chunks:
        row_blocks = []
        for bi in range(nblk):
            r0 = bi * sub
            blk = off[n, bi] if bi > 0 else jnp.zeros((sub, c), F32)
            for g0 in range(0, sub, SUBLANE):
                q_g = qc[n][r0 + g0:r0 + g0 + SUBLANE]
                b_g = bcum[n][r0 + g0:r0 + g0 + SUBLANE]
                part = blk[g0:g0 + SUBLANE]
                for j in range(g0 + SUBLANE):
                    kj = kc[n][r0 + j:r0 + j + 1, :]
                    bj = bcum[n][r0 + j:r0 + j + 1, :]
                    live = grp_rows + g0 >= j
                    dec = jnp.exp(jnp.where(live, b_g - bj, 0.0))
                    col = jnp.sum(jnp.where(live, q_g * kj * dec, 0.0), axis=1, keepdims=True)
                    part = jnp.where(grp_cols == r0 + j, col, part)
                row_blocks.append(part)
        a_qk.append(jnp.concatenate(row_blocks, axis=0))

    intra = [_dot(a_qk[n], vc[n]) for n in chunks]
    state_t = state_ref[...]
    for n in chunks:
        sl = slice(n * c, (n + 1) * c)
        o = _dot_nt(q_dec[n], state_t) + intra[n]
        state_t = state_t * jnp.exp(b_last[n]) + kv[n]
        o = _rms(o, nw) * _silu(r_ref[sl, :].astype(F32))
        o_ref[sl, :] = o.astype(o_ref.dtype)
    state_ref[...] = state_t


def _gla(z3, zs3, w_up_pad, b_up, norm_w, tt=256):
    b, s, _ = z3.shape
    hq, hk = _OFF_GQ // GLA_DK, _OFF_GK // GLA_DK
    hv, hr = _OFF_GV // GLA_DV, _OFF_GR // GLA_DV
    ns = zs3.shape[-1]
    return pl.pallas_call(
        _gla_kernel,
        out_shape=jax.ShapeDtypeStruct((b, s, GLA_HEADS * GLA_DV), BF16),
        grid=(b, GLA_HEADS, s // tt),
        in_specs=[
            pl.BlockSpec((None, tt, GLA_DK), lambda bi, h, i: (bi, i, hq + h)),
            pl.BlockSpec((None, tt, GLA_DK), lambda bi, h, i: (bi, i, hk + h)),
            pl.BlockSpec((None, tt, GLA_DV), lambda bi, h, i: (bi, i, hv + h)),
            pl.BlockSpec((None, tt, GLA_DV), lambda bi, h, i: (bi, i, hr + h)),
            pl.BlockSpec((None, tt, ns), lambda bi, h, i: (bi, i, 0)),
            pl.BlockSpec((ns, GLA_DK), lambda bi, h, i: (0, h)),
            pl.BlockSpec((1, GLA_DK), lambda bi, h, i: (0, h)),
            pl.BlockSpec((1, GLA_DV), lambda bi, h, i: (0, 0)),
        ],
        out_specs=pl.BlockSpec((None, tt, GLA_DV), lambda bi, h, i: (bi, i, h)),
        scratch_shapes=[pltpu.VMEM((GLA_DV, GLA_DK), F32)],
        compiler_params=_cparams(("parallel", "parallel", "arbitrary")),
        name="gla",
    )(z3, z3, z3, z3, zs3, w_up_pad, b_up, norm_w)


def _s5_kernel(su_ref, kblk_ref, vre_ref, vim_ref, wre_ref, wim_ref, al_ref, d_ref,
               o_ref, m_ref, v_ref, w_ref, e_ref, xp_ref):
    ll = S5_L
    rows = e_ref.shape[0]
    nst = S5_LB * S5_STATE

    @pl.when(pl.program_id(1) == 0)
    def _():
        zero_blk = jnp.zeros((LANE, LANE), BF16)
        for s in range(ll):
            for i in range(ll):
                m_ref[s * LANE:(s + 1) * LANE, i * LANE:(i + 1) * LANE] = (
                    kblk_ref[i - s] if i >= s else zero_blk)
        rblk = 2 * LANE
        vr = lax.broadcasted_iota(jnp.int32, (rblk, nst), 0)
        vc = lax.broadcasted_iota(jnp.int32, (rblk, nst), 1)
        keep_v = ((vr & (LANE - 1)) >> 4) == (vc >> 6)
        for part, vt_ref in enumerate((vre_ref, vim_ref)):
            vt = vt_ref[...]
            v2 = jnp.concatenate([vt, vt], axis=0).T
            for r in range(ll * LANE // rblk):
                rs = slice(r * rblk, (r + 1) * rblk)
                v_ref[rs, part * nst:(part + 1) * nst] = jnp.where(
                    keep_v, jnp.tile(v2[rs, :], (1, nst // LANE)), 0.0).astype(BF16)
        wr = lax.broadcasted_iota(jnp.int32, (nst, rblk), 0)
        wc = lax.broadcasted_iota(jnp.int32, (nst, rblk), 1)
        keep_w = (wr >> 6) == ((wc & (LANE - 1)) >> 4)
        for r in range(ll * LANE // rblk):
            cs = slice(r * rblk, (r + 1) * rblk)
            w_ref[0:nst, cs] = jnp.where(keep_w, jnp.tile(wre_ref[:, cs], (S5_LB, 1)), 0.0).astype(BF16)
            w_ref[nst:2 * nst, cs] = jnp.where(keep_w, jnp.tile(wim_ref[:, cs], (S5_LB, 1)), 0.0).astype(BF16)

    u_list = [su_ref[pl.ds(s, rows, stride=ll), :] for s in range(ll)]
    u_cat = jnp.concatenate([u.astype(BF16) for u in u_list], axis=1)
    e_ref[...] = jnp.dot(u_cat, v_ref[...], preferred_element_type=F32)
    ar = al_ref[:, 0:nst]
    ai = al_ref[:, nst:2 * nst]

    def step(cidx, carry):
        xr, xi = carry
        row = pl.ds(cidx, 1)
        xp_ref[row, 0:nst] = xr
        xp_ref[row, nst:2 * nst] = xi
        er = e_ref[row, 0:nst]
        ei = e_ref[row, nst:2 * nst]
        return ar * xr - ai * xi + er, ar * xi + ai * xr + ei

    zero = jnp.zeros((1, nst), F32)
    lax.fori_loop(0, rows, step, (zero, zero), unroll=8)

    pair = 2 * LANE
    intra = [jnp.dot(u_cat[:, 0:(m + 1) * pair], m_ref[0:(m + 1) * pair, m * pair:(m + 1) * pair],
                     preferred_element_type=F32) for m in range(ll // 2)]
    y = (jnp.concatenate(intra, axis=1)
         + jnp.dot(xp_ref[...].astype(BF16), w_ref[...], preferred_element_type=F32))
    d = d_ref[...]
    for i in range(ll):
        yi = y[:, i * LANE:(i + 1) * LANE] + d * u_list[i]
        o_ref[pl.ds(i, rows, stride=ll), :] = _gelu_tanh(yi)


def _s5(su2, mats, layer, nbatch):
    kblk, vre, vim, wre, wim, al, dskip = mats
    t, width = su2.shape
    s = t // nbatch
    ll = S5_L
    rows = s // ll
    nblk = width // LANE
    nst2 = 2 * S5_LB * S5_STATE

    return pl.pallas_call(
        _s5_kernel,
        out_shape=jax.ShapeDtypeStruct((t, width), F32),
        grid=(nblk, nbatch),
        in_specs=[
            pl.BlockSpec((s, LANE), lambda j, b: (b, j)),
            pl.BlockSpec((None, None, ll, LANE, LANE), lambda j, b: (layer, j, 0, 0, 0)),
            pl.BlockSpec((None, None, S5_STATE, ll * LANE), lambda j, b: (layer, j, 0, 0)),
            pl.BlockSpec((None, None, S5_STATE, ll * LANE), lambda j, b: (layer, j, 0, 0)),
            pl.BlockSpec((None, None, S5_STATE, ll * LANE), lambda j, b: (layer, j, 0, 0)),
            pl.BlockSpec((None, None, S5_STATE, ll * LANE), lambda j, b: (layer, j, 0, 0)),
            pl.BlockSpec((None, None, 1, nst2), lambda j, b: (layer, j, 0, 0)),
            pl.BlockSpec((None, None, 1, LANE), lambda j, b: (layer, j, 0, 0)),
        ],
        out_specs=pl.BlockSpec((s, LANE), lambda j, b: (b, j)),
        scratch_shapes=[
            pltpu.VMEM((ll * LANE, ll * LANE), BF16),
            pltpu.VMEM((ll * LANE, nst2), BF16),
            pltpu.VMEM((nst2, ll * LANE), BF16),
            pltpu.VMEM((rows, nst2), F32),
            pltpu.VMEM((rows, nst2), F32),
        ],
        compiler_params=_cparams(("parallel", "arbitrary")),
        name="s5",
    )(su2, kblk, vre, vim, wre, wim, al, dskip)


def _s5_matrices(lam_re, lam_im, log_step, b_re, b_im, c_re, c_im, d_skip):
    g, p, hh, ll, lb = S5_GROUPS, S5_STATE, S5_GROUP, S5_L, S5_LB
    nb = g // lb
    lr = jnp.minimum(lam_re.astype(F32), -1e-4)
    li = lam_im.astype(F32)
    dt = jnp.exp(log_step.astype(F32))[:, None]
    mag = jnp.exp(lr * dt)
    ar = mag * jnp.cos(li * dt)
    ai = mag * jnp.sin(li * dt)
    nr = ar - 1.0
    den = lr * lr + li * li
    zr = (nr * lr + ai * li) / den
    zi = (ai * lr - nr * li) / den
    br = b_re.astype(F32)
    bi = b_im.astype(F32)
    bbr = zr[..., None] * br - zi[..., None] * bi
    bbi = zr[..., None] * bi + zi[..., None] * br
    pr = [jnp.ones_like(ar)]
    pi = [jnp.zeros_like(ai)]
    for _ in range(ll):
        pr.append(pr[-1] * ar - pi[-1] * ai)
        pi.append(pr[-2] * ai + pi[-1] * ar)
    pr_rev = jnp.stack(pr[ll - 1::-1], axis=0)
    pi_rev = jnp.stack(pi[ll - 1::-1], axis=0)
    pr = jnp.stack(pr, axis=0)
    pi = jnp.stack(pi, axis=0)
    cr = c_re.astype(F32)
    cim = c_im.astype(F32)

    def blk(a, *shape):
        return a.reshape((nb, lb) + shape)

    pr_d = jnp.moveaxis(pr.reshape(ll + 1, nb, lb, p), 0, 1)[:, :, :, None, None, :]
    pi_d = jnp.moveaxis(pi.reshape(ll + 1, nb, lb, p), 0, 1)[:, :, :, None, None, :]
    cr_b = blk(cr, hh, p)[:, None, :, None, :, :]
    ci_b = blk(cim, hh, p)[:, None, :, None, :, :]
    qr = cr_b * pr_d - ci_b * pi_d
    qi = cr_b * pi_d + ci_b * pr_d
    bbr_b = blk(jnp.swapaxes(bbr, 1, 2), hh, p)[:, None, :, :, None, :]
    bbi_b = blk(jnp.swapaxes(bbi, 1, 2), hh, p)[:, None, :, :, None, :]
    kd = (jnp.einsum('ndgop,ngip->ndgio', qr[:, :ll, :, 0], bbr_b[:, 0, :, :, 0])
          - jnp.einsum('ndgop,ngip->ndgio', qi[:, :ll, :, 0], bbi_b[:, 0, :, :, 0]))
    spread = jnp.tile(jnp.eye(hh, dtype=F32), (1, lb))
    kfull = jnp.einsum('ndro,oc->ndrc', kd.reshape(nb, ll, lb * hh, hh), spread)
    rg = lax.broadcasted_iota(jnp.int32, (lb * hh, lb * hh), 0) // hh
    cg = lax.broadcasted_iota(jnp.int32, (lb * hh, lb * hh), 1) // hh
    kblk = jnp.where(rg == cg, kfull, 0.0)
    prs = jnp.moveaxis(pr_rev.reshape(ll, nb, lb, p), 0, 1)[:, :, :, None, :]
    pis = jnp.moveaxis(pi_rev.reshape(ll, nb, lb, p), 0, 1)[:, :, :, None, :]
    bbr_v, bbi_v = bbr_b[:, :, :, :, 0, :], bbi_b[:, :, :, :, 0, :]
    vre = jnp.moveaxis(prs * bbr_v - pis * bbi_v, -1, 1).reshape(nb, p, ll * lb * hh)
    vim = jnp.moveaxis(prs * bbi_v + pis * bbr_v, -1, 1).reshape(nb, p, ll * lb * hh)
    wre = jnp.moveaxis(qr[:, 1:ll + 1, :, 0], -1, 1).reshape(nb, p, ll * lb * hh)
    wim = -jnp.moveaxis(qi[:, 1:ll + 1, :, 0], -1, 1).reshape(nb, p, ll * lb * hh)

    al = jnp.concatenate([pr[ll].reshape(nb, 1, lb * p), pi[ll].reshape(nb, 1, lb * p)], axis=2)
    dskip = d_skip.astype(F32).reshape(nb, 1, lb * hh)
    return kblk.astype(BF16), vre, vim, wre, wim, al, dskip


def _mix_out_kernel(oa_ref, ob_ref, ys_ref, ga_ref, gb_ref, gc_ref, x_ref,
                    wa_ref, wb_ref, wca_ref, wcb_ref, wm_ref, gpost_ref, o_ref, acc_ref):
    n = pl.program_id(1)

    @pl.when(n == 0)
    def _():
        acc_ref[...] = jnp.zeros_like(acc_ref)

    oa, ob = oa_ref[...], ob_ref[...]
    ys = ys_ref[...].astype(BF16)
    tn = wa_ref.shape[1]
    halves = [slice(h * (tn // 2), (h + 1) * (tn // 2)) for h in range(2)]
    merged = []
    for hs in halves:
        y_a = jnp.dot(oa, wa_ref[:, hs], preferred_element_type=F32)
        y_b = jnp.dot(ob, wb_ref[:, hs], preferred_element_type=F32)
        glu_a = jnp.dot(ys, wca_ref[:, hs], preferred_element_type=F32)
        glu_b = jnp.dot(ys, wcb_ref[:, hs], preferred_element_type=F32)
        y_c = glu_a * _sigmoid(glu_b)
        merged.append((_sigmoid(ga_ref[:, hs].astype(F32)) * y_a + _sigmoid(gb_ref[:, hs].astype(F32)) * y_b
                       + _sigmoid(gc_ref[:, hs].astype(F32)) * y_c).astype(BF16))
    acc_ref[...] += (jnp.dot(merged[0], wm_ref[halves[0], :], preferred_element_type=F32)
                     + jnp.dot(merged[1], wm_ref[halves[1], :], preferred_element_type=F32))

    @pl.when(n == pl.num_programs(1) - 1)
    def _():
        o_ref[...] = x_ref[...] + _rms(acc_ref[...], gpost_ref[...])


def _mix_out(oa, ob, ys, z2, x2, w_dn_out, w_gla_out, w_s5_glu, w_mix_out, g_post, layer, tm=512, tn=512):
    t, d = x2.shape
    nsteps = d // tn
    gblk = _OFF_GATES // tn
    ka, kb, kc = oa.shape[1], ob.shape[1], ys.shape[1]
    return pl.pallas_call(
        _mix_out_kernel,
        out_shape=jax.ShapeDtypeStruct((t, d), F32),
        grid=(t // tm, nsteps),
        in_specs=[
            pl.BlockSpec((tm, ka), lambda i, n: (i, 0)),
            pl.BlockSpec((tm, kb), lambda i, n: (i, 0)),
            pl.BlockSpec((tm, kc), lambda i, n: (i, 0)),
            pl.BlockSpec((tm, tn), lambda i, n: (i, gblk + n)),
            pl.BlockSpec((tm, tn), lambda i, n: (i, gblk + nsteps + n)),
            pl.BlockSpec((tm, tn), lambda i, n: (i, gblk + 2 * nsteps + n)),
            pl.BlockSpec((tm, d), lambda i, n: (i, 0)),
            pl.BlockSpec((None, ka, tn), lambda i, n: (layer, 0, n)),
            pl.BlockSpec((None, kb, tn), lambda i, n: (layer, 0, n)),
            pl.BlockSpec((None, kc, tn), lambda i, n: (layer, 0, n)),
            pl.BlockSpec((None, kc, tn), lambda i, n: (layer, 0, nsteps + n)),
            pl.BlockSpec((None, tn, d), lambda i, n: (layer, n, 0)),
            pl.BlockSpec((1, d), lambda i, n: (0, 0)),
        ],
        out_specs=pl.BlockSpec((tm, d), lambda i, n: (i, 0)),
        scratch_shapes=[pltpu.VMEM((tm, d), F32)],
        compiler_params=_cparams(("parallel", "arbitrary")),
        name="mix_out",
    )(oa, ob, ys, z2, z2, z2, x2, w_dn_out, w_gla_out, w_s5_glu, w_s5_glu, w_mix_out, g_post)


def _xattn_kernel(x_ref, gpre_ref, wq_ref, kv_ref, wo_ref, gpost_ref, o_ref):
    x = x_ref[...]
    hn = _rms(x, gpre_ref[...]).astype(BF16)
    q = jnp.dot(hn, wq_ref[...], preferred_element_type=F32)
    kv = kv_ref[...]
    heads = range(XA_HEADS)
    scores = [_dot_nt(q[:, h * XA_DH:(h + 1) * XA_DH], kv[:, h * XA_DH:(h + 1) * XA_DH])
              * (XA_DH ** -0.5) for h in heads]
    probs = []
    for s in scores:
        s = s - jnp.max(s, axis=-1, keepdims=True)
        p = jnp.exp(s)
        probs.append(p / jnp.sum(p, axis=-1, keepdims=True))
    outs = [_dot(probs[h], kv[:, (XA_HEADS + h) * XA_DH:(XA_HEADS + h + 1) * XA_DH]) for h in heads]
    o = jnp.concatenate(outs, axis=1)
    y = jnp.dot(o.astype(BF16), wo_ref[...], preferred_element_type=F32)
    o_ref[...] = x + _rms(y, gpost_ref[...])


def _xattn(x3, g_pre, w_q, kv3, w_out, g_post, layer, tm=512):
    b, s, d = x3.shape
    m, nkv = kv3.shape[1], kv3.shape[2]
    nq = w_q.shape[2]
    return pl.pallas_call(
        _xattn_kernel,
        out_shape=jax.ShapeDtypeStruct((b, s, d), F32),
        grid=(b, s // tm),
        in_specs=[
            pl.BlockSpec((None, tm, d), lambda bi, i: (bi, i, 0)),
            pl.BlockSpec((1, d), lambda bi, i: (0, 0)),
            pl.BlockSpec((None, d, nq), lambda bi, i: (layer, 0, 0)),
            pl.BlockSpec((None, m, nkv), lambda bi, i: (bi, 0, 0)),
            pl.BlockSpec((None, nq, d), lambda bi, i: (layer, 0, 0)),
            pl.BlockSpec((1, d), lambda bi, i: (0, 0)),
        ],
        out_specs=pl.BlockSpec((None, tm, d), lambda bi, i: (bi, i, 0)),
        compiler_params=_cparams(("parallel", "parallel")),
        name="xattn",
    )(x3, g_pre, w_q, kv3, w_out, g_post)


def _convffn_kernel(x_ref, xh_ref, gpre_ref, wg_ref, wu_ref, cw_ref, wd_ref, gpost_ref,
                    o_ref, hn_ref, hh_ref, gate_ref, acc_ref):
    i = pl.program_id(1)
    f = pl.program_id(2)
    tm = x_ref.shape[0]
    halo = SUBLANE

    @pl.when(f == 0)
    def _():
        hn_ref[...] = _rms(x_ref[...], gpre_ref[...]).astype(BF16)
        keep = jnp.where(i > 0, 1.0, 0.0)
        hh_ref[...] = (_rms(xh_ref[...], gpre_ref[...]) * keep).astype(BF16)
        acc_ref[...] = jnp.zeros_like(acc_ref)

    hn = hn_ref[...]
    wg = wg_ref[...]
    gate_ref[0:halo, :] = jnp.dot(hh_ref[...], wg, preferred_element_type=F32)
    gate_ref[halo:halo + tm, :] = jnp.dot(hn, wg, preferred_element_type=F32)
    up = jnp.dot(hn, wu_ref[...], preferred_element_type=F32)
    cw = cw_ref[...]
    base = halo - (FFN_CONV - 1)
    conv = gate_ref[base:base + tm, :] * cw[0:1, :]
    for j in range(1, FFN_CONV):
        conv = conv + gate_ref[base + j:base + j + tm, :] * cw[j:j + 1, :]
    act = (_gelu_tanh(conv) * up).astype(BF16)
    acc_ref[...] += jnp.dot(act, wd_ref[...], preferred_element_type=F32)

    @pl.when(f == pl.num_programs(2) - 1)
    def _():
        o_ref[...] = x_ref[...] + _rms(acc_ref[...], gpost_ref[...])


def _convffn(x3, g_pre, w_up, conv_w_t, w_down, g_post, layer, tm=512, tf=512):
    b, s, d = x3.shape
    fh = w_down.shape[1]
    nf = fh // tf
    hb = tm // SUBLANE
    return pl.pallas_call(
        _convffn_kernel,
        out_shape=jax.ShapeDtypeStruct((b, s, d), F32),
        grid=(b, s // tm, nf),
        in_specs=[
            pl.BlockSpec((None, tm, d), lambda bi, i, f: (bi, i, 0)),
            pl.BlockSpec((None, SUBLANE, d), lambda bi, i, f: (bi, jnp.maximum(i * hb - 1, 0), 0)),
            pl.BlockSpec((1, d), lambda bi, i, f: (0, 0)),
            pl.BlockSpec((None, d, tf), lambda bi, i, f: (layer, 0, f)),
            pl.BlockSpec((None, d, tf), lambda bi, i, f: (layer, 0, nf + f)),
            pl.BlockSpec((FFN_CONV, tf), lambda bi, i, f: (0, f)),
            pl.BlockSpec((None, tf, d), lambda bi, i, f: (layer, f, 0)),
            pl.BlockSpec((1, d), lambda bi, i, f: (0, 0)),
        ],
        out_specs=pl.BlockSpec((None, tm, d), lambda bi, i, f: (bi, i, 0)),
        scratch_shapes=[
            pltpu.VMEM((tm, d), BF16),
            pltpu.VMEM((SUBLANE, d), BF16),
            pltpu.VMEM((tm + SUBLANE, tf), F32),
            pltpu.VMEM((tm, d), F32),
        ],
        compiler_params=_cparams(("parallel", "parallel", "arbitrary")),
        name="convffn",
    )(x3, x3, g_pre, w_up, w_up, conv_w_t, w_down, g_post)


def _split_w_in(w_in):
    sizes = (DN_HEADS * DN_DK, DN_HEADS * DN_DK, DN_HEADS * DN_DV, DN_HEADS * DN_DV, DN_HEADS,
             DN_HEADS, GLA_HEADS * GLA_DK, GLA_HEADS * GLA_DK, GLA_HEADS * GLA_DV, GLA_RANK,
             GLA_HEADS * GLA_DV, S5_WIDTH, 3 * D_MODEL)
    parts = []
    acc = 0
    for n in sizes:
        parts.append(w_in[..., acc:acc + n])
        acc += n
    (dq, dk, dv, dz, db, da, gq, gk, gv, glow, gr, su, gates) = parts
    o1 = sum(sizes[:4])
    o2, o3 = o1 + sizes[4] + sizes[5], o1 + sum(sizes[4:9])
    o4 = o3 + sizes[9]
    big = jnp.concatenate([w_in[..., :o1], w_in[..., o2:o3], w_in[..., o4:]], axis=-1).astype(BF16)
    pad = jnp.zeros(w_in.shape[:-1] + (LANE - 2 * DN_HEADS - GLA_RANK,), w_in.dtype)
    per_group = []
    for g in range(DN_HEADS // DN_HG):
        per_group += [db[..., g * DN_HG:(g + 1) * DN_HG], da[..., g * DN_HG:(g + 1) * DN_HG]]
    small = jnp.concatenate(per_group + [glow, pad], axis=-1).astype(BF16)
    return big, small


def kernel(x, mem, norm_mix_pre, norm_mix_post, w_in, dn_conv_w, dn_a_log, dn_dt_bias, dn_norm_w, w_dn_out, gla_w_up, gla_b_up, gla_norm_w, w_gla_out, s5_lam_re, s5_lam_im, s5_log_step, s5_b_re, s5_b_im, s5_c_re, s5_c_im, s5_d, w_s5_glu, w_mix_out, norm_xa_pre, norm_xa_post, norm_mem, w_xa_q, w_xa_kv, w_xa_out, norm_ffn_pre, norm_ffn_post, w_ffn_up, ffn_conv_w, w_ffn_down):
    b, s, d = x.shape
    t = b * s
    depth = w_in.shape[0]
    mem2 = mem.reshape(-1, d)
    row = lambda a: a.reshape(1, -1).astype(F32)
    s5_mats = jax.vmap(_s5_matrices)(s5_lam_re, s5_lam_im, s5_log_step, s5_b_re, s5_b_im,
                                     s5_c_re, s5_c_im, s5_d)

    w_big, w_small = _split_w_in(w_in)
    w_dn_out, w_gla_out, w_s5_glu, w_mix_out, w_xa_q, w_xa_kv, w_xa_out, w_ffn_up, w_ffn_down = (
        w.astype(BF16) for w in (w_dn_out, w_gla_out, w_s5_glu, w_mix_out, w_xa_q, w_xa_kv, w_xa_out,
                                 w_ffn_up, w_ffn_down))

    for l in range(depth):
        z2, zs2, su2 = _in_proj(x.reshape(t, d), row(norm_mix_pre[l]), w_big, w_small, l)
        z3 = z2.reshape(b, s, -1)
        zs3 = zs2.reshape(b, s, -1)

        oa = _deltanet(z3, zs3, dn_a_log[l], dn_dt_bias[l], dn_conv_w[l].T.astype(F32),
                       row(dn_norm_w[l]))

        w_up_pad = jnp.zeros((zs2.shape[1], GLA_HEADS * GLA_DK), F32)
        w_up_pad = w_up_pad.at[_SM_GLOW:_SM_GLOW + GLA_RANK].set(gla_w_up[l]).astype(BF16)
        ob = _gla(z3, zs3, w_up_pad, row(gla_b_up[l]), row(gla_norm_w[l]))

        ys = _s5(su2, s5_mats, l, nbatch=b)

        x = _mix_out(oa.reshape(t, -1), ob.reshape(t, -1), ys, z2, x.reshape(t, d),
                     w_dn_out, w_gla_out, w_s5_glu, w_mix_out, row(norm_mix_post[l]), l).reshape(b, s, d)

        kv = _norm_proj(mem2, row(norm_mem[l]), w_xa_kv, l)
        x = _xattn(x, row(norm_xa_pre[l]), w_xa_q, kv.reshape(b, -1, kv.shape[-1]), w_xa_out,
                   row(norm_xa_post[l]), l)

        x = _convffn(x, row(norm_ffn_pre[l]), w_ffn_up, ffn_conv_w[l].T.astype(F32), w_ffn_down,
                     row(norm_ffn_post[l]), l)
    return x
```
